```python
import jax, jax.numpy as jnp
from jax import lax
import numpy as np

D_MODEL = 2048
BATCH = 1
SEQ = 8192
DEPTH = 1

FOX_HEADS = 8
FOX_HEAD_DIM = 128
FOX_WIDTH = FOX_HEADS * FOX_HEAD_DIM
SWA_Q_HEADS = 16
SWA_KV_HEADS = 4
SWA_HEAD_DIM = 64
SWA_GROUP = SWA_Q_HEADS // SWA_KV_HEADS
SWA_WIDTH = SWA_Q_HEADS * SWA_HEAD_DIM
SWA_KV_WIDTH = SWA_KV_HEADS * SWA_HEAD_DIM
WINDOW = 128
Q_BLOCK = 128
LN_EPS = 1e-5
NEG_INF = -1e30
DEEPNORM_ALPHA = (2.0 * DEPTH) ** 0.25
DEEPNORM_BETA = (8.0 * DEPTH) ** -0.25

_SPLIT_SIZES = (FOX_WIDTH, FOX_WIDTH, FOX_WIDTH, FOX_HEADS,
                SWA_WIDTH, SWA_KV_WIDTH, SWA_KV_WIDTH,
                FOX_WIDTH, SWA_WIDTH,
                D_MODEL, D_MODEL)
IN_WIDTH = sum(_SPLIT_SIZES)
SPLIT_IDX = tuple(int(v) for v in np.cumsum(_SPLIT_SIZES)[:-1])

kernel_name = "hybrid_fox_swa_gated_deepnorm_adaln"


def _layer_norm(x):
    xf = x.astype(jnp.float32)
    mu = jnp.mean(xf, axis=-1, keepdims=True)
    var = jnp.mean(jnp.square(xf - mu), axis=-1, keepdims=True)
    return ((xf - mu) * lax.rsqrt(var + LN_EPS)).astype(x.dtype)


def _forgetting_attention(q, k, v, fgate_logit, b_f):
    B, S, H, Dh = q.shape
    log_f = jax.nn.log_sigmoid(fgate_logit.astype(jnp.float32) + b_f.astype(jnp.float32))
    cum = jnp.cumsum(log_f, axis=1)
    cum_k = cum.transpose(0, 2, 1)[:, :, None, :]
    nb = S // Q_BLOCK
    qb = q.reshape(B, nb, Q_BLOCK, H, Dh).transpose(1, 0, 2, 3, 4)
    cqb = cum.reshape(B, nb, Q_BLOCK, H).transpose(1, 0, 2, 3)
    qpos = jnp.arange(S, dtype=jnp.int32).reshape(nb, Q_BLOCK)
    kpos = jnp.arange(S, dtype=jnp.int32)
    scale = Dh ** -0.5

    def block(args):
        qi, ci, pi = args
        s = jnp.einsum('bqhd,bkhd->bhqk', qi, k).astype(jnp.float32) * scale
        s = s + ci.transpose(0, 2, 1)[..., None] - cum_k
        mask = kpos[None, :] <= pi[:, None]
        s = jnp.where(mask[None, None], s, NEG_INF)
        p = jax.nn.softmax(s, axis=-1).astype(v.dtype)
        return jnp.einsum('bhqk,bkhd->bqhd', p, v)

    out = lax.map(block, (qb, cqb, qpos))
    return out.transpose(1, 0, 2, 3, 4).reshape(B, S, H * Dh)


def _sliding_window_attention(q, k, v, sinks):
    B, S, HQ, d = q.shape
    nb = S // WINDOW
    qb = q.reshape(B, nb, WINDOW, SWA_KV_HEADS, SWA_GROUP, d)
    kb = k.reshape(B, nb, WINDOW, SWA_KV_HEADS, d)
    vb = v.reshape(B, nb, WINDOW, SWA_KV_HEADS, d)
    pad = ((0, 0), (1, 0), (0, 0), (0, 0), (0, 0))
    kk = jnp.concatenate([jnp.pad(kb, pad)[:, :-1], kb], axis=2)
    vv = jnp.concatenate([jnp.pad(vb, pad)[:, :-1], vb], axis=2)
    s = jnp.einsum('bnqhgd,bnkhd->bnhgqk', qb, kk).astype(jnp.float32) * (d ** -0.5)
    i = jnp.arange(WINDOW, dtype=jnp.int32)[:, None]
    j = jnp.arange(2 * WINDOW, dtype=jnp.int32)[None, :]
    dist = (i - j + WINDOW)
    n = jnp.arange(nb, dtype=jnp.int32)[:, None, None]
    kpos_abs = n * WINDOW - WINDOW + j[None]
    valid = (dist[None] >= 0) & (dist[None] < WINDOW) & (kpos_abs >= 0)
    slopes = 2.0 ** (-8.0 * (jnp.arange(HQ, dtype=jnp.float32) + 1.0) / HQ)
    slopes = slopes.reshape(SWA_KV_HEADS, SWA_GROUP)
    s = s - slopes[:, :, None, None] * dist.astype(jnp.float32)
    s = jnp.where(valid[None, :, None, None], s, NEG_INF)
    sink = jnp.broadcast_to(sinks.astype(jnp.float32).reshape(SWA_KV_HEADS, SWA_GROUP)[:, :, None, None],
                            s.shape[:-1] + (1,))
    p = jax.nn.softmax(jnp.concatenate([s, sink], axis=-1), axis=-1)[..., :-1].astype(v.dtype)
    out = jnp.einsum('bnhgqk,bnkhd->bnqhgd', p, vv)
    return out.reshape(B, S, HQ * d)


def setup_inputs(seed: int = 0) -> dict:
    key = jax.random.key(seed)
    ks = jax.random.split(key, 13)
    f32 = jnp.float32
    nrm = lambda k, shape, s: jax.random.normal(k, shape, f32) * s
    x = nrm(ks[0], (BATCH, SEQ, D_MODEL), 1.0)
    c = nrm(ks[1], (BATCH, D_MODEL), 1.0)
    w_ada = nrm(ks[2], (DEPTH, D_MODEL, 3 * D_MODEL), 0.5 * D_MODEL ** -0.5)
    b_ada = nrm(ks[3], (DEPTH, 3 * D_MODEL), 0.02)
    w_in = nrm(ks[4], (DEPTH, D_MODEL, IN_WIDTH), D_MODEL ** -0.5)
    b_f = jnp.linspace(2.0, 7.0, FOX_HEADS, dtype=f32)[None, :] + nrm(ks[5], (DEPTH, FOX_HEADS), 0.1)
    attn_sinks = nrm(ks[6], (DEPTH, SWA_Q_HEADS), 1.0)
    w_br_fox = nrm(ks[7], (DEPTH, FOX_WIDTH, D_MODEL), DEEPNORM_BETA * FOX_WIDTH ** -0.5)
    w_br_swa = nrm(ks[8], (DEPTH, SWA_WIDTH, D_MODEL), DEEPNORM_BETA * SWA_WIDTH ** -0.5)
    w_out = nrm(ks[9], (DEPTH, D_MODEL, D_MODEL), DEEPNORM_BETA * D_MODEL ** -0.5)
    ln_g = 1.0 + nrm(ks[10], (DEPTH, D_MODEL), 0.02)
    ln_b = nrm(ks[11], (DEPTH, D_MODEL), 0.02)
    return {"x": x, "c": c, "w_ada": w_ada, "b_ada": b_ada, "w_in": w_in, "b_f": b_f,
            "attn_sinks": attn_sinks, "w_br_fox": w_br_fox, "w_br_swa": w_br_swa,
            "w_out": w_out, "ln_g": ln_g, "ln_b": ln_b}


def reference(x, c, w_ada, b_ada, w_in, b_f, attn_sinks, w_br_fox, w_br_swa, w_out, ln_g, ln_b):
    B, S, D = x.shape
    for l in range(DEPTH):
        ada = c @ w_ada[l] + b_ada[l]
        shift, scale, gate = jnp.split(ada, 3, axis=-1)
        h = _layer_norm(x) * (1.0 + scale[:, None, :]) + shift[:, None, :]
        proj = h @ w_in[l]
        (fq, fk, fv, flog, sq, sk, sv, g_fox, g_swa, m_fox, m_swa) = jnp.split(proj, SPLIT_IDX, axis=-1)
        o_fox = _forgetting_attention(fq.reshape(B, S, FOX_HEADS, FOX_HEAD_DIM),
                                      fk.reshape(B, S, FOX_HEADS, FOX_HEAD_DIM),
                                      fv.reshape(B, S, FOX_HEADS, FOX_HEAD_DIM),
                                      flog, b_f[l])
        o_swa = _sliding_window_attention(sq.reshape(B, S, SWA_Q_HEADS, SWA_HEAD_DIM),
                                          sk.reshape(B, S, SWA_KV_HEADS, SWA_HEAD_DIM),
                                          sv.reshape(B, S, SWA_KV_HEADS, SWA_HEAD_DIM),
                                          attn_sinks[l])
        y_fox = (o_fox * jax.nn.silu(g_fox)) @ w_br_fox[l]
        y_swa = (o_swa * jax.nn.silu(g_swa)) @ w_br_swa[l]
        merged = jax.nn.sigmoid(m_fox) * y_fox + jax.nn.sigmoid(m_swa) * y_swa
        sub = merged @ w_out[l]
        z = DEEPNORM_ALPHA * x + gate[:, None, :] * sub
        x = _layer_norm(z) * ln_g[l] + ln_b[l]
    return x
```

```python
import functools
import math

import numpy as np
import jax
import jax.numpy as jnp
from jax import lax
from jax.experimental import pallas as pl
from jax.experimental.pallas import tpu as pltpu

F32 = jnp.float32
BF16 = jnp.bfloat16

D_MODEL = 2048
SEQ = 8192
FOX_HEADS = 8
FOX_HEAD_DIM = 128
FOX_WIDTH = FOX_HEADS * FOX_HEAD_DIM
SWA_Q_HEADS = 16
SWA_KV_HEADS = 4
SWA_HEAD_DIM = 64
SWA_GROUP = SWA_Q_HEADS // SWA_KV_HEADS
SWA_WIDTH = SWA_Q_HEADS * SWA_HEAD_DIM
SWA_KV_WIDTH = SWA_KV_HEADS * SWA_HEAD_DIM
WINDOW = 128
LN_EPS = 1e-5
NEG_INF = -1e30
DEPTH = 1
DEEPNORM_ALPHA = (2.0 * DEPTH) ** 0.25
LOG2E = math.log2(math.e)

LANES = 128
VMEM_LIMIT = 56 * 1024 * 1024

COL_FQ = 0
COL_FK = 1024
COL_FV = 2048
COL_SQ = 3072
COL_GF = 4096
COL_GS = 5120
COL_MF = 6144
COL_MS = 8192
COL_SK = 10240
COL_SV = 10496
PROJ_WIDTH = 10752

_O_FQ, _O_FK, _O_FV, _O_FLOG = 0, 1024, 2048, 3072
_O_SQ, _O_SK, _O_SV = 3080, 4104, 4360
_O_GF, _O_GS, _O_MF, _O_MS = 4616, 5640, 6664, 8712


def _swa_cols(w):
    r = w.shape[0]
    w4 = w.reshape(r, SWA_KV_HEADS, SWA_GROUP, SWA_HEAD_DIM)
    return w4.transpose(0, 2, 1, 3).reshape(r, SWA_WIDTH)


def _swa_rows(w):
    c = w.shape[1]
    w4 = w.reshape(SWA_KV_HEADS, SWA_GROUP, SWA_HEAD_DIM, c)
    return w4.transpose(1, 0, 2, 3).reshape(SWA_WIDTH, c)


def _proj_col_scale():
    s = np.ones((1, PROJ_WIDTH), np.float32)
    s[0, COL_FQ:COL_FQ + 1024] = FOX_HEAD_DIM ** -0.5 * LOG2E
    s[0, COL_SQ:COL_SQ + 1024] = SWA_HEAD_DIM ** -0.5
    return s


def _ada_kernel(c_ref, w_ref, b_ref, o_ref):
    c8 = jnp.broadcast_to(c_ref[...], (8, D_MODEL))
    r = jnp.dot(c8, w_ref[...], preferred_element_type=F32)
    o_ref[...] = r[0:1, :] + b_ref[...]


def _ada(c, w_ada, b_ada):
    tn = 512
    n = w_ada.shape[1]
    return pl.pallas_call(
        _ada_kernel,
        out_shape=jax.ShapeDtypeStruct((1, n), F32),
        grid=(n // tn,),
        in_specs=[pl.BlockSpec((1, D_MODEL), lambda j: (0, 0)),
                  pl.BlockSpec((D_MODEL, tn), lambda j: (0, j)),
                  pl.BlockSpec((1, tn), lambda j: (0, j))],
        out_specs=pl.BlockSpec((1, tn), lambda j: (0, j)),
        compiler_params=pltpu.CompilerParams(dimension_semantics=("arbitrary",),
                                             vmem_limit_bytes=VMEM_LIMIT),
        name="ada",
    )(c, w_ada, b_ada)


PROJ_TM = 1024
PROJ_TN = 768
PROJ_RC = 256


def _proj_kernel(x_ref, shift_ref, scale_ref, w_ref, wf_ref, cs_ref, o_ref, flog_ref, h_scr):
    j = pl.program_id(1)

    @pl.when(j == 0)
    def _():
        mod = 1.0 + scale_ref[...]
        shift = shift_ref[...]

        def body(r, carry):
            rows = pl.ds(pl.multiple_of(r * PROJ_RC, PROJ_RC), PROJ_RC)
            x = x_ref[rows, :]
            mu = jnp.mean(x, axis=-1, keepdims=True)
            xc = x - mu
            var = jnp.mean(xc * xc, axis=-1, keepdims=True)
            h = xc * lax.rsqrt(var + LN_EPS) * mod + shift
            h_scr[rows, :] = h.astype(BF16)
            return carry

        lax.fori_loop(0, PROJ_TM // PROJ_RC, body, 0)
        flog_ref[...] = jnp.dot(h_scr[...], wf_ref[...], preferred_element_type=F32)

    acc = jnp.dot(h_scr[...], w_ref[...], preferred_element_type=F32)
    o_ref[...] = (acc * cs_ref[...]).astype(BF16)


def _proj(x2, ada, w_main, w_flog, col_scale):
    s = x2.shape[0]
    return pl.pallas_call(
        _proj_kernel,
        out_shape=(jax.ShapeDtypeStruct((s, PROJ_WIDTH), BF16),
                   jax.ShapeDtypeStruct((s, LANES), F32)),
        grid=(s // PROJ_TM, PROJ_WIDTH // PROJ_TN),
        in_specs=[pl.BlockSpec((PROJ_TM, D_MODEL), lambda i, j: (i, 0)),
                  pl.BlockSpec((1, D_MODEL), lambda i, j: (0, 0)),
                  pl.BlockSpec((1, D_MODEL), lambda i, j: (0, 1)),
                  pl.BlockSpec((D_MODEL, PROJ_TN), lambda i, j: (0, j)),
                  pl.BlockSpec((D_MODEL, LANES), lambda i, j: (0, 0)),
                  pl.BlockSpec((1, PROJ_TN), lambda i, j: (0, j))],
        out_specs=(pl.BlockSpec((PROJ_TM, PROJ_TN), lambda i, j: (i, j)),
                   pl.BlockSpec((PROJ_TM, LANES), lambda i, j: (i, 0))),
        scratch_shapes=[pltpu.VMEM((PROJ_TM, D_MODEL), BF16)],
        compiler_params=pltpu.CompilerParams(dimension_semantics=("arbitrary", "arbitrary"),
                                             vmem_limit_bytes=VMEM_LIMIT),
        name="proj",
    )(x2, ada, ada, w_main, w_flog, col_scale)


def _cum_kernel(flog_ref, bf_ref, f_ref):
    s = flog_ref.shape[0]
    lf = jax.nn.log_sigmoid(flog_ref[...] + bf_ref[...])
    acc = lf.T[0:FOX_HEADS, :]
    lane = lax.broadcasted_iota(jnp.int32, acc.shape, 1)
    sh = 1
    while sh < s:
        rolled = pltpu.roll(acc, sh, axis=1)
        acc = acc + jnp.where(lane >= sh, rolled, 0.0)
        sh *= 2
    f_ref[...] = acc


def _cum(flog, bf_pad):
    s = flog.shape[0]
    return pl.pallas_call(
        _cum_kernel,
        out_shape=jax.ShapeDtypeStruct((FOX_HEADS, s), F32),
        in_specs=[pl.BlockSpec((s, LANES), lambda: (0, 0)),
                  pl.BlockSpec((1, LANES), lambda: (0, 0))],
        out_specs=pl.BlockSpec((FOX_HEADS, s), lambda: (0, 0)),
        compiler_params=pltpu.CompilerParams(vmem_limit_bytes=VMEM_LIMIT),
        name="cum",
    )(flog, bf_pad)


FOX_T = 512


def _fox_kernel(q_ref, k_ref, v_ref, f_ref, o_ref, m_scr, acc_scr):
    i = pl.program_id(1)
    t = FOX_T
    q = q_ref[...]
    q0 = pl.multiple_of(i * t, t)
    f_q = f_ref[0, :, pl.ds(q0, t)]
    f_base = jnp.max(f_q, axis=1, keepdims=True)
    ones = jnp.ones((t, LANES), BF16)

    m_scr[...] = jnp.full(m_scr.shape, NEG_INF, F32)
    acc_scr[...] = jnp.zeros(acc_scr.shape, F32)

    def tile(k0, masked):
        k_t = k_ref[pl.ds(k0, t), :]
        v_t = v_ref[pl.ds(k0, t), :]
        s = lax.dot_general(q, k_t, (((1,), (1,)), ((), ())), preferred_element_type=F32)
        bias = (f_base - f_ref[0, :, pl.ds(k0, t)]) * LOG2E
        s = s + bias
        if masked:
            row = lax.broadcasted_iota(jnp.int32, (t, t), 0)
            col = lax.broadcasted_iota(jnp.int32, (t, t), 1)
            s = jnp.where(col <= row, s, NEG_INF)
        m_prev = m_scr[...]
        m_new = jnp.maximum(m_prev, jnp.max(s, axis=1, keepdims=True))
        alpha = jnp.exp2(m_prev - m_new)
        p = jnp.exp2(s - jnp.tile(m_new, (1, t // LANES))).astype(BF16)
        v_aug = jnp.concatenate([v_t, ones], axis=1)
        pv = jnp.dot(p, v_aug, preferred_element_type=F32)
        acc_scr[...] = acc_scr[...] * jnp.tile(alpha, (1, 2)) + pv
        m_scr[...] = m_new

    def body(kj, carry):
        tile(pl.multiple_of(kj * t, t), False)
        return carry

    lax.fori_loop(0, i, body, 0)
    tile(q0, True)

    acc = acc_scr[...]
    o_ref[...] = (acc[:, :FOX_HEAD_DIM] / acc[:, FOX_HEAD_DIM:]).astype(BF16)


def _fox(proj, f3):
    s = proj.shape[0]
    t = FOX_T
    return pl.pallas_call(
        _fox_kernel,
        out_shape=jax.ShapeDtypeStruct((s, FOX_WIDTH), BF16),
        grid=(FOX_HEADS, s // t),
        in_specs=[pl.BlockSpec((t, LANES), lambda h, i: (i, COL_FQ // LANES + h)),
                  pl.BlockSpec((s, LANES), lambda h, i: (0, COL_FK // LANES + h)),
                  pl.BlockSpec((s, LANES), lambda h, i: (0, COL_FV // LANES + h)),
                  pl.BlockSpec((1, 1, s), lambda h, i: (h, 0, 0))],
        out_specs=pl.BlockSpec((t, LANES), lambda h, i: (i, h)),
        scratch_shapes=[pltpu.VMEM((t, LANES), F32), pltpu.VMEM((t, 2 * LANES), F32)],
        compiler_params=pltpu.CompilerParams(dimension_semantics=("arbitrary", "arbitrary"),
                                             vmem_limit_bytes=VMEM_LIMIT),
        name="fox",
    )(proj, proj, proj, f3)


def _swa_kernel(sink_ref, q_ref, kp_ref, kc_ref, vp_ref, vc_ref, o_ref):
    n = pl.program_id(0)
    w = WINDOW
    kk = jnp.concatenate([kp_ref[...], kc_ref[...]], axis=0)
    vv = jnp.concatenate([vp_ref[...], vc_ref[...]], axis=0)
    lane_head = lax.broadcasted_iota(jnp.int32, (w, 2 * LANES), 1) // SWA_HEAD_DIM
    row = lax.broadcasted_iota(jnp.int32, (w, 2 * w), 0)
    col = lax.broadcasted_iota(jnp.int32, (w, 2 * w), 1)
    dist = row - col + w
    valid = (dist >= 0) & (dist < w) & ((col >= w) | (n > 0))
    mask_bias = jnp.where(valid, 0.0, NEG_INF)
    distf = dist.astype(F32)

    for g in range(SWA_GROUP):
        qg = q_ref[:, g * 256:(g + 1) * 256]
        out_g = jnp.zeros((w, 2 * LANES), F32)
        for h in range(SWA_KV_HEADS):
            hq = h * SWA_GROUP + g
            slope = 2.0 ** (-8.0 * (hq + 1.0) / SWA_Q_HEADS)
            sel = lane_head == h
            qm = jnp.where(sel, qg, jnp.zeros_like(qg))
            s = lax.dot_general(qm, kk, (((1,), (1,)), ((), ())), preferred_element_type=F32)
            s = s - slope * distf + mask_bias
            sink = sink_ref[hq]
            m = jnp.maximum(jnp.max(s, axis=1, keepdims=True), sink)
            p = jnp.exp(s - m)
            denom = jnp.sum(p, axis=1, keepdims=True) + jnp.exp(sink - m)
            o = jnp.dot(p.astype(BF16), vv, preferred_element_type=F32)
            out_g = jnp.where(sel, o / denom, out_g)
        o_ref[:, g * 256:(g + 1) * 256] = out_g.astype(BF16)


def _swa(proj, sinks):
    s = proj.shape[0]
    w = WINDOW
    nb = s // w
    kcol = COL_SK // SWA_KV_WIDTH
    vcol = COL_SV // SWA_KV_WIDTH
    grid_spec = pltpu.PrefetchScalarGridSpec(
        num_scalar_prefetch=1,
        grid=(nb,),
        in_specs=[pl.BlockSpec((w, SWA_WIDTH), lambda n, sk: (n, COL_SQ // SWA_WIDTH)),
                  pl.BlockSpec((w, SWA_KV_WIDTH), lambda n, sk: (jnp.maximum(n - 1, 0), kcol)),
                  pl.BlockSpec((w, SWA_KV_WIDTH), lambda n, sk: (n, kcol)),
                  pl.BlockSpec((w, SWA_KV_WIDTH), lambda n, sk: (jnp.maximum(n - 1, 0), vcol)),
                  pl.BlockSpec((w, SWA_KV_WIDTH), lambda n, sk: (n, vcol))],
        out_specs=pl.BlockSpec((w, SWA_WIDTH), lambda n, sk: (n, 0)),
    )
    return pl.pallas_call(
        _swa_kernel,
        out_shape=jax.ShapeDtypeStruct((s, SWA_WIDTH), BF16),
        grid_spec=grid_spec,
        compiler_params=pltpu.CompilerParams(dimension_semantics=("arbitrary",),
                                             vmem_limit_bytes=VMEM_LIMIT),
        name="swa",
    )(sinks, proj, proj, proj, proj, proj)


OUT_TM = 256


def _out_kernel(of_ref, gf_ref, os_ref, gs_ref, mf_ref, ms_ref, x_ref, gate_ref,
                wbf_ref, wbs_ref, wo_ref, lng_ref, lnb_ref, o_ref):
    af = (of_ref[...].astype(F32) * jax.nn.silu(gf_ref[...].astype(F32))).astype(BF16)
    yf = jnp.dot(af, wbf_ref[...], preferred_element_type=F32)
    a_s = (os_ref[...].astype(F32) * jax.nn.silu(gs_ref[...].astype(F32))).astype(BF16)
    ys = jnp.dot(a_s, wbs_ref[...], preferred_element_type=F32)
    merged = (jax.nn.sigmoid(mf_ref[...].astype(F32)) * yf
              + jax.nn.sigmoid(ms_ref[...].astype(F32)) * ys)
    sub = jnp.dot(merged.astype(BF16), wo_ref[...], preferred_element_type=F32)
    z = DEEPNORM_ALPHA * x_ref[...] + gate_ref[...] * sub
    mu = jnp.mean(z, axis=-1, keepdims=True)
    zc = z - mu
    var = jnp.mean(zc * zc, axis=-1, keepdims=True)
    o_ref[...] = zc * lax.rsqrt(var + LN_EPS) * lng_ref[...] + lnb_ref[...]


def _out(proj, o_fox, o_swa, x2, ada, wbf, wbs, wo, ln_g, ln_b):
    s = x2.shape[0]
    tm = OUT_TM
    const = lambda i: (0, 0)
    return pl.pallas_call(
        _out_kernel,
        out_shape=jax.ShapeDtypeStruct((s, D_MODEL), F32),
        grid=(s // tm,),
        in_specs=[pl.BlockSpec((tm, FOX_WIDTH), lambda i: (i, 0)),
                  pl.BlockSpec((tm, FOX_WIDTH), lambda i: (i, COL_GF // FOX_WIDTH)),
                  pl.BlockSpec((tm, SWA_WIDTH), lambda i: (i, 0)),
                  pl.BlockSpec((tm, SWA_WIDTH), lambda i: (i, COL_GS // SWA_WIDTH)),
                  pl.BlockSpec((tm, D_MODEL), lambda i: (i, COL_MF // D_MODEL)),
                  pl.BlockSpec((tm, D_MODEL), lambda i: (i, COL_MS // D_MODEL)),
                  pl.BlockSpec((tm, D_MODEL), lambda i: (i, 0)),
                  pl.BlockSpec((1, D_MODEL), lambda i: (0, 2)),
                  pl.BlockSpec((FOX_WIDTH, D_MODEL), const),
                  pl.BlockSpec((SWA_WIDTH, D_MODEL), const),
                  pl.BlockSpec((D_MODEL, D_MODEL), const),
                  pl.BlockSpec((1, D_MODEL), const),
                  pl.BlockSpec((1, D_MODEL), const)],
        out_specs=pl.BlockSpec((tm, D_MODEL), lambda i: (i, 0)),
        compiler_params=pltpu.CompilerParams(dimension_semantics=("arbitrary",),
                                             vmem_limit_bytes=VMEM_LIMIT),
        name="out",
    )(o_fox, proj, o_swa, proj, proj, proj, x2, ada, wbf, wbs, wo, ln_g, ln_b)


def kernel(x, c, w_ada, b_ada, w_in, b_f, attn_sinks, w_br_fox, w_br_swa, w_out, ln_g, ln_b):
    b, s, d = x.shape
    assert (b, s, d) == (1, SEQ, D_MODEL) and w_in.shape[0] == DEPTH
    x2 = x.reshape(s, d)

    wi = w_in[0]
    seg = lambda o, n: wi[:, o:o + n]
    w_main = jnp.concatenate([
        seg(_O_FQ, 1024), seg(_O_FK, 1024), seg(_O_FV, 1024),
        _swa_cols(seg(_O_SQ, 1024)),
        seg(_O_GF, 1024), _swa_cols(seg(_O_GS, 1024)),
        seg(_O_MF, 2048), seg(_O_MS, 2048),
        seg(_O_SK, 256), seg(_O_SV, 256)], axis=1).astype(BF16)
    w_flog = jnp.pad(seg(_O_FLOG, FOX_HEADS), ((0, 0), (0, LANES - FOX_HEADS))).astype(BF16)
    bf_pad = jnp.pad(b_f[0], (0, LANES - FOX_HEADS)).reshape(1, LANES)
    col_scale = jnp.asarray(_proj_col_scale())
    wbf = w_br_fox[0].astype(BF16)
    wbs = _swa_rows(w_br_swa[0]).astype(BF16)
    wo = w_out[0].astype(BF16)

    ada = _ada(c, w_ada[0], b_ada[0].reshape(1, -1))
    proj, flog = _proj(x2, ada, w_main, w_flog, col_scale)
    f_cum = _cum(flog, bf_pad)
    o_fox = _fox(proj, f_cum.reshape(FOX_HEADS, 1, s))
    o_swa = _swa(proj, attn_sinks[0])
    out = _out(proj, o_fox, o_swa, x2, ada, wbf, wbs, wo,
               ln_g[0].reshape(1, d), ln_b[0].reshape(1, d))
    return out.reshape(b, s, d)
```

```python
import functools
import math

import numpy as np
import jax
import jax.numpy as jnp
from jax import lax
from jax.experimental import pallas as pl
from jax.experimental.pallas import tpu as pltpu

F32 = jnp.float32
BF16 = jnp.bfloat16

D_MODEL = 2048
SEQ = 8192
FOX_HEADS = 8
FOX_HEAD_DIM = 128
FOX_WIDTH = FOX_HEADS * FOX_HEAD_DIM
SWA_Q_HEADS = 16
SWA_KV_HEADS = 4
SWA_HEAD_DIM = 64
SWA_GROUP = SWA_Q_HEADS // SWA_KV_HEADS
SWA_WIDTH = SWA_Q_HEADS * SWA_HEAD_DIM
SWA_KV_WIDTH = SWA_KV_HEADS * SWA_HEAD_DIM
WINDOW = 128
LN_EPS = 1e-5
NEG_INF = -1e30
DEPTH = 1
DEEPNORM_ALPHA = (2.0 * DEPTH) ** 0.25
LOG2E = math.log2(math.e)

LANES = 128
VMEM_LIMIT = 56 * 1024 * 1024

COL_FQ = 0
COL_FK = 1024
COL_FV = 2048
COL_SQ = 3072
COL_GF = 4096
COL_GS = 5120
COL_MF = 6144
COL_MS = 8192
COL_SK = 10240
COL_SV = 10496
PROJ_WIDTH = 10752

_O_FQ, _O_FK, _O_FV, _O_FLOG = 0, 1024, 2048, 3072
_O_SQ, _O_SK, _O_SV = 3080, 4104, 4360
_O_GF, _O_GS, _O_MF, _O_MS = 4616, 5640, 6664, 8712


def _swa_cols(w):
    r = w.shape[0]
    w4 = w.reshape(r, SWA_KV_HEADS, SWA_GROUP, SWA_HEAD_DIM)
    return w4.transpose(0, 2, 1, 3).reshape(r, SWA_WIDTH)


def _swa_rows(w):
    c = w.shape[1]
    w4 = w.reshape(SWA_KV_HEADS, SWA_GROUP, SWA_HEAD_DIM, c)
    return w4.transpose(1, 0, 2, 3).reshape(SWA_WIDTH, c)


def _proj_col_scale():
    s = np.ones((1, PROJ_WIDTH), np.float32)
    s[0, COL_FQ:COL_FQ + 1024] = FOX_HEAD_DIM ** -0.5 * LOG2E
    s[0, COL_SQ:COL_SQ + 1024] = SWA_HEAD_DIM ** -0.5
    return s


def _ada_kernel(c_ref, w_ref, b_ref, o_ref):
    c8 = jnp.broadcast_to(c_ref[...], (8, D_MODEL))
    r = jnp.dot(c8, w_ref[...], preferred_element_type=F32)
    o_ref[...] = r[0:1, :] + b_ref[...]


def _ada(c, w_ada, b_ada):
    tn = 512
    n = w_ada.shape[1]
    return pl.pallas_call(
        _ada_kernel,
        out_shape=jax.ShapeDtypeStruct((1, n), F32),
        grid=(n // tn,),
        in_specs=[pl.BlockSpec((1, D_MODEL), lambda j: (0, 0)),
                  pl.BlockSpec((D_MODEL, tn), lambda j: (0, j)),
                  pl.BlockSpec((1, tn), lambda j: (0, j))],
        out_specs=pl.BlockSpec((1, tn), lambda j: (0, j)),
        compiler_params=pltpu.CompilerParams(dimension_semantics=("arbitrary",),
                                             vmem_limit_bytes=VMEM_LIMIT),
        name="ada",
    )(c, w_ada, b_ada)


PROJ_TM = 1024
PROJ_TN = 768
PROJ_RC = 256


def _proj_kernel(x_ref, shift_ref, scale_ref, w_ref, wf_ref, cs_ref, o_ref, flog_ref, h_scr):
    j = pl.program_id(1)

    @pl.when(j == 0)
    def _():
        mod = 1.0 + scale_ref[...]
        shift = shift_ref[...]

        def body(r, carry):
            rows = pl.ds(pl.multiple_of(r * PROJ_RC, PROJ_RC), PROJ_RC)
            x = x_ref[rows, :]
            mu = jnp.mean(x, axis=-1, keepdims=True)
            xc = x - mu
            var = jnp.mean(xc * xc, axis=-1, keepdims=True)
            h = xc * lax.rsqrt(var + LN_EPS) * mod + shift
            h_scr[rows, :] = h.astype(BF16)
            return carry

        lax.fori_loop(0, PROJ_TM // PROJ_RC, body, 0)
        flog_ref[...] = jnp.dot(h_scr[...], wf_ref[...], preferred_element_type=F32)

    acc = jnp.dot(h_scr[...], w_ref[...], preferred_element_type=F32)
    o_ref[...] = (acc * cs_ref[...]).astype(BF16)


def _proj(x2, ada, w_main, w_flog, col_scale):
    s = x2.shape[0]
    return pl.pallas_call(
        _proj_kernel,
        out_shape=(jax.ShapeDtypeStruct((s, PROJ_WIDTH), BF16),
                   jax.ShapeDtypeStruct((s, LANES), F32)),
        grid=(s // PROJ_TM, PROJ_WIDTH // PROJ_TN),
        in_specs=[pl.BlockSpec((PROJ_TM, D_MODEL), lambda i, j: (i, 0)),
                  pl.BlockSpec((1, D_MODEL), lambda i, j: (0, 0)),
                  pl.BlockSpec((1, D_MODEL), lambda i, j: (0, 1)),
                  pl.BlockSpec((D_MODEL, PROJ_TN), lambda i, j: (0, j)),
                  pl.BlockSpec((D_MODEL, LANES), lambda i, j: (0, 0)),
                  pl.BlockSpec((1, PROJ_TN), lambda i, j: (0, j))],
        out_specs=(pl.BlockSpec((PROJ_TM, PROJ_TN), lambda i, j: (i, j)),
                   pl.BlockSpec((PROJ_TM, LANES), lambda i, j: (i, 0))),
        scratch_shapes=[pltpu.VMEM((PROJ_TM, D_MODEL), BF16)],
        compiler_params=pltpu.CompilerParams(dimension_semantics=("arbitrary", "arbitrary"),
                                             vmem_limit_bytes=VMEM_LIMIT),
        name="proj",
    )(x2, ada, ada, w_main, w_flog, col_scale)


def _cum_kernel(flog_ref, bf_ref, f_ref):
    s = flog_ref.shape[0]
    lf = jax.nn.log_sigmoid(flog_ref[...] + bf_ref[...])
    acc = lf.T[0:FOX_HEADS, :]
    lane = lax.broadcasted_iota(jnp.int32, acc.shape, 1)
    sh = 1
    while sh < s:
        rolled = pltpu.roll(acc, sh, axis=1)
        acc = acc + jnp.where(lane >= sh, rolled, 0.0)
        sh *= 2
    f_ref[...] = acc


def _cum(flog, bf_pad):
    s = flog.shape[0]
    return pl.pallas_call(
        _cum_kernel,
        out_shape=jax.ShapeDtypeStruct((FOX_HEADS, s), F32),
        in_specs=[pl.BlockSpec((s, LANES), lambda: (0, 0)),
                  pl.BlockSpec((1, LANES), lambda: (0, 0))],
        out_specs=pl.BlockSpec((FOX_HEADS, s), lambda: (0, 0)),
        compiler_params=pltpu.CompilerParams(vmem_limit_bytes=VMEM_LIMIT),
        name="cum",
    )(flog, bf_pad)


FOX_T = 512


def _fox_kernel(q_ref, k_ref, v_ref, f_ref, o_ref, sa_scr, sb_scr, m_scr, acc_scr):
    i = pl.program_id(1)
    t = FOX_T
    q0 = pl.multiple_of(i * t, t)
    f_q = f_ref[0, :, pl.ds(q0, t)]
    f_base = jnp.max(f_q, axis=1, keepdims=True)
    ones = jnp.ones((t, LANES), BF16)

    m_scr[...] = jnp.full(m_scr.shape, NEG_INF, F32)
    acc_scr[...] = jnp.zeros(acc_scr.shape, F32)

    def scores(k0, dst):
        k_t = k_ref[pl.ds(k0, t), :]
        s = lax.dot_general(q_ref[...], k_t, (((1,), (1,)), ((), ())), preferred_element_type=F32)
        bias = (f_base - f_ref[0, :, pl.ds(k0, t)]) * LOG2E
        dst[...] = s + bias

    def softmax_pv(src, k0, masked):
        s = src[...]
        if masked:
            rel = (lax.broadcasted_iota(jnp.int32, (t, t), 1)
                   - lax.broadcasted_iota(jnp.int32, (t, t), 0))
            s = jnp.where(rel <= q0 - k0, s, NEG_INF)
        m_prev = m_scr[...]
        m_new = jnp.maximum(m_prev, jnp.max(s, axis=1, keepdims=True))
        alpha = jnp.exp2(m_prev - m_new)
        p = jnp.exp2(s - jnp.tile(m_new, (1, t // LANES))).astype(BF16)
        v_aug = jnp.concatenate([v_ref[pl.ds(k0, t), :], ones], axis=1)
        pv = jnp.dot(p, v_aug, preferred_element_type=F32)
        acc_scr[...] = acc_scr[...] * jnp.tile(alpha, (1, 2)) + pv
        m_scr[...] = m_new

    def pair(ka, masked, last):
        kb = pl.multiple_of(ka + t, t)
        scores(kb, sb_scr)
        softmax_pv(sa_scr, ka, masked)
        if not last:
            scores(pl.multiple_of(ka + 2 * t, t), sa_scr)
        softmax_pv(sb_scr, kb, masked)

    scores(0, sa_scr)

    def body(j, carry):
        pair(pl.multiple_of(j * (2 * t), 2 * t), False, False)
        return carry

    n_full_pairs = i // 2
    lax.fori_loop(0, n_full_pairs, body, 0)
    pair(pl.multiple_of(n_full_pairs * (2 * t), 2 * t), True, True)

    acc = acc_scr[...]
    o_ref[...] = (acc[:, :FOX_HEAD_DIM] / acc[:, FOX_HEAD_DIM:]).astype(BF16)


def _fox(proj, f3):
    s = proj.shape[0]
    t = FOX_T
    return pl.pallas_call(
        _fox_kernel,
        out_shape=jax.ShapeDtypeStruct((s, FOX_WIDTH), BF16),
        grid=(FOX_HEADS, s // t),
        in_specs=[pl.BlockSpec((t, LANES), lambda h, i: (i, COL_FQ // LANES + h)),
                  pl.BlockSpec((s, LANES), lambda h, i: (0, COL_FK // LANES + h)),
                  pl.BlockSpec((s, LANES), lambda h, i: (0, COL_FV // LANES + h)),
                  pl.BlockSpec((1, 1, s), lambda h, i: (h, 0, 0))],
        out_specs=pl.BlockSpec((t, LANES), lambda h, i: (i, h)),
        scratch_shapes=[pltpu.VMEM((t, t), F32), pltpu.VMEM((t, t), F32),
                        pltpu.VMEM((t, LANES), F32), pltpu.VMEM((t, 2 * LANES), F32)],
        compiler_params=pltpu.CompilerParams(dimension_semantics=("arbitrary", "arbitrary"),
                                             vmem_limit_bytes=VMEM_LIMIT),
        name="fox",
    )(proj, proj, proj, f3)


def _swa_kernel(sink_ref, q_ref, kp_ref, kc_ref, vp_ref, vc_ref, o_ref):
    n = pl.program_id(0)
    w = WINDOW
    kk = jnp.concatenate([kp_ref[...], kc_ref[...]], axis=0)
    vv = jnp.concatenate([vp_ref[...], vc_ref[...]], axis=0)
    lane_head = lax.broadcasted_iota(jnp.int32, (w, 2 * LANES), 1) // SWA_HEAD_DIM
    row = lax.broadcasted_iota(jnp.int32, (w, 2 * w), 0)
    col = lax.broadcasted_iota(jnp.int32, (w, 2 * w), 1)
    dist = row - col + w
    valid = (dist >= 0) & (dist < w) & ((col >= w) | (n > 0))
    mask_bias = jnp.where(valid, 0.0, NEG_INF)
    distf = dist.astype(F32)

    for g in range(SWA_GROUP):
        qg = q_ref[:, g * 256:(g + 1) * 256]
        out_g = jnp.zeros((w, 2 * LANES), F32)
        for h in range(SWA_KV_HEADS):
            hq = h * SWA_GROUP + g
            slope = 2.0 ** (-8.0 * (hq + 1.0) / SWA_Q_HEADS)
            sel = lane_head == h
            qm = jnp.where(sel, qg, jnp.zeros_like(qg))
            s = lax.dot_general(qm, kk, (((1,), (1,)), ((), ())), preferred_element_type=F32)
            s = s - slope * distf + mask_bias
            sink = sink_ref[hq]
            m = jnp.maximum(jnp.max(s, axis=1, keepdims=True), sink)
            p = jnp.exp(s - m)
            denom = jnp.sum(p, axis=1, keepdims=True) + jnp.exp(sink - m)
            o = jnp.dot(p.astype(BF16), vv, preferred_element_type=F32)
            out_g = jnp.where(sel, o / denom, out_g)
        o_ref[:, g * 256:(g + 1) * 256] = out_g.astype(BF16)


def _swa(proj, sinks):
    s = proj.shape[0]
    w = WINDOW
    nb = s // w
    kcol = COL_SK // SWA_KV_WIDTH
    vcol = COL_SV // SWA_KV_WIDTH
    grid_spec = pltpu.PrefetchScalarGridSpec(
        num_scalar_prefetch=1,
        grid=(nb,),
        in_specs=[pl.BlockSpec((w, SWA_WIDTH), lambda n, sk: (n, COL_SQ // SWA_WIDTH)),
                  pl.BlockSpec((w, SWA_KV_WIDTH), lambda n, sk: (jnp.maximum(n - 1, 0), kcol)),
                  pl.BlockSpec((w, SWA_KV_WIDTH), lambda n, sk: (n, kcol)),
                  pl.BlockSpec((w, SWA_KV_WIDTH), lambda n, sk: (jnp.maximum(n - 1, 0), vcol)),
                  pl.BlockSpec((w, SWA_KV_WIDTH), lambda n, sk: (n, vcol))],
        out_specs=pl.BlockSpec((w, SWA_WIDTH), lambda n, sk: (n, 0)),
    )
    return pl.pallas_call(
        _swa_kernel,
        out_shape=jax.ShapeDtypeStruct((s, SWA_WIDTH), BF16),
        grid_spec=grid_spec,
        compiler_params=pltpu.CompilerParams(dimension_semantics=("arbitrary",),
                                             vmem_limit_bytes=VMEM_LIMIT),
        name="swa",
    )(sinks, proj, proj, proj, proj, proj)


OUT_TM = 256


def _out_kernel(of_ref, gf_ref, os_ref, gs_ref, mf_ref, ms_ref, x_ref, gate_ref,
                wbf_ref, wbs_ref, wo_ref, lng_ref, lnb_ref, o_ref):
    af = (of_ref[...].astype(F32) * jax.nn.silu(gf_ref[...].astype(F32))).astype(BF16)
    yf = jnp.dot(af, wbf_ref[...], preferred_element_type=F32)
    a_s = (os_ref[...].astype(F32) * jax.nn.silu(gs_ref[...].astype(F32))).astype(BF16)
    ys = jnp.dot(a_s, wbs_ref[...], preferred_element_type=F32)
    merged = (jax.nn.sigmoid(mf_ref[...].astype(F32)) * yf
              + jax.nn.sigmoid(ms_ref[...].astype(F32)) * ys)
    sub = jnp.dot(merged.astype(BF16), wo_ref[...], preferred_element_type=F32)
    z = DEEPNORM_ALPHA * x_ref[...] + gate_ref[...] * sub
    mu = jnp.mean(z, axis=-1, keepdims=True)
    zc = z - mu
    var = jnp.mean(zc * zc, axis=-1, keepdims=True)
    o_ref[...] = zc * lax.rsqrt(var + LN_EPS) * lng_ref[...] + lnb_ref[...]


def _out(proj, o_fox, o_swa, x2, ada, wbf, wbs, wo, ln_g, ln_b):
    s = x2.shape[0]
    tm = OUT_TM
    const = lambda i: (0, 0)
    return pl.pallas_call(
        _out_kernel,
        out_shape=jax.ShapeDtypeStruct((s, D_MODEL), F32),
        grid=(s // tm,),
        in_specs=[pl.BlockSpec((tm, FOX_WIDTH), lambda i: (i, 0)),
                  pl.BlockSpec((tm, FOX_WIDTH), lambda i: (i, COL_GF // FOX_WIDTH)),
                  pl.BlockSpec((tm, SWA_WIDTH), lambda i: (i, 0)),
                  pl.BlockSpec((tm, SWA_WIDTH), lambda i: (i, COL_GS // SWA_WIDTH)),
                  pl.BlockSpec((tm, D_MODEL), lambda i: (i, COL_MF // D_MODEL)),
                  pl.BlockSpec((tm, D_MODEL), lambda i: (i, COL_MS // D_MODEL)),
                  pl.BlockSpec((tm, D_MODEL), lambda i: (i, 0)),
                  pl.BlockSpec((1, D_MODEL), lambda i: (0, 2)),
                  pl.BlockSpec((FOX_WIDTH, D_MODEL), const),
                  pl.BlockSpec((SWA_WIDTH, D_MODEL), const),
                  pl.BlockSpec((D_MODEL, D_MODEL), const),
                  pl.BlockSpec((1, D_MODEL), const),
                  pl.BlockSpec((1, D_MODEL), const)],
        out_specs=pl.BlockSpec((tm, D_MODEL), lambda i: (i, 0)),
        compiler_params=pltpu.CompilerParams(dimension_semantics=("arbitrary",),
                                             vmem_limit_bytes=VMEM_LIMIT),
        name="out",
    )(o_fox, proj, o_swa, proj, proj, proj, x2, ada, wbf, wbs, wo, ln_g, ln_b)


def kernel(x, c, w_ada, b_ada, w_in, b_f, attn_sinks, w_br_fox, w_br_swa, w_out, ln_g, ln_b):
    b, s, d = x.shape
    assert (b, s, d) == (1, SEQ, D_MODEL) and w_in.shape[0] == DEPTH
    x2 = x.reshape(s, d)

    wi = w_in[0]
    seg = lambda o, n: wi[:, o:o + n]
    w_main = jnp.concatenate([
        seg(_O_FQ, 1024), seg(_O_FK, 1024), seg(_O_FV, 1024),
        _swa_cols(seg(_O_SQ, 1024)),
        seg(_O_GF, 1024), _swa_cols(seg(_O_GS, 1024)),
        seg(_O_MF, 2048), seg(_O_MS, 2048),
        seg(_O_SK, 256), seg(_O_SV, 256)], axis=1).astype(BF16)
    w_flog = jnp.pad(seg(_O_FLOG, FOX_HEADS), ((0, 0), (0, LANES - FOX_HEADS))).astype(BF16)
    bf_pad = jnp.pad(b_f[0], (0, LANES - FOX_HEADS)).reshape(1, LANES)
    col_scale = jnp.asarray(_proj_col_scale())
    wbf = w_br_fox[0].astype(BF16)
    wbs = _swa_rows(w_br_swa[0]).astype(BF16)
    wo = w_out[0].astype(BF16)

    ada = _ada(c, w_ada[0], b_ada[0].reshape(1, -1))
    proj, flog = _proj(x2, ada, w_main, w_flog, col_scale)
    f_cum = _cum(flog, bf_pad)
    o_fox = _fox(proj, f_cum.reshape(FOX_HEADS, 1, s))
    o_swa = _swa(proj, attn_sinks[0])
    out = _out(proj, o_fox, o_swa, x2, ada, wbf, wbs, wo,
               ln_g[0].reshape(1, d), ln_b[0].reshape(1, d))
    return out.reshape(b, s, d)
```

```python
import functools
import math

import numpy as np
import jax
import jax.numpy as jnp
from jax import lax
from jax.experimental import pallas as pl
from jax.experimental.pallas import tpu as pltpu

F32 = jnp.float32
BF16 = jnp.bfloat16

D_MODEL = 2048
SEQ = 8192
FOX_HEADS = 8
FOX_HEAD_DIM = 128
FOX_WIDTH = FOX_HEADS * FOX_HEAD_DIM
SWA_Q_HEADS = 16
SWA_KV_HEADS = 4
SWA_HEAD_DIM = 64
SWA_GROUP = SWA_Q_HEADS // SWA_KV_HEADS
SWA_WIDTH = SWA_Q_HEADS * SWA_HEAD_DIM
SWA_KV_WIDTH = SWA_KV_HEADS * SWA_HEAD_DIM
WINDOW = 128
LN_EPS = 1e-5
NEG_INF = -1e30
DEPTH = 1
DEEPNORM_ALPHA = (2.0 * DEPTH) ** 0.25
LOG2E = math.log2(math.e)

LANES = 128
VMEM_LIMIT = 56 * 1024 * 1024

COL_FQ = 0
COL_FK = 1024
COL_FV = 2048
COL_SQ = 3072
COL_GF = 4096
COL_GS = 5120
COL_MF = 6144
COL_MS = 8192
COL_SK = 10240
COL_SV = 10496
PROJ_WIDTH = 10752

_O_FQ, _O_FK, _O_FV, _O_FLOG = 0, 1024, 2048, 3072
_O_SQ, _O_SK, _O_SV = 3080, 4104, 4360
_O_GF, _O_GS, _O_MF, _O_MS = 4616, 5640, 6664, 8712


def _swa_cols(w):
    r = w.shape[0]
    w4 = w.reshape(r, SWA_KV_HEADS, SWA_GROUP, SWA_HEAD_DIM)
    return w4.transpose(0, 2, 1, 3).reshape(r, SWA_WIDTH)


def _swa_rows(w):
    c = w.shape[1]
    w4 = w.reshape(SWA_KV_HEADS, SWA_GROUP, SWA_HEAD_DIM, c)
    return w4.transpose(1, 0, 2, 3).reshape(SWA_WIDTH, c)


def _proj_col_scale():
    s = np.ones((1, PROJ_WIDTH), np.float32)
    s[0, COL_FQ:COL_FQ + 1024] = FOX_HEAD_DIM ** -0.5 * LOG2E
    s[0, COL_SQ:COL_SQ + 1024] = SWA_HEAD_DIM ** -0.5
    return s


def _ada_kernel(c_ref, w_ref, b_ref, o_ref):
    c8 = jnp.broadcast_to(c_ref[...], (8, D_MODEL))
    r = jnp.dot(c8, w_ref[...], preferred_element_type=F32)
    o_ref[...] = r[0:1, :] + b_ref[...]


def _ada(c, w_ada, b_ada):
    tn = 512
    n = w_ada.shape[1]
    return pl.pallas_call(
        _ada_kernel,
        out_shape=jax.ShapeDtypeStruct((1, n), F32),
        grid=(n // tn,),
        in_specs=[pl.BlockSpec((1, D_MODEL), lambda j: (0, 0)),
                  pl.BlockSpec((D_MODEL, tn), lambda j: (0, j)),
                  pl.BlockSpec((1, tn), lambda j: (0, j))],
        out_specs=pl.BlockSpec((1, tn), lambda j: (0, j)),
        compiler_params=pltpu.CompilerParams(dimension_semantics=("arbitrary",),
                                             vmem_limit_bytes=VMEM_LIMIT),
        name="ada",
    )(c, w_ada, b_ada)


PROJ_TM = 1024
PROJ_TN = 768
PROJ_RC = 256


def _proj_kernel(x_ref, shift_ref, scale_ref, w_ref, wf_ref, cs_ref, o_ref, flog_ref, h_scr):
    j = pl.program_id(1)

    @pl.when(j == 0)
    def _():
        mod = 1.0 + scale_ref[...]
        shift = shift_ref[...]

        def body(r, carry):
            rows = pl.ds(pl.multiple_of(r * PROJ_RC, PROJ_RC), PROJ_RC)
            x = x_ref[rows, :]
            mu = jnp.mean(x, axis=-1, keepdims=True)
            xc = x - mu
            var = jnp.mean(xc * xc, axis=-1, keepdims=True)
            h = xc * lax.rsqrt(var + LN_EPS) * mod + shift
            h_scr[rows, :] = h.astype(BF16)
            return carry

        lax.fori_loop(0, PROJ_TM // PROJ_RC, body, 0)
        flog_ref[...] = jnp.dot(h_scr[...], wf_ref[...], preferred_element_type=F32)

    acc = jnp.dot(h_scr[...], w_ref[...], preferred_element_type=F32)
    o_ref[...] = (acc * cs_ref[...]).astype(BF16)


def _proj(x2, ada, w_main, w_flog, col_scale):
    s = x2.shape[0]
    return pl.pallas_call(
        _proj_kernel,
        out_shape=(jax.ShapeDtypeStruct((s, PROJ_WIDTH), BF16),
                   jax.ShapeDtypeStruct((s, LANES), F32)),
        grid=(s // PROJ_TM, PROJ_WIDTH // PROJ_TN),
        in_specs=[pl.BlockSpec((PROJ_TM, D_MODEL), lambda i, j: (i, 0)),
                  pl.BlockSpec((1, D_MODEL), lambda i, j: (0, 0)),
                  pl.BlockSpec((1, D_MODEL), lambda i, j: (0, 1)),
                  pl.BlockSpec((D_MODEL, PROJ_TN), lambda i, j: (0, j)),
                  pl.BlockSpec((D_MODEL, LANES), lambda i, j: (0, 0)),
                  pl.BlockSpec((1, PROJ_TN), lambda i, j: (0, j))],
        out_specs=(pl.BlockSpec((PROJ_TM, PROJ_TN), lambda i, j: (i, j)),
                   pl.BlockSpec((PROJ_TM, LANES), lambda i, j: (i, 0))),
        scratch_shapes=[pltpu.VMEM((PROJ_TM, D_MODEL), BF16)],
        compiler_params=pltpu.CompilerParams(dimension_semantics=("arbitrary", "arbitrary"),
                                             vmem_limit_bytes=VMEM_LIMIT),
        name="proj",
    )(x2, ada, ada, w_main, w_flog, col_scale)


def _cum_kernel(flog_ref, bf_ref, f_ref):
    s = flog_ref.shape[0]
    lf = jax.nn.log_sigmoid(flog_ref[...] + bf_ref[...])
    acc = lf.T[0:FOX_HEADS, :]
    lane = lax.broadcasted_iota(jnp.int32, acc.shape, 1)
    sh = 1
    while sh < s:
        rolled = pltpu.roll(acc, sh, axis=1)
        acc = acc + jnp.where(lane >= sh, rolled, 0.0)
        sh *= 2
    f_ref[...] = acc


def _cum(flog, bf_pad):
    s = flog.shape[0]
    return pl.pallas_call(
        _cum_kernel,
        out_shape=jax.ShapeDtypeStruct((FOX_HEADS, s), F32),
        in_specs=[pl.BlockSpec((s, LANES), lambda: (0, 0)),
                  pl.BlockSpec((1, LANES), lambda: (0, 0))],
        out_specs=pl.BlockSpec((FOX_HEADS, s), lambda: (0, 0)),
        compiler_params=pltpu.CompilerParams(vmem_limit_bytes=VMEM_LIMIT),
        name="cum",
    )(flog, bf_pad)


FOX_TK = 512
FOX_TQ = 2 * FOX_TK


def _fox_kernel(q_ref, k_ref, v_ref, f_ref, o_ref, sa_scr, sb_scr, m_scr, acc_scr):
    i = pl.program_id(1)
    tq, tk = FOX_TQ, FOX_TK
    q0 = pl.multiple_of(i * tq, tq)
    f_q = f_ref[0, :, pl.ds(q0, tq)]
    f_base = jnp.max(f_q, axis=1, keepdims=True)
    ones = jnp.ones((tk, LANES), BF16)

    m_scr[...] = jnp.full(m_scr.shape, NEG_INF, F32)
    acc_scr[...] = jnp.zeros(acc_scr.shape, F32)

    def scores(k0, dst):
        k_t = k_ref[pl.ds(k0, tk), :]
        s = lax.dot_general(q_ref[...], k_t, (((1,), (1,)), ((), ())), preferred_element_type=F32)
        bias = (f_base - f_ref[0, :, pl.ds(k0, tk)]) * LOG2E
        dst[...] = s + bias

    def softmax_pv(src, k0, masked):
        s = src[...]
        if masked:
            rel = (lax.broadcasted_iota(jnp.int32, (tq, tk), 1)
                   - lax.broadcasted_iota(jnp.int32, (tq, tk), 0))
            s = jnp.where(rel <= q0 - k0, s, NEG_INF)
        m_prev = m_scr[...]
        m_new = jnp.maximum(m_prev, jnp.max(s, axis=1, keepdims=True))
        alpha = jnp.exp2(m_prev - m_new)
        p = jnp.exp2(s - jnp.tile(m_new, (1, tk // LANES))).astype(BF16)
        v_aug = jnp.concatenate([v_ref[pl.ds(k0, tk), :], ones], axis=1)
        pv = jnp.dot(p, v_aug, preferred_element_type=F32)
        acc_scr[...] = acc_scr[...] * jnp.tile(alpha, (1, 2)) + pv
        m_scr[...] = m_new

    def pair(ka, masked, last):
        kb = pl.multiple_of(ka + tk, tk)
        scores(kb, sb_scr)
        softmax_pv(sa_scr, ka, masked)
        if not last:
            scores(pl.multiple_of(ka + 2 * tk, tk), sa_scr)
        softmax_pv(sb_scr, kb, masked)

    scores(0, sa_scr)

    def body(j, carry):
        pair(pl.multiple_of(j * tq, tq), False, False)
        return carry

    lax.fori_loop(0, i, body, 0)
    pair(q0, True, True)

    acc = acc_scr[...]
    o_ref[...] = (acc[:, :FOX_HEAD_DIM] / acc[:, FOX_HEAD_DIM:]).astype(BF16)


def _fox(proj, f3):
    s = proj.shape[0]
    tq, tk = FOX_TQ, FOX_TK
    return pl.pallas_call(
        _fox_kernel,
        out_shape=jax.ShapeDtypeStruct((s, FOX_WIDTH), BF16),
        grid=(FOX_HEADS, s // tq),
        in_specs=[pl.BlockSpec((tq, LANES), lambda h, i: (i, COL_FQ // LANES + h)),
                  pl.BlockSpec((s, LANES), lambda h, i: (0, COL_FK // LANES + h)),
                  pl.BlockSpec((s, LANES), lambda h, i: (0, COL_FV // LANES + h)),
                  pl.BlockSpec((1, 1, s), lambda h, i: (h, 0, 0))],
        out_specs=pl.BlockSpec((tq, LANES), lambda h, i: (i, h)),
        scratch_shapes=[pltpu.VMEM((tq, tk), F32), pltpu.VMEM((tq, tk), F32),
                        pltpu.VMEM((tq, LANES), F32), pltpu.VMEM((tq, 2 * LANES), F32)],
        compiler_params=pltpu.CompilerParams(dimension_semantics=("arbitrary", "arbitrary"),
                                             vmem_limit_bytes=VMEM_LIMIT),
        name="fox",
    )(proj, proj, proj, f3)


def _swa_kernel(sink_ref, q_ref, kp_ref, kc_ref, vp_ref, vc_ref, o_ref, bias_scr, sink_scr):
    n = pl.program_id(0)
    w = WINDOW

    @pl.when(n <= 1)
    def _():
        row = lax.broadcasted_iota(jnp.int32, (w, 2 * w), 0)
        col = lax.broadcasted_iota(jnp.int32, (w, 2 * w), 1)
        dist = row - col + w
        valid = (dist >= 0) & (dist < w) & ((col >= w) | (n > 0))
        distf = dist.astype(F32)
        for g in range(SWA_GROUP):
            for h in range(SWA_KV_HEADS):
                blk = g * SWA_KV_HEADS + h
                hq = h * SWA_GROUP + g
                slope = 2.0 ** (-8.0 * (hq + 1.0) / SWA_Q_HEADS)
                bias_scr[blk * w:(blk + 1) * w, :] = jnp.where(valid, -slope * distf, NEG_INF)
                sink_scr[blk * w:(blk + 1) * w, :] = jnp.full((w, LANES), sink_ref[hq], F32)

    kk = jnp.concatenate([kp_ref[...], kc_ref[...]], axis=0)
    vv = jnp.concatenate([vp_ref[...], vc_ref[...]], axis=0)
    lane_head = lax.broadcasted_iota(jnp.int32, (w, 2 * LANES), 1) // SWA_HEAD_DIM
    parts = []
    for g in range(SWA_GROUP):
        qg = q_ref[:, g * 256:(g + 1) * 256]
        for h in range(SWA_KV_HEADS):
            parts.append(jnp.where(lane_head == h, qg, jnp.zeros_like(qg)))
    qs = jnp.concatenate(parts, axis=0)
    s = lax.dot_general(qs, kk, (((1,), (1,)), ((), ())), preferred_element_type=F32)
    s = s + bias_scr[...]
    sink = sink_scr[...]
    m = jnp.maximum(jnp.max(s, axis=1, keepdims=True), sink)
    p = jnp.exp(s - jnp.tile(m, (1, 2)))
    denom = jnp.sum(p, axis=1, keepdims=True) + jnp.exp(sink - m)
    o = jnp.dot(p.astype(BF16), vv, preferred_element_type=F32)
    o = o * jnp.tile(1.0 / denom, (1, 2))
    for g in range(SWA_GROUP):
        out_g = jnp.zeros((w, 2 * LANES), F32)
        for h in range(SWA_KV_HEADS):
            blk = g * SWA_KV_HEADS + h
            out_g = jnp.where(lane_head == h, o[blk * w:(blk + 1) * w, :], out_g)
        o_ref[:, g * 256:(g + 1) * 256] = out_g.astype(BF16)


def _swa(proj, sinks):
    s = proj.shape[0]
    w = WINDOW
    nb = s // w
    kcol = COL_SK // SWA_KV_WIDTH
    vcol = COL_SV // SWA_KV_WIDTH
    grid_spec = pltpu.PrefetchScalarGridSpec(
        num_scalar_prefetch=1,
        grid=(nb,),
        in_specs=[pl.BlockSpec((w, SWA_WIDTH), lambda n, sk: (n, COL_SQ // SWA_WIDTH)),
                  pl.BlockSpec((w, SWA_KV_WIDTH), lambda n, sk: (jnp.maximum(n - 1, 0), kcol)),
                  pl.BlockSpec((w, SWA_KV_WIDTH), lambda n, sk: (n, kcol)),
                  pl.BlockSpec((w, SWA_KV_WIDTH), lambda n, sk: (jnp.maximum(n - 1, 0), vcol)),
                  pl.BlockSpec((w, SWA_KV_WIDTH), lambda n, sk: (n, vcol))],
        out_specs=pl.BlockSpec((w, SWA_WIDTH), lambda n, sk: (n, 0)),
        scratch_shapes=[pltpu.VMEM((SWA_Q_HEADS * w, 2 * w), F32),
                        pltpu.VMEM((SWA_Q_HEADS * w, LANES), F32)],
    )
    return pl.pallas_call(
        _swa_kernel,
        out_shape=jax.ShapeDtypeStruct((s, SWA_WIDTH), BF16),
        grid_spec=grid_spec,
        compiler_params=pltpu.CompilerParams(dimension_semantics=("arbitrary",),
                                             vmem_limit_bytes=VMEM_LIMIT),
        name="swa",
    )(sinks, proj, proj, proj, proj, proj)


OUT_TM = 256


def _out_kernel(of_ref, gf_ref, os_ref, gs_ref, mf_ref, ms_ref, x_ref, gate_ref,
                wbf_ref, wbs_ref, wo_ref, lng_ref, lnb_ref, o_ref):
    af = (of_ref[...].astype(F32) * jax.nn.silu(gf_ref[...].astype(F32))).astype(BF16)
    yf = jnp.dot(af, wbf_ref[...], preferred_element_type=F32)
    a_s = (os_ref[...].astype(F32) * jax.nn.silu(gs_ref[...].astype(F32))).astype(BF16)
    ys = jnp.dot(a_s, wbs_ref[...], preferred_element_type=F32)
    merged = (jax.nn.sigmoid(mf_ref[...].astype(F32)) * yf
              + jax.nn.sigmoid(ms_ref[...].astype(F32)) * ys)
    sub = jnp.dot(merged.astype(BF16), wo_ref[...], preferred_element_type=F32)
    z = DEEPNORM_ALPHA * x_ref[...] + gate_ref[...] * sub
    mu = jnp.mean(z, axis=-1, keepdims=True)
    zc = z - mu
    var = jnp.mean(zc * zc, axis=-1, keepdims=True)
    o_ref[...] = zc * lax.rsqrt(var + LN_EPS) * lng_ref[...] + lnb_ref[...]


def _out(proj, o_fox, o_swa, x2, ada, wbf, wbs, wo, ln_g, ln_b):
    s = x2.shape[0]
    tm = OUT_TM
    const = lambda i: (0, 0)
    return pl.pallas_call(
        _out_kernel,
        out_shape=jax.ShapeDtypeStruct((s, D_MODEL), F32),
        grid=(s // tm,),
        in_specs=[pl.BlockSpec((tm, FOX_WIDTH), lambda i: (i, 0)),
                  pl.BlockSpec((tm, FOX_WIDTH), lambda i: (i, COL_GF // FOX_WIDTH)),
                  pl.BlockSpec((tm, SWA_WIDTH), lambda i: (i, 0)),
                  pl.BlockSpec((tm, SWA_WIDTH), lambda i: (i, COL_GS // SWA_WIDTH)),
                  pl.BlockSpec((tm, D_MODEL), lambda i: (i, COL_MF // D_MODEL)),
                  pl.BlockSpec((tm, D_MODEL), lambda i: (i, COL_MS // D_MODEL)),
                  pl.BlockSpec((tm, D_MODEL), lambda i: (i, 0)),
                  pl.BlockSpec((1, D_MODEL), lambda i: (0, 2)),
                  pl.BlockSpec((FOX_WIDTH, D_MODEL), const),
                  pl.BlockSpec((SWA_WIDTH, D_MODEL), const),
                  pl.BlockSpec((D_MODEL, D_MODEL), const),
                  pl.BlockSpec((1, D_MODEL), const),
                  pl.BlockSpec((1, D_MODEL), const)],
        out_specs=pl.BlockSpec((tm, D_MODEL), lambda i: (i, 0)),
        compiler_params=pltpu.CompilerParams(dimension_semantics=("arbitrary",),
                                             vmem_limit_bytes=VMEM_LIMIT),
        name="out",
    )(o_fox, proj, o_swa, proj, proj, proj, x2, ada, wbf, wbs, wo, ln_g, ln_b)


def kernel(x, c, w_ada, b_ada, w_in, b_f, attn_sinks, w_br_fox, w_br_swa, w_out, ln_g, ln_b):
    b, s, d = x.shape
    assert (b, s, d) == (1, SEQ, D_MODEL) and w_in.shape[0] == DEPTH
    x2 = x.reshape(s, d)

    wi = w_in[0]
    seg = lambda o, n: wi[:, o:o + n]
    w_main = jnp.concatenate([
        seg(_O_FQ, 1024), seg(_O_FK, 1024), seg(_O_FV, 1024),
        _swa_cols(seg(_O_SQ, 1024)),
        seg(_O_GF, 1024), _swa_cols(seg(_O_GS, 1024)),
        seg(_O_MF, 2048), seg(_O_MS, 2048),
        seg(_O_SK, 256), seg(_O_SV, 256)], axis=1).astype(BF16)
    w_flog = jnp.pad(seg(_O_FLOG, FOX_HEADS), ((0, 0), (0, LANES - FOX_HEADS))).astype(BF16)
    bf_pad = jnp.pad(b_f[0], (0, LANES - FOX_HEADS)).reshape(1, LANES)
    col_scale = jnp.asarray(_proj_col_scale())
    wbf = w_br_fox[0].astype(BF16)
    wbs = _swa_rows(w_br_swa[0]).astype(BF16)
    wo = w_out[0].astype(BF16)

    ada = _ada(c, w_ada[0], b_ada[0].reshape(1, -1))
    proj, flog = _proj(x2, ada, w_main, w_flog, col_scale)
    f_cum = _cum(flog, bf_pad)
    o_fox = _fox(proj, f_cum.reshape(FOX_HEADS, 1, s))
    o_swa = _swa(proj, attn_sinks[0])
    out = _out(proj, o_fox, o_swa, x2, ada, wbf, wbs, wo,
               ln_g[0].reshape(1, d), ln_b[0].reshape(1, d))
    return out.reshape(b, s, d)
```

```python
import functools
import math

import numpy as np
import jax
import jax.numpy as jnp
from jax import lax
from jax.experimental import pallas as pl
from jax.experimental.pallas import tpu as pltpu

F32 = jnp.float32
BF16 = jnp.bfloat16

D_MODEL = 2048
SEQ = 8192
FOX_HEADS = 8
FOX_HEAD_DIM = 128
FOX_WIDTH = FOX_HEADS * FOX_HEAD_DIM
SWA_Q_HEADS = 16
SWA_KV_HEADS = 4
SWA_HEAD_DIM = 64
SWA_GROUP = SWA_Q_HEADS // SWA_KV_HEADS
SWA_WIDTH = SWA_Q_HEADS * SWA_HEAD_DIM
SWA_KV_WIDTH = SWA_KV_HEADS * SWA_HEAD_DIM
WINDOW = 128
LN_EPS = 1e-5
NEG_INF = -1e30
DEPTH = 1
DEEPNORM_ALPHA = (2.0 * DEPTH) ** 0.25
LOG2E = math.log2(math.e)

LANES = 128
VMEM_LIMIT = 56 * 1024 * 1024

COL_FQ = 0
COL_FK = 1024
COL_FV = 2048
COL_SQ = 3072
COL_GF = 4096
COL_GS = 5120
COL_MF = 6144
COL_MS = 8192
COL_SK = 10240
COL_SV = 10496
PROJ_WIDTH = 10752

_O_FQ, _O_FK, _O_FV, _O_FLOG = 0, 1024, 2048, 3072
_O_SQ, _O_SK, _O_SV = 3080, 4104, 4360
_O_GF, _O_GS, _O_MF, _O_MS = 4616, 5640, 6664, 8712


def _swa_cols(w):
    r = w.shape[0]
    w4 = w.reshape(r, SWA_KV_HEADS, SWA_GROUP, SWA_HEAD_DIM)
    return w4.transpose(0, 2, 1, 3).reshape(r, SWA_WIDTH)


def _swa_rows(w):
    c = w.shape[1]
    w4 = w.reshape(SWA_KV_HEADS, SWA_GROUP, SWA_HEAD_DIM, c)
    return w4.transpose(1, 0, 2, 3).reshape(SWA_WIDTH, c)


def _proj_col_scale():
    s = np.ones((1, PROJ_WIDTH), np.float32)
    s[0, COL_FQ:COL_FQ + 1024] = FOX_HEAD_DIM ** -0.5 * LOG2E
    s[0, COL_SQ:COL_SQ + 1024] = SWA_HEAD_DIM ** -0.5
    return s


def _ada_kernel(c_ref, w_ref, b_ref, o_ref):
    c8 = jnp.broadcast_to(c_ref[...], (8, D_MODEL))
    r = jnp.dot(c8, w_ref[...], preferred_element_type=F32)
    o_ref[...] = r[0:1, :] + b_ref[...]


def _ada(c, w_ada, b_ada):
    tn = 512
    n = w_ada.shape[1]
    return pl.pallas_call(
        _ada_kernel,
        out_shape=jax.ShapeDtypeStruct((1, n), F32),
        grid=(n // tn,),
        in_specs=[pl.BlockSpec((1, D_MODEL), lambda j: (0, 0)),
                  pl.BlockSpec((D_MODEL, tn), lambda j: (0, j)),
                  pl.BlockSpec((1, tn), lambda j: (0, j))],
        out_specs=pl.BlockSpec((1, tn), lambda j: (0, j)),
        compiler_params=pltpu.CompilerParams(dimension_semantics=("arbitrary",),
                                             vmem_limit_bytes=VMEM_LIMIT),
        name="ada",
    )(c, w_ada, b_ada)


PROJ_TM = 1024
PROJ_TN = 768
PROJ_RC = 256


def _proj_kernel(x_ref, shift_ref, scale_ref, w_ref, wf_ref, cs_ref, o_ref, flog_ref, h_scr):
    j = pl.program_id(1)

    @pl.when(j == 0)
    def _():
        mod = 1.0 + scale_ref[...]
        shift = shift_ref[...]

        def body(r, carry):
            rows = pl.ds(pl.multiple_of(r * PROJ_RC, PROJ_RC), PROJ_RC)
            x = x_ref[rows, :]
            mu = jnp.mean(x, axis=-1, keepdims=True)
            xc = x - mu
            var = jnp.mean(xc * xc, axis=-1, keepdims=True)
            h = xc * lax.rsqrt(var + LN_EPS) * mod + shift
            h_scr[rows, :] = h.astype(BF16)
            return carry

        lax.fori_loop(0, PROJ_TM // PROJ_RC, body, 0)
        flog_ref[...] = jnp.dot(h_scr[...], wf_ref[...], preferred_element_type=F32)

    acc = jnp.dot(h_scr[...], w_ref[...], preferred_element_type=F32)
    o_ref[...] = (acc * cs_ref[...]).astype(BF16)


def _proj(x2, ada, w_main, w_flog, col_scale):
    s = x2.shape[0]
    return pl.pallas_call(
        _proj_kernel,
        out_shape=(jax.ShapeDtypeStruct((s, PROJ_WIDTH), BF16),
                   jax.ShapeDtypeStruct((s, LANES), F32)),
        grid=(s // PROJ_TM, PROJ_WIDTH // PROJ_TN),
        in_specs=[pl.BlockSpec((PROJ_TM, D_MODEL), lambda i, j: (i, 0)),
                  pl.BlockSpec((1, D_MODEL), lambda i, j: (0, 0)),
                  pl.BlockSpec((1, D_MODEL), lambda i, j: (0, 1)),
                  pl.BlockSpec((D_MODEL, PROJ_TN), lambda i, j: (0, j)),
                  pl.BlockSpec((D_MODEL, LANES), lambda i, j: (0, 0)),
                  pl.BlockSpec((1, PROJ_TN), lambda i, j: (0, j))],
        out_specs=(pl.BlockSpec((PROJ_TM, PROJ_TN), lambda i, j: (i, j)),
                   pl.BlockSpec((PROJ_TM, LANES), lambda i, j: (i, 0))),
        scratch_shapes=[pltpu.VMEM((PROJ_TM, D_MODEL), BF16)],
        compiler_params=pltpu.CompilerParams(dimension_semantics=("arbitrary", "arbitrary"),
                                             vmem_limit_bytes=VMEM_LIMIT),
        name="proj",
    )(x2, ada, ada, w_main, w_flog, col_scale)


CUM_RC = 1024


def _split3(x):
    hi = x.astype(BF16)
    r1 = x - hi.astype(F32)
    mid = r1.astype(BF16)
    lo = (r1 - mid.astype(F32)).astype(BF16)
    return hi, mid, lo


def _cum_kernel(flog_ref, bf_ref, f_ref, kaug_ref, c_scr):
    s = flog_ref.shape[0]
    lf = jax.nn.log_sigmoid(flog_ref[...] + bf_ref[...])
    acc = lf.T[0:FOX_HEADS, :]
    lane = lax.broadcasted_iota(jnp.int32, acc.shape, 1)
    sh = 1
    while sh < s:
        rolled = pltpu.roll(acc, sh, axis=1)
        acc = acc + jnp.where(lane >= sh, rolled, 0.0)
        sh *= 2
    f_ref[...] = acc
    pad = jnp.zeros((LANES - FOX_HEADS, s), F32)
    c_scr[...] = jnp.concatenate([acc, pad], axis=0).T * LOG2E

    rowi = lax.broadcasted_iota(jnp.int32, (3 * LANES, LANES), 0)
    coli = lax.broadcasted_iota(jnp.int32, (3 * LANES, LANES), 1)
    ones_row = jnp.where((lax.broadcasted_iota(jnp.int32, (1, LANES), 1) >= 3)
                         & (lax.broadcasted_iota(jnp.int32, (1, LANES), 1) < 6), 1.0, 0.0)

    def chunk(r, carry):
        rows = pl.ds(pl.multiple_of(r * CUM_RC, CUM_RC), CUM_RC)
        hi, mid, lo = _split3(c_scr[rows, :])
        pieces = jnp.concatenate([hi, mid, lo], axis=1)
        for h in range(FOX_HEADS):
            sel = jnp.where(rowi == coli * LANES + h, 1.0, 0.0).astype(BF16)
            picked = jnp.dot(pieces, sel, preferred_element_type=F32)
            kaug_ref[h, rows, :] = (picked + ones_row).astype(BF16)
        return carry

    lax.fori_loop(0, s // CUM_RC, chunk, 0)


def _cum(flog, bf_pad):
    s = flog.shape[0]
    return pl.pallas_call(
        _cum_kernel,
        out_shape=(jax.ShapeDtypeStruct((FOX_HEADS, s), F32),
                   jax.ShapeDtypeStruct((FOX_HEADS, s, LANES), BF16)),
        in_specs=[pl.BlockSpec((s, LANES), lambda: (0, 0)),
                  pl.BlockSpec((1, LANES), lambda: (0, 0))],
        out_specs=(pl.BlockSpec((FOX_HEADS, s), lambda: (0, 0)),
                   pl.BlockSpec((FOX_HEADS, s, LANES), lambda: (0, 0, 0))),
        scratch_shapes=[pltpu.VMEM((s, LANES), F32)],
        compiler_params=pltpu.CompilerParams(vmem_limit_bytes=VMEM_LIMIT),
        name="cum",
    )(flog, bf_pad)


FOX_TK = 512
FOX_TQ = 2 * FOX_TK


FOX_VT_ROWS = FOX_HEAD_DIM + 16
FOX_VT_CHUNK = 512


def _fox_kernel(q_ref, k_ref, ka_ref, v_ref, f_ref, o_ref,
                vt_scr, qa_scr, sa_scr, sb_scr, m_scr, acc_scr):
    i = pl.program_id(1)
    tq, tk = FOX_TQ, FOX_TK
    s_len = v_ref.shape[0]

    @pl.when(i == 0)
    def _():
        for c in range(s_len // FOX_VT_CHUNK):
            cols = slice(c * FOX_VT_CHUNK, (c + 1) * FOX_VT_CHUNK)
            vt_scr[0:FOX_HEAD_DIM, cols] = v_ref[cols, :].astype(F32).T.astype(BF16)
        vt_scr[FOX_HEAD_DIM:FOX_VT_ROWS, :] = jnp.ones((FOX_VT_ROWS - FOX_HEAD_DIM, s_len), BF16)

    q0 = pl.multiple_of(i * tq, tq)
    f_q = f_ref[0, :, pl.ds(q0, tq)]
    f_base = jnp.max(f_q, axis=1, keepdims=True)
    b_hi, b_mid, b_lo = _split3(jnp.broadcast_to(f_base * LOG2E, (1, LANES)))
    lane = lax.broadcasted_iota(jnp.int32, (1, LANES), 1)
    qa_row = jnp.where(lane < 3, -1.0,
                       jnp.where(lane == 3, b_hi.astype(F32),
                                 jnp.where(lane == 4, b_mid.astype(F32),
                                           jnp.where(lane == 5, b_lo.astype(F32), 0.0))))
    qa_scr[:, 0:LANES] = q_ref[...]
    qa_scr[:, LANES:2 * LANES] = jnp.broadcast_to(qa_row, (tq, LANES)).astype(BF16)

    m_scr[...] = jnp.full(m_scr.shape, NEG_INF, F32)
    acc_scr[...] = jnp.zeros(acc_scr.shape, F32)

    def scores(k0, dst):
        k_aug = jnp.concatenate([k_ref[pl.ds(k0, tk), :], ka_ref[0, pl.ds(k0, tk), :]], axis=1)
        dst[...] = lax.dot_general(k_aug, qa_scr[...], (((1,), (1,)), ((), ())),
                                   preferred_element_type=F32)

    def softmax_pv(src, k0, masked):
        s = src[...]
        if masked:
            rel = (lax.broadcasted_iota(jnp.int32, (tk, tq), 0)
                   - lax.broadcasted_iota(jnp.int32, (tk, tq), 1))
            s = jnp.where(rel <= q0 - k0, s, NEG_INF)
        m_prev = m_scr[...]
        m_new = jnp.maximum(m_prev, jnp.max(s, axis=0, keepdims=True))
        alpha = jnp.exp2(m_prev - m_new)
        p = jnp.exp2(s - m_new[0:1, :]).astype(BF16)
        pv = jnp.dot(vt_scr[:, pl.ds(k0, tk)], p, preferred_element_type=F32)
        acc_scr[...] = acc_scr[...] * alpha[0:1, :] + pv
        m_scr[...] = m_new

    def pair(ka, masked, last):
        kb = pl.multiple_of(ka + tk, tk)
        scores(kb, sb_scr)
        softmax_pv(sa_scr, ka, masked)
        if not last:
            scores(pl.multiple_of(ka + 2 * tk, tk), sa_scr)
        softmax_pv(sb_scr, kb, masked)

    scores(0, sa_scr)

    def body(j, carry):
        pair(pl.multiple_of(j * tq, tq), False, False)
        return carry

    lax.fori_loop(0, i, body, 0)
    pair(q0, True, True)

    o_t = acc_scr[0:FOX_HEAD_DIM, :] / acc_scr[FOX_HEAD_DIM:FOX_HEAD_DIM + 1, :]
    o_ref[...] = o_t.T.astype(BF16)


def _fox(proj, kaug, f3):
    s = proj.shape[0]
    tq, tk = FOX_TQ, FOX_TK
    return pl.pallas_call(
        _fox_kernel,
        out_shape=jax.ShapeDtypeStruct((s, FOX_WIDTH), BF16),
        grid=(FOX_HEADS, s // tq),
        in_specs=[pl.BlockSpec((tq, LANES), lambda h, i: (i, COL_FQ // LANES + h)),
                  pl.BlockSpec((s, LANES), lambda h, i: (0, COL_FK // LANES + h)),
                  pl.BlockSpec((1, s, LANES), lambda h, i: (h, 0, 0)),
                  pl.BlockSpec((s, LANES), lambda h, i: (0, COL_FV // LANES + h)),
                  pl.BlockSpec((1, 1, s), lambda h, i: (h, 0, 0))],
        out_specs=pl.BlockSpec((tq, LANES), lambda h, i: (i, h)),
        scratch_shapes=[pltpu.VMEM((FOX_VT_ROWS, s), BF16), pltpu.VMEM((tq, 2 * LANES), BF16),
                        pltpu.VMEM((tk, tq), F32), pltpu.VMEM((tk, tq), F32),
                        pltpu.VMEM((8, tq), F32), pltpu.VMEM((FOX_VT_ROWS, tq), F32)],
        compiler_params=pltpu.CompilerParams(dimension_semantics=("arbitrary", "arbitrary"),
                                             vmem_limit_bytes=VMEM_LIMIT),
        name="fox",
    )(proj, proj, kaug, proj, f3)


def _swa_kernel(sink_ref, q_ref, kp_ref, kc_ref, vp_ref, vc_ref, o_ref, bias_scr, sink_scr):
    n = pl.program_id(0)
    w = WINDOW

    @pl.when(n <= 1)
    def _():
        row = lax.broadcasted_iota(jnp.int32, (w, 2 * w), 0)
        col = lax.broadcasted_iota(jnp.int32, (w, 2 * w), 1)
        dist = row - col + w
        valid = (dist >= 0) & (dist < w) & ((col >= w) | (n > 0))
        distf = dist.astype(F32)
        for g in range(SWA_GROUP):
            for h in range(SWA_KV_HEADS):
                blk = g * SWA_KV_HEADS + h
                hq = h * SWA_GROUP + g
                slope = 2.0 ** (-8.0 * (hq + 1.0) / SWA_Q_HEADS)
                bias_scr[blk * w:(blk + 1) * w, :] = jnp.where(valid, -slope * distf, NEG_INF)
                sink_scr[blk * w:(blk + 1) * w, :] = jnp.full((w, LANES), sink_ref[hq], F32)

    kk = jnp.concatenate([kp_ref[...], kc_ref[...]], axis=0)
    vv = jnp.concatenate([vp_ref[...], vc_ref[...]], axis=0)
    lane_head = lax.broadcasted_iota(jnp.int32, (w, 2 * LANES), 1) // SWA_HEAD_DIM
    parts = []
    for g in range(SWA_GROUP):
        qg = q_ref[:, g * 256:(g + 1) * 256]
        for h in range(SWA_KV_HEADS):
            parts.append(jnp.where(lane_head == h, qg, jnp.zeros_like(qg)))
    qs = jnp.concatenate(parts, axis=0)
    s = lax.dot_general(qs, kk, (((1,), (1,)), ((), ())), preferred_element_type=F32)
    s = s + bias_scr[...]
    sink = sink_scr[...]
    m = jnp.maximum(jnp.max(s, axis=1, keepdims=True), sink)
    p = jnp.exp(s - jnp.tile(m, (1, 2)))
    denom = jnp.sum(p, axis=1, keepdims=True) + jnp.exp(sink - m)
    o = jnp.dot(p.astype(BF16), vv, preferred_element_type=F32)
    o = o * jnp.tile(1.0 / denom, (1, 2))
    for g in range(SWA_GROUP):
        out_g = jnp.zeros((w, 2 * LANES), F32)
        for h in range(SWA_KV_HEADS):
            blk = g * SWA_KV_HEADS + h
            out_g = jnp.where(lane_head == h, o[blk * w:(blk + 1) * w, :], out_g)
        o_ref[:, g * 256:(g + 1) * 256] = out_g.astype(BF16)


def _swa(proj, sinks):
    s = proj.shape[0]
    w = WINDOW
    nb = s // w
    kcol = COL_SK // SWA_KV_WIDTH
    vcol = COL_SV // SWA_KV_WIDTH
    grid_spec = pltpu.PrefetchScalarGridSpec(
        num_scalar_prefetch=1,
        grid=(nb,),
        in_specs=[pl.BlockSpec((w, SWA_WIDTH), lambda n, sk: (n, COL_SQ // SWA_WIDTH)),
                  pl.BlockSpec((w, SWA_KV_WIDTH), lambda n, sk: (jnp.maximum(n - 1, 0), kcol)),
                  pl.BlockSpec((w, SWA_KV_WIDTH), lambda n, sk: (n, kcol)),
                  pl.BlockSpec((w, SWA_KV_WIDTH), lambda n, sk: (jnp.maximum(n - 1, 0), vcol)),
                  pl.BlockSpec((w, SWA_KV_WIDTH), lambda n, sk: (n, vcol))],
        out_specs=pl.BlockSpec((w, SWA_WIDTH), lambda n, sk: (n, 0)),
        scratch_shapes=[pltpu.VMEM((SWA_Q_HEADS * w, 2 * w), F32),
                        pltpu.VMEM((SWA_Q_HEADS * w, LANES), F32)],
    )
    return pl.pallas_call(
        _swa_kernel,
        out_shape=jax.ShapeDtypeStruct((s, SWA_WIDTH), BF16),
        grid_spec=grid_spec,
        compiler_params=pltpu.CompilerParams(dimension_semantics=("arbitrary",),
                                             vmem_limit_bytes=VMEM_LIMIT),
        name="swa",
    )(sinks, proj, proj, proj, proj, proj)


OUT_TM = 256


def _out_kernel(of_ref, gf_ref, os_ref, gs_ref, mf_ref, ms_ref, x_ref, gate_ref,
                wbf_ref, wbs_ref, wo_ref, lng_ref, lnb_ref, o_ref):
    af = (of_ref[...].astype(F32) * jax.nn.silu(gf_ref[...].astype(F32))).astype(BF16)
    yf = jnp.dot(af, wbf_ref[...], preferred_element_type=F32)
    a_s = (os_ref[...].astype(F32) * jax.nn.silu(gs_ref[...].astype(F32))).astype(BF16)
    ys = jnp.dot(a_s, wbs_ref[...], preferred_element_type=F32)
    merged = (jax.nn.sigmoid(mf_ref[...].astype(F32)) * yf
              + jax.nn.sigmoid(ms_ref[...].astype(F32)) * ys)
    sub = jnp.dot(merged.astype(BF16), wo_ref[...], preferred_element_type=F32)
    z = DEEPNORM_ALPHA * x_ref[...] + gate_ref[...] * sub
    mu = jnp.mean(z, axis=-1, keepdims=True)
    zc = z - mu
    var = jnp.mean(zc * zc, axis=-1, keepdims=True)
    o_ref[...] = zc * lax.rsqrt(var + LN_EPS) * lng_ref[...] + lnb_ref[...]


def _out(proj, o_fox, o_swa, x2, ada, wbf, wbs, wo, ln_g, ln_b):
    s = x2.shape[0]
    tm = OUT_TM
    const = lambda i: (0, 0)
    return pl.pallas_call(
        _out_kernel,
        out_shape=jax.ShapeDtypeStruct((s, D_MODEL), F32),
        grid=(s // tm,),
        in_specs=[pl.BlockSpec((tm, FOX_WIDTH), lambda i: (i, 0)),
                  pl.BlockSpec((tm, FOX_WIDTH), lambda i: (i, COL_GF // FOX_WIDTH)),
                  pl.BlockSpec((tm, SWA_WIDTH), lambda i: (i, 0)),
                  pl.BlockSpec((tm, SWA_WIDTH), lambda i: (i, COL_GS // SWA_WIDTH)),
                  pl.BlockSpec((tm, D_MODEL), lambda i: (i, COL_MF // D_MODEL)),
                  pl.BlockSpec((tm, D_MODEL), lambda i: (i, COL_MS // D_MODEL)),
                  pl.BlockSpec((tm, D_MODEL), lambda i: (i, 0)),
                  pl.BlockSpec((1, D_MODEL), lambda i: (0, 2)),
                  pl.BlockSpec((FOX_WIDTH, D_MODEL), const),
                  pl.BlockSpec((SWA_WIDTH, D_MODEL), const),
                  pl.BlockSpec((D_MODEL, D_MODEL), const),
                  pl.BlockSpec((1, D_MODEL), const),
                  pl.BlockSpec((1, D_MODEL), const)],
        out_specs=pl.BlockSpec((tm, D_MODEL), lambda i: (i, 0)),
        compiler_params=pltpu.CompilerParams(dimension_semantics=("arbitrary",),
                                             vmem_limit_bytes=VMEM_LIMIT),
        name="out",
    )(o_fox, proj, o_swa, proj, proj, proj, x2, ada, wbf, wbs, wo, ln_g, ln_b)


def kernel(x, c, w_ada, b_ada, w_in, b_f, attn_sinks, w_br_fox, w_br_swa, w_out, ln_g, ln_b):
    b, s, d = x.shape
    assert (b, s, d) == (1, SEQ, D_MODEL) and w_in.shape[0] == DEPTH
    x2 = x.reshape(s, d)

    wi = w_in[0]
    seg = lambda o, n: wi[:, o:o + n]
    w_main = jnp.concatenate([
        seg(_O_FQ, 1024), seg(_O_FK, 1024), seg(_O_FV, 1024),
        _swa_cols(seg(_O_SQ, 1024)),
        seg(_O_GF, 1024), _swa_cols(seg(_O_GS, 1024)),
        seg(_O_MF, 2048), seg(_O_MS, 2048),
        seg(_O_SK, 256), seg(_O_SV, 256)], axis=1).astype(BF16)
    w_flog = jnp.pad(seg(_O_FLOG, FOX_HEADS), ((0, 0), (0, LANES - FOX_HEADS))).astype(BF16)
    bf_pad = jnp.pad(b_f[0], (0, LANES - FOX_HEADS)).reshape(1, LANES)
    col_scale = jnp.asarray(_proj_col_scale())
    wbf = w_br_fox[0].astype(BF16)
    wbs = _swa_rows(w_br_swa[0]).astype(BF16)
    wo = w_out[0].astype(BF16)

    ada = _ada(c, w_ada[0], b_ada[0].reshape(1, -1))
    proj, flog = _proj(x2, ada, w_main, w_flog, col_scale)
    f_cum, kaug = _cum(flog, bf_pad)
    o_fox = _fox(proj, kaug, f_cum.reshape(FOX_HEADS, 1, s))
    o_swa = _swa(proj, attn_sinks[0])
    out = _out(proj, o_fox, o_swa, x2, ada, wbf, wbs, wo,
               ln_g[0].reshape(1, d), ln_b[0].reshape(1, d))
    return out.reshape(b, s, d)
```

```python
import math

import numpy as np
import jax
import jax.numpy as jnp
from jax import lax
from jax.experimental import pallas as pl
from jax.experimental.pallas import tpu as pltpu

F32 = jnp.float32
BF16 = jnp.bfloat16

D_MODEL = 2048
SEQ = 8192
FOX_HEADS = 8
FOX_HEAD_DIM = 128
FOX_WIDTH = FOX_HEADS * FOX_HEAD_DIM
SWA_Q_HEADS = 16
SWA_KV_HEADS = 4
SWA_HEAD_DIM = 64
SWA_GROUP = SWA_Q_HEADS // SWA_KV_HEADS
SWA_WIDTH = SWA_Q_HEADS * SWA_HEAD_DIM
SWA_KV_WIDTH = SWA_KV_HEADS * SWA_HEAD_DIM
WINDOW = 128
LN_EPS = 1e-5
NEG_INF = -1e30
DEPTH = 1
DEEPNORM_ALPHA = (2.0 * DEPTH) ** 0.25
LOG2E = math.log2(math.e)

LANES = 128
VMEM_LIMIT = 56 * 1024 * 1024

COL_FQ = 0
COL_FK = 1024
COL_FV = 2048
COL_SQ = 3072
COL_GF = 4096
COL_GS = 5120
COL_MF = 6144
COL_MS = 8192
COL_SK = 10240
COL_SV = 10496
PROJ_WIDTH = 10752

_O_FQ, _O_FK, _O_FV, _O_FLOG = 0, 1024, 2048, 3072
_O_SQ, _O_SK, _O_SV = 3080, 4104, 4360
_O_GF, _O_GS, _O_MF, _O_MS = 4616, 5640, 6664, 8712


def _swa_cols(w):
    r = w.shape[0]
    w4 = w.reshape(r, SWA_KV_HEADS, SWA_GROUP, SWA_HEAD_DIM)
    return w4.transpose(0, 2, 1, 3).reshape(r, SWA_WIDTH)


def _swa_rows(w):
    c = w.shape[1]
    w4 = w.reshape(SWA_KV_HEADS, SWA_GROUP, SWA_HEAD_DIM, c)
    return w4.transpose(1, 0, 2, 3).reshape(SWA_WIDTH, c)


def _proj_col_scale():
    s = np.ones((1, PROJ_WIDTH), np.float32)
    s[0, COL_FQ:COL_FQ + 1024] = FOX_HEAD_DIM ** -0.5 * LOG2E
    s[0, COL_SQ:COL_SQ + 1024] = SWA_HEAD_DIM ** -0.5
    return s


def _ada_kernel(c_ref, w_ref, b_ref, o_ref):
    c8 = jnp.broadcast_to(c_ref[...], (8, D_MODEL))
    r = jnp.dot(c8, w_ref[...], preferred_element_type=F32)
    o_ref[...] = r[0:1, :] + b_ref[...]


def _ada(c, w_ada, b_ada):
    tn = 512
    n = w_ada.shape[1]
    return pl.pallas_call(
        _ada_kernel,
        out_shape=jax.ShapeDtypeStruct((1, n), F32),
        grid=(n // tn,),
        in_specs=[pl.BlockSpec((1, D_MODEL), lambda j: (0, 0)),
                  pl.BlockSpec((D_MODEL, tn), lambda j: (0, j)),
                  pl.BlockSpec((1, tn), lambda j: (0, j))],
        out_specs=pl.BlockSpec((1, tn), lambda j: (0, j)),
        compiler_params=pltpu.CompilerParams(dimension_semantics=("arbitrary",),
                                             vmem_limit_bytes=VMEM_LIMIT),
        name="ada",
    )(c, w_ada, b_ada)


PROJ_TM = 1024
PROJ_TN = 1536
PROJ_RC = 256


def _proj_kernel(x_ref, shift_ref, scale_ref, w_ref, wf_ref, cs_ref, o_ref, flog_ref, h_scr):
    j = pl.program_id(1)

    @pl.when(j == 0)
    def _():
        mod = 1.0 + scale_ref[...]
        shift = shift_ref[...]

        def body(r, carry):
            rows = pl.ds(pl.multiple_of(r * PROJ_RC, PROJ_RC), PROJ_RC)
            x = x_ref[rows, :]
            mu = jnp.mean(x, axis=-1, keepdims=True)
            xc = x - mu
            var = jnp.mean(xc * xc, axis=-1, keepdims=True)
            h = xc * lax.rsqrt(var + LN_EPS) * mod + shift
            h_scr[rows, :] = h.astype(BF16)
            return carry

        lax.fori_loop(0, PROJ_TM // PROJ_RC, body, 0)
        flog_ref[...] = jnp.dot(h_scr[...], wf_ref[...], preferred_element_type=F32)

    acc = jnp.dot(h_scr[...], w_ref[...], preferred_element_type=F32)
    o_ref[...] = (acc * cs_ref[...]).astype(BF16)


def _proj(x2, ada, w_main, w_flog, col_scale):
    s = x2.shape[0]
    return pl.pallas_call(
        _proj_kernel,
        out_shape=(jax.ShapeDtypeStruct((s, PROJ_WIDTH), BF16),
                   jax.ShapeDtypeStruct((s, LANES), F32)),
        grid=(s // PROJ_TM, PROJ_WIDTH // PROJ_TN),
        in_specs=[pl.BlockSpec((PROJ_TM, D_MODEL), lambda i, j: (i, 0)),
                  pl.BlockSpec((1, D_MODEL), lambda i, j: (0, 0)),
                  pl.BlockSpec((1, D_MODEL), lambda i, j: (0, 1)),
                  pl.BlockSpec((D_MODEL, PROJ_TN), lambda i, j: (0, j)),
                  pl.BlockSpec((D_MODEL, LANES), lambda i, j: (0, 0)),
                  pl.BlockSpec((1, PROJ_TN), lambda i, j: (0, j))],
        out_specs=(pl.BlockSpec((PROJ_TM, PROJ_TN), lambda i, j: (i, j)),
                   pl.BlockSpec((PROJ_TM, LANES), lambda i, j: (i, 0))),
        scratch_shapes=[pltpu.VMEM((PROJ_TM, D_MODEL), BF16)],
        compiler_params=pltpu.CompilerParams(dimension_semantics=("arbitrary", "arbitrary"),
                                             vmem_limit_bytes=VMEM_LIMIT),
        name="proj",
    )(x2, ada, ada, w_main, w_flog, col_scale)


def _cum_kernel(flog_ref, bf_ref, f_ref):
    s = flog_ref.shape[0]
    lf = jax.nn.log_sigmoid(flog_ref[...] + bf_ref[...])
    acc = lf.T[0:FOX_HEADS, :]
    lane = lax.broadcasted_iota(jnp.int32, acc.shape, 1)
    sh = 1
    while sh < s:
        rolled = pltpu.roll(acc, sh, axis=1)
        acc = acc + jnp.where(lane >= sh, rolled, 0.0)
        sh *= 2
    f_ref[...] = acc


def _cum(flog, bf_pad):
    s = flog.shape[0]
    return pl.pallas_call(
        _cum_kernel,
        out_shape=jax.ShapeDtypeStruct((FOX_HEADS, s), F32),
        in_specs=[pl.BlockSpec((s, LANES), lambda: (0, 0)),
                  pl.BlockSpec((1, LANES), lambda: (0, 0))],
        out_specs=pl.BlockSpec((FOX_HEADS, s), lambda: (0, 0)),
        compiler_params=pltpu.CompilerParams(vmem_limit_bytes=VMEM_LIMIT),
        name="cum",
    )(flog, bf_pad)


FOX_TK = 512
FOX_TQ = 2 * FOX_TK


def _fox_kernel(q_ref, k_ref, v_ref, f_ref, o_ref, sa_scr, sb_scr, m_scr, acc_scr):
    i = pl.program_id(1)
    tq, tk = FOX_TQ, FOX_TK
    q0 = pl.multiple_of(i * tq, tq)
    f_q = f_ref[0, :, pl.ds(q0, tq)]
    f_base = jnp.max(f_q, axis=1, keepdims=True)
    ones = jnp.ones((tk, LANES), BF16)
    top, bot = slice(0, tk), slice(tk, tq)

    m_scr[...] = jnp.full(m_scr.shape, NEG_INF, F32)
    acc_scr[...] = jnp.zeros(acc_scr.shape, F32)

    def scores(k0, dst, rows=slice(0, FOX_TQ)):
        k_t = k_ref[pl.ds(k0, tk), :]
        s = lax.dot_general(q_ref[rows, :], k_t, (((1,), (1,)), ((), ())), preferred_element_type=F32)
        bias = (f_base - f_ref[0, :, pl.ds(k0, tk)]) * LOG2E
        dst[rows, :] = s + bias

    def softmax_pv(src, k0, rows=slice(0, FOX_TQ), causal=False):
        s = src[rows, :]
        if causal:
            n = rows.stop - rows.start
            keep = (lax.broadcasted_iota(jnp.int32, (n, tk), 1)
                    <= lax.broadcasted_iota(jnp.int32, (n, tk), 0))
            s = jnp.where(keep, s, NEG_INF)
        m_prev = m_scr[rows, :]
        m_new = jnp.maximum(m_prev, jnp.max(s, axis=1, keepdims=True))
        alpha = jnp.exp2(m_prev - m_new)
        p = jnp.exp2(s - jnp.tile(m_new, (1, tk // LANES))).astype(BF16)
        v_aug = jnp.concatenate([v_ref[pl.ds(k0, tk), :], ones], axis=1)
        pv = jnp.dot(p, v_aug, preferred_element_type=F32)
        acc_scr[rows, :] = acc_scr[rows, :] * jnp.tile(alpha, (1, 2)) + pv
        m_scr[rows, :] = m_new

    scores(0, sa_scr)

    def body(j, carry):
        ka = pl.multiple_of(j * tq, tq)
        kb = pl.multiple_of(ka + tk, tk)
        scores(kb, sb_scr)
        softmax_pv(sa_scr, ka)
        scores(pl.multiple_of(ka + tq, tk), sa_scr)
        softmax_pv(sb_scr, kb)
        return carry

    lax.fori_loop(0, i, body, 0)

    kd = pl.multiple_of(q0 + tk, tk)
    scores(kd, sb_scr, bot)
    softmax_pv(sa_scr, q0, top, causal=True)
    softmax_pv(sa_scr, q0, bot)
    softmax_pv(sb_scr, kd, bot, causal=True)

    acc = acc_scr[...]
    o_ref[...] = (acc[:, :FOX_HEAD_DIM] / acc[:, FOX_HEAD_DIM:]).astype(BF16)


def _fox(proj, f3):
    s = proj.shape[0]
    tq, tk = FOX_TQ, FOX_TK
    return pl.pallas_call(
        _fox_kernel,
        out_shape=jax.ShapeDtypeStruct((s, FOX_WIDTH), BF16),
        grid=(FOX_HEADS, s // tq),
        in_specs=[pl.BlockSpec((tq, LANES), lambda h, i: (i, COL_FQ // LANES + h)),
                  pl.BlockSpec((s, LANES), lambda h, i: (0, COL_FK // LANES + h)),
                  pl.BlockSpec((s, LANES), lambda h, i: (0, COL_FV // LANES + h)),
                  pl.BlockSpec((1, 1, s), lambda h, i: (h, 0, 0))],
        out_specs=pl.BlockSpec((tq, LANES), lambda h, i: (i, h)),
        scratch_shapes=[pltpu.VMEM((tq, tk), F32), pltpu.VMEM((tq, tk), F32),
                        pltpu.VMEM((tq, LANES), F32), pltpu.VMEM((tq, 2 * LANES), F32)],
        compiler_params=pltpu.CompilerParams(dimension_semantics=("arbitrary", "arbitrary"),
                                             vmem_limit_bytes=VMEM_LIMIT),
        name="fox",
    )(proj, proj, proj, f3)


def _swa_kernel(sink_ref, q_ref, kp_ref, kc_ref, vp_ref, vc_ref, o_ref, bias_scr, sink_scr):
    n = pl.program_id(0)
    w = WINDOW

    @pl.when(n <= 1)
    def _():
        row = lax.broadcasted_iota(jnp.int32, (w, 2 * w), 0)
        col = lax.broadcasted_iota(jnp.int32, (w, 2 * w), 1)
        dist = row - col + w
        valid = (dist >= 0) & (dist < w) & ((col >= w) | (n > 0))
        distf = dist.astype(F32)
        for g in range(SWA_GROUP):
            for h in range(SWA_KV_HEADS):
                blk = g * SWA_KV_HEADS + h
                hq = h * SWA_GROUP + g
                slope = 2.0 ** (-8.0 * (hq + 1.0) / SWA_Q_HEADS)
                bias_scr[blk * w:(blk + 1) * w, :] = jnp.where(valid, -slope * distf, NEG_INF)
                sink_scr[blk * w:(blk + 1) * w, :] = jnp.full((w, LANES), sink_ref[hq], F32)

    kk = jnp.concatenate([kp_ref[...], kc_ref[...]], axis=0)
    vv = jnp.concatenate([vp_ref[...], vc_ref[...]], axis=0)
    lane_head = lax.broadcasted_iota(jnp.int32, (w, 2 * LANES), 1) // SWA_HEAD_DIM
    parts = []
    for g in range(SWA_GROUP):
        qg = q_ref[:, g * 256:(g + 1) * 256]
        for h in range(SWA_KV_HEADS):
            parts.append(jnp.where(lane_head == h, qg, jnp.zeros_like(qg)))
    qs = jnp.concatenate(parts, axis=0)
    s = lax.dot_general(qs, kk, (((1,), (1,)), ((), ())), preferred_element_type=F32)
    s = s + bias_scr[...]
    sink = sink_scr[...]
    m = jnp.maximum(jnp.max(s, axis=1, keepdims=True), sink)
    p = jnp.exp(s - jnp.tile(m, (1, 2)))
    denom = jnp.sum(p, axis=1, keepdims=True) + jnp.exp(sink - m)
    o = jnp.dot(p.astype(BF16), vv, preferred_element_type=F32)
    o = o * jnp.tile(1.0 / denom, (1, 2))
    for g in range(SWA_GROUP):
        out_g = jnp.zeros((w, 2 * LANES), F32)
        for h in range(SWA_KV_HEADS):
            blk = g * SWA_KV_HEADS + h
            out_g = jnp.where(lane_head == h, o[blk * w:(blk + 1) * w, :], out_g)
        o_ref[:, g * 256:(g + 1) * 256] = out_g.astype(BF16)


def _swa(proj, sinks):
    s = proj.shape[0]
    w = WINDOW
    nb = s // w
    kcol = COL_SK // SWA_KV_WIDTH
    vcol = COL_SV // SWA_KV_WIDTH
    grid_spec = pltpu.PrefetchScalarGridSpec(
        num_scalar_prefetch=1,
        grid=(nb,),
        in_specs=[pl.BlockSpec((w, SWA_WIDTH), lambda n, sk: (n, COL_SQ // SWA_WIDTH)),
                  pl.BlockSpec((w, SWA_KV_WIDTH), lambda n, sk: (jnp.maximum(n - 1, 0), kcol)),
                  pl.BlockSpec((w, SWA_KV_WIDTH), lambda n, sk: (n, kcol)),
                  pl.BlockSpec((w, SWA_KV_WIDTH), lambda n, sk: (jnp.maximum(n - 1, 0), vcol)),
                  pl.BlockSpec((w, SWA_KV_WIDTH), lambda n, sk: (n, vcol))],
        out_specs=pl.BlockSpec((w, SWA_WIDTH), lambda n, sk: (n, 0)),
        scratch_shapes=[pltpu.VMEM((SWA_Q_HEADS * w, 2 * w), F32),
                        pltpu.VMEM((SWA_Q_HEADS * w, LANES), F32)],
    )
    return pl.pallas_call(
        _swa_kernel,
        out_shape=jax.ShapeDtypeStruct((s, SWA_WIDTH), BF16),
        grid_spec=grid_spec,
        compiler_params=pltpu.CompilerParams(dimension_semantics=("arbitrary",),
                                             vmem_limit_bytes=VMEM_LIMIT),
        name="swa",
    )(sinks, proj, proj, proj, proj, proj)


OUT_TM = 256


def _out_kernel(of_ref, gf_ref, os_ref, gs_ref, mf_ref, ms_ref, x_ref, gate_ref,
                wbf_ref, wbs_ref, wo_ref, lng_ref, lnb_ref, o_ref):
    af = (of_ref[...].astype(F32) * jax.nn.silu(gf_ref[...].astype(F32))).astype(BF16)
    yf = jnp.dot(af, wbf_ref[...], preferred_element_type=F32)
    a_s = (os_ref[...].astype(F32) * jax.nn.silu(gs_ref[...].astype(F32))).astype(BF16)
    ys = jnp.dot(a_s, wbs_ref[...], preferred_element_type=F32)
    merged = (jax.nn.sigmoid(mf_ref[...].astype(F32)) * yf
              + jax.nn.sigmoid(ms_ref[...].astype(F32)) * ys)
    sub = jnp.dot(merged.astype(BF16), wo_ref[...], preferred_element_type=F32)
    z = DEEPNORM_ALPHA * x_ref[...] + gate_ref[...] * sub
    mu = jnp.mean(z, axis=-1, keepdims=True)
    zc = z - mu
    var = jnp.mean(zc * zc, axis=-1, keepdims=True)
    o_ref[...] = zc * lax.rsqrt(var + LN_EPS) * lng_ref[...] + lnb_ref[...]


def _out(proj, o_fox, o_swa, x2, ada, wbf, wbs, wo, ln_g, ln_b):
    s = x2.shape[0]
    tm = OUT_TM
    const = lambda i: (0, 0)
    return pl.pallas_call(
        _out_kernel,
        out_shape=jax.ShapeDtypeStruct((s, D_MODEL), F32),
        grid=(s // tm,),
        in_specs=[pl.BlockSpec((tm, FOX_WIDTH), lambda i: (i, 0)),
                  pl.BlockSpec((tm, FOX_WIDTH), lambda i: (i, COL_GF // FOX_WIDTH)),
                  pl.BlockSpec((tm, SWA_WIDTH), lambda i: (i, 0)),
                  pl.BlockSpec((tm, SWA_WIDTH), lambda i: (i, COL_GS // SWA_WIDTH)),
                  pl.BlockSpec((tm, D_MODEL), lambda i: (i, COL_MF // D_MODEL)),
                  pl.BlockSpec((tm, D_MODEL), lambda i: (i, COL_MS // D_MODEL)),
                  pl.BlockSpec((tm, D_MODEL), lambda i: (i, 0)),
                  pl.BlockSpec((1, D_MODEL), lambda i: (0, 2)),
                  pl.BlockSpec((FOX_WIDTH, D_MODEL), const),
                  pl.BlockSpec((SWA_WIDTH, D_MODEL), const),
                  pl.BlockSpec((D_MODEL, D_MODEL), const),
                  pl.BlockSpec((1, D_MODEL), const),
                  pl.BlockSpec((1, D_MODEL), const)],
        out_specs=pl.BlockSpec((tm, D_MODEL), lambda i: (i, 0)),
        compiler_params=pltpu.CompilerParams(dimension_semantics=("arbitrary",),
                                             vmem_limit_bytes=VMEM_LIMIT),
        name="out",
    )(o_fox, proj, o_swa, proj, proj, proj, x2, ada, wbf, wbs, wo, ln_g, ln_b)


def kernel(x, c, w_ada, b_ada, w_in, b_f, attn_sinks, w_br_fox, w_br_swa, w_out, ln_g, ln_b):
    b, s, d = x.shape
    assert (b, s, d) == (1, SEQ, D_MODEL) and w_in.shape[0] == DEPTH
    x2 = x.reshape(s, d)

    wi = w_in[0]
    seg = lambda o, n: wi[:, o:o + n]
    w_main = jnp.concatenate([
        seg(_O_FQ, 1024), seg(_O_FK, 1024), seg(_O_FV, 1024),
        _swa_cols(seg(_O_SQ, 1024)),
        seg(_O_GF, 1024), _swa_cols(seg(_O_GS, 1024)),
        seg(_O_MF, 2048), seg(_O_MS, 2048),
        seg(_O_SK, 256), seg(_O_SV, 256)], axis=1).astype(BF16)
    w_flog = jnp.pad(seg(_O_FLOG, FOX_HEADS), ((0, 0), (0, LANES - FOX_HEADS))).astype(BF16)
    bf_pad = jnp.pad(b_f[0], (0, LANES - FOX_HEADS)).reshape(1, LANES)
    col_scale = jnp.asarray(_proj_col_scale())
    wbf = w_br_fox[0].astype(BF16)
    wbs = _swa_rows(w_br_swa[0]).astype(BF16)
    wo = w_out[0].astype(BF16)

    ada = _ada(c, w_ada[0], b_ada[0].reshape(1, -1))
    proj, flog = _proj(x2, ada, w_main, w_flog, col_scale)
    f_cum = _cum(flog, bf_pad)
    o_fox = _fox(proj, f_cum.reshape(FOX_HEADS, 1, s))
    o_swa = _swa(proj, attn_sinks[0])
    out = _out(proj, o_fox, o_swa, x2, ada, wbf, wbs, wo,
               ln_g[0].reshape(1, d), ln_b[0].reshape(1, d))
    return out.reshape(b, s, d)
```

```python
import math

import numpy as np
import jax
import jax.numpy as jnp
from jax import lax
from jax.experimental import pallas as pl
from jax.experimental.pallas import tpu as pltpu

F32 = jnp.float32
BF16 = jnp.bfloat16

D_MODEL = 2048
SEQ = 8192
FOX_HEADS = 8
FOX_HEAD_DIM = 128
FOX_WIDTH = FOX_HEADS * FOX_HEAD_DIM
SWA_Q_HEADS = 16
SWA_KV_HEADS = 4
SWA_HEAD_DIM = 64
SWA_GROUP = SWA_Q_HEADS // SWA_KV_HEADS
SWA_WIDTH = SWA_Q_HEADS * SWA_HEAD_DIM
SWA_KV_WIDTH = SWA_KV_HEADS * SWA_HEAD_DIM
WINDOW = 128
LN_EPS = 1e-5
NEG_INF = -1e30
DEPTH = 1
DEEPNORM_ALPHA = (2.0 * DEPTH) ** 0.25
LOG2E = math.log2(math.e)

LANES = 128
VMEM_LIMIT = 56 * 1024 * 1024

COL_FQ = 0
COL_FK = 1024
COL_FV = 2048
COL_SQ = 3072
COL_GF = 4096
COL_GS = 5120
COL_MF = 6144
COL_MS = 8192
COL_SK = 10240
COL_SV = 10496
PROJ_WIDTH = 10752

_O_FQ, _O_FK, _O_FV, _O_FLOG = 0, 1024, 2048, 3072
_O_SQ, _O_SK, _O_SV = 3080, 4104, 4360
_O_GF, _O_GS, _O_MF, _O_MS = 4616, 5640, 6664, 8712


def _swa_cols(w):
    r = w.shape[0]
    w4 = w.reshape(r, SWA_KV_HEADS, SWA_GROUP, SWA_HEAD_DIM)
    return w4.transpose(0, 2, 1, 3).reshape(r, SWA_WIDTH)


def _swa_rows(w):
    c = w.shape[1]
    w4 = w.reshape(SWA_KV_HEADS, SWA_GROUP, SWA_HEAD_DIM, c)
    return w4.transpose(1, 0, 2, 3).reshape(SWA_WIDTH, c)


def _proj_col_scale():
    s = np.ones((1, PROJ_WIDTH), np.float32)
    s[0, COL_FQ:COL_FQ + 1024] = FOX_HEAD_DIM ** -0.5 * LOG2E
    s[0, COL_SQ:COL_SQ + 1024] = SWA_HEAD_DIM ** -0.5
    return s


WP_TN = 512
WP_RC = 256
WP_SHIFT = FOX_HEADS
_WP_ALIGNED, _WP_SHIFTED, _WP_PERM = 0, 1, 2


def _wprep_tables():
    nb = PROJ_WIDTH // WP_TN
    src = np.zeros((nb,), np.int32)
    prm = np.zeros((nb,), np.int32)
    mode = np.zeros((nb,), np.int32)

    def fill(col_out, col_src, width, m):
        for t in range(width // WP_TN):
            ob = col_out // WP_TN + t
            mode[ob] = m
            if m == _WP_PERM:
                prm[ob] = col_src // WP_TN + t
            else:
                assert (col_src - (WP_SHIFT if m == _WP_SHIFTED else 0)) % WP_TN == 0
                src[ob] = col_src // WP_TN + t
    fill(COL_FQ, _O_FQ, 3072, _WP_ALIGNED)
    fill(COL_SQ, 0, 1024, _WP_PERM)
    fill(COL_GF, _O_GF, 1024, _WP_SHIFTED)
    fill(COL_GS, 1024, 1024, _WP_PERM)
    fill(COL_MF, _O_MF, 2048, _WP_SHIFTED)
    fill(COL_MS, _O_MS, 2048, _WP_SHIFTED)
    fill(COL_SK, _O_SK, 512, _WP_SHIFTED)
    for ob in range(1, nb):
        if mode[ob] == _WP_PERM:
            src[ob] = src[ob - 1]
        else:
            prm[ob] = prm[ob - 1]
    return src, prm, mode


def _wprep_kernel(src_ref, prm_ref, mode_ref, lo_ref, hi_ref, perm_ref, o_ref):
    mode = mode_ref[pl.program_id(0)]
    nchunk = D_MODEL // WP_RC

    def rows_of(r):
        return pl.ds(pl.multiple_of(r * WP_RC, WP_RC), WP_RC)

    @pl.when(mode == _WP_ALIGNED)
    def _():
        def body(r, carry):
            o_ref[rows_of(r), :] = lo_ref[rows_of(r), :].astype(BF16)
            return carry
        lax.fori_loop(0, nchunk, body, 0)

    @pl.when(mode == _WP_SHIFTED)
    def _():
        def body(r, carry):
            win = jnp.concatenate([lo_ref[rows_of(r), :], hi_ref[rows_of(r), :]], axis=1)
            win = pltpu.roll(win, WP_TN + LANES - WP_SHIFT, axis=1)
            o_ref[rows_of(r), :] = win[:, :WP_TN].astype(BF16)
            return carry
        lax.fori_loop(0, nchunk, body, 0)

    @pl.when(mode == _WP_PERM)
    def _():
        o_ref[...] = perm_ref[...]


def _wprep(wi, w_perm):
    src, prm, mode = _wprep_tables()
    grid_spec = pltpu.PrefetchScalarGridSpec(
        num_scalar_prefetch=3,
        grid=(PROJ_WIDTH // WP_TN,),
        in_specs=[pl.BlockSpec((D_MODEL, WP_TN), lambda b, s, p, m: (0, s[b])),
                  pl.BlockSpec((D_MODEL, LANES), lambda b, s, p, m: (0, (s[b] + 1) * (WP_TN // LANES))),
                  pl.BlockSpec((D_MODEL, WP_TN), lambda b, s, p, m: (0, p[b]))],
        out_specs=pl.BlockSpec((D_MODEL, WP_TN), lambda b, s, p, m: (0, b)),
    )
    return pl.pallas_call(
        _wprep_kernel,
        out_shape=jax.ShapeDtypeStruct((D_MODEL, PROJ_WIDTH), BF16),
        grid_spec=grid_spec,
        compiler_params=pltpu.CompilerParams(dimension_semantics=("arbitrary",),
                                             vmem_limit_bytes=VMEM_LIMIT),
        name="wprep",
    )(jnp.asarray(src), jnp.asarray(prm), jnp.asarray(mode), wi, wi, w_perm)


def _ada_kernel(c_ref, w_ref, b_ref, o_ref):
    c8 = jnp.broadcast_to(c_ref[...], (8, D_MODEL))
    r = jnp.dot(c8, w_ref[...], preferred_element_type=F32)
    o_ref[...] = r[0:1, :] + b_ref[...]


def _ada(c, w_ada, b_ada):
    tn = 512
    n = w_ada.shape[1]
    return pl.pallas_call(
        _ada_kernel,
        out_shape=jax.ShapeDtypeStruct((1, n), F32),
        grid=(n // tn,),
        in_specs=[pl.BlockSpec((1, D_MODEL), lambda j: (0, 0)),
                  pl.BlockSpec((D_MODEL, tn), lambda j: (0, j)),
                  pl.BlockSpec((1, tn), lambda j: (0, j))],
        out_specs=pl.BlockSpec((1, tn), lambda j: (0, j)),
        compiler_params=pltpu.CompilerParams(dimension_semantics=("arbitrary",),
                                             vmem_limit_bytes=VMEM_LIMIT),
        name="ada",
    )(c, w_ada, b_ada)


PROJ_TM = 1024
PROJ_TN = 1536
PROJ_RC = 256


def _proj_kernel(x_ref, shift_ref, scale_ref, w_ref, wf_ref, cs_ref, o_ref, flog_ref, h_scr):
    j = pl.program_id(1)

    @pl.when(j == 0)
    def _():
        mod = 1.0 + scale_ref[...]
        shift = shift_ref[...]

        def body(r, carry):
            rows = pl.ds(pl.multiple_of(r * PROJ_RC, PROJ_RC), PROJ_RC)
            x = x_ref[rows, :]
            mu = jnp.mean(x, axis=-1, keepdims=True)
            xc = x - mu
            var = jnp.mean(xc * xc, axis=-1, keepdims=True)
            h = xc * lax.rsqrt(var + LN_EPS) * mod + shift
            h_scr[rows, :] = h.astype(BF16)
            return carry

        lax.fori_loop(0, PROJ_TM // PROJ_RC, body, 0)
        flog_ref[...] = jnp.dot(h_scr[...], wf_ref[...], preferred_element_type=F32)

    acc = jnp.dot(h_scr[...], w_ref[...], preferred_element_type=F32)
    o_ref[...] = (acc * cs_ref[...]).astype(BF16)


def _proj(x2, ada, w_main, w_flog, col_scale):
    s = x2.shape[0]
    return pl.pallas_call(
        _proj_kernel,
        out_shape=(jax.ShapeDtypeStruct((s, PROJ_WIDTH), BF16),
                   jax.ShapeDtypeStruct((s, LANES), F32)),
        grid=(s // PROJ_TM, PROJ_WIDTH // PROJ_TN),
        in_specs=[pl.BlockSpec((PROJ_TM, D_MODEL), lambda i, j: (i, 0)),
                  pl.BlockSpec((1, D_MODEL), lambda i, j: (0, 0)),
                  pl.BlockSpec((1, D_MODEL), lambda i, j: (0, 1)),
                  pl.BlockSpec((D_MODEL, PROJ_TN), lambda i, j: (0, j)),
                  pl.BlockSpec((D_MODEL, LANES), lambda i, j: (0, 0)),
                  pl.BlockSpec((1, PROJ_TN), lambda i, j: (0, j))],
        out_specs=(pl.BlockSpec((PROJ_TM, PROJ_TN), lambda i, j: (i, j)),
                   pl.BlockSpec((PROJ_TM, LANES), lambda i, j: (i, 0))),
        scratch_shapes=[pltpu.VMEM((PROJ_TM, D_MODEL), BF16)],
        compiler_params=pltpu.CompilerParams(dimension_semantics=("arbitrary", "arbitrary"),
                                             vmem_limit_bytes=VMEM_LIMIT),
        name="proj",
    )(x2, ada, ada, w_main, w_flog, col_scale)


def _cum_kernel(flog_ref, bf_ref, f_ref):
    s = flog_ref.shape[0]
    lf = jax.nn.log_sigmoid(flog_ref[...] + bf_ref[...])
    acc = lf.T[0:FOX_HEADS, :]
    lane = lax.broadcasted_iota(jnp.int32, acc.shape, 1)
    sh = 1
    while sh < s:
        rolled = pltpu.roll(acc, sh, axis=1)
        acc = acc + jnp.where(lane >= sh, rolled, 0.0)
        sh *= 2
    f_ref[...] = acc


def _cum(flog, bf_pad):
    s = flog.shape[0]
    return pl.pallas_call(
        _cum_kernel,
        out_shape=jax.ShapeDtypeStruct((FOX_HEADS, s), F32),
        in_specs=[pl.BlockSpec((s, LANES), lambda: (0, 0)),
                  pl.BlockSpec((1, LANES), lambda: (0, 0))],
        out_specs=pl.BlockSpec((FOX_HEADS, s), lambda: (0, 0)),
        compiler_params=pltpu.CompilerParams(vmem_limit_bytes=VMEM_LIMIT),
        name="cum",
    )(flog, bf_pad)


FOX_TK = 512
FOX_TQ = 2 * FOX_TK


def _fox_kernel(q_ref, k_ref, v_ref, f_ref, o_ref, sa_scr, sb_scr, m_scr, acc_scr):
    i = pl.program_id(1)
    tq, tk = FOX_TQ, FOX_TK
    q0 = pl.multiple_of(i * tq, tq)
    f_q = f_ref[0, :, pl.ds(q0, tq)]
    f_base = jnp.max(f_q, axis=1, keepdims=True)
    ones = jnp.ones((tk, LANES), BF16)
    top, bot = slice(0, tk), slice(tk, tq)

    m_scr[...] = jnp.full(m_scr.shape, NEG_INF, F32)
    acc_scr[...] = jnp.zeros(acc_scr.shape, F32)

    def scores(k0, dst, rows=slice(0, FOX_TQ)):
        k_t = k_ref[pl.ds(k0, tk), :]
        s = lax.dot_general(q_ref[rows, :], k_t, (((1,), (1,)), ((), ())), preferred_element_type=F32)
        bias = (f_base - f_ref[0, :, pl.ds(k0, tk)]) * LOG2E
        dst[rows, :] = s + bias

    def softmax_pv(src, k0, rows=slice(0, FOX_TQ), causal=False):
        s = src[rows, :]
        if causal:
            n = rows.stop - rows.start
            keep = (lax.broadcasted_iota(jnp.int32, (n, tk), 1)
                    <= lax.broadcasted_iota(jnp.int32, (n, tk), 0))
            s = jnp.where(keep, s, NEG_INF)
        m_prev = m_scr[rows, :]
        m_new = jnp.maximum(m_prev, jnp.max(s, axis=1, keepdims=True))
        alpha = jnp.exp2(m_prev - m_new)
        p = jnp.exp2(s - jnp.tile(m_new, (1, tk // LANES))).astype(BF16)
        v_aug = jnp.concatenate([v_ref[pl.ds(k0, tk), :], ones], axis=1)
        pv = jnp.dot(p, v_aug, preferred_element_type=F32)
        acc_scr[rows, :] = acc_scr[rows, :] * jnp.tile(alpha, (1, 2)) + pv
        m_scr[rows, :] = m_new

    scores(0, sa_scr)

    def body(j, carry):
        ka = pl.multiple_of(j * tq, tq)
        kb = pl.multiple_of(ka + tk, tk)
        scores(kb, sb_scr)
        softmax_pv(sa_scr, ka)
        scores(pl.multiple_of(ka + tq, tk), sa_scr)
        softmax_pv(sb_scr, kb)
        return carry

    lax.fori_loop(0, i, body, 0)

    kd = pl.multiple_of(q0 + tk, tk)
    scores(kd, sb_scr, bot)
    softmax_pv(sa_scr, q0, top, causal=True)
    softmax_pv(sa_scr, q0, bot)
    softmax_pv(sb_scr, kd, bot, causal=True)

    acc = acc_scr[...]
    o_ref[...] = (acc[:, :FOX_HEAD_DIM] / acc[:, FOX_HEAD_DIM:]).astype(BF16)


def _fox(proj, f3):
    s = proj.shape[0]
    tq, tk = FOX_TQ, FOX_TK
    return pl.pallas_call(
        _fox_kernel,
        out_shape=jax.ShapeDtypeStruct((s, FOX_WIDTH), BF16),
        grid=(FOX_HEADS, s // tq),
        in_specs=[pl.BlockSpec((tq, LANES), lambda h, i: (i, COL_FQ // LANES + h)),
                  pl.BlockSpec((s, LANES), lambda h, i: (0, COL_FK // LANES + h)),
                  pl.BlockSpec((s, LANES), lambda h, i: (0, COL_FV // LANES + h)),
                  pl.BlockSpec((1, 1, s), lambda h, i: (h, 0, 0))],
        out_specs=pl.BlockSpec((tq, LANES), lambda h, i: (i, h)),
        scratch_shapes=[pltpu.VMEM((tq, tk), F32), pltpu.VMEM((tq, tk), F32),
                        pltpu.VMEM((tq, LANES), F32), pltpu.VMEM((tq, 2 * LANES), F32)],
        compiler_params=pltpu.CompilerParams(dimension_semantics=("arbitrary", "arbitrary"),
                                             vmem_limit_bytes=VMEM_LIMIT),
        name="fox",
    )(proj, proj, proj, f3)


def _swa_kernel(sink_ref, q_ref, kp_ref, kc_ref, vp_ref, vc_ref, o_ref, bias_scr, sink_scr):
    n = pl.program_id(0)
    w = WINDOW

    @pl.when(n <= 1)
    def _():
        row = lax.broadcasted_iota(jnp.int32, (w, 2 * w), 0)
        col = lax.broadcasted_iota(jnp.int32, (w, 2 * w), 1)
        dist = row - col + w
        valid = (dist >= 0) & (dist < w) & ((col >= w) | (n > 0))
        distf = dist.astype(F32)
        for g in range(SWA_GROUP):
            for h in range(SWA_KV_HEADS):
                blk = g * SWA_KV_HEADS + h
                hq = h * SWA_GROUP + g
                slope = 2.0 ** (-8.0 * (hq + 1.0) / SWA_Q_HEADS)
                bias_scr[blk * w:(blk + 1) * w, :] = jnp.where(valid, -slope * distf, NEG_INF)
                sink_scr[blk * w:(blk + 1) * w, :] = jnp.full((w, LANES), sink_ref[hq], F32)

    kk = jnp.concatenate([kp_ref[...], kc_ref[...]], axis=0)
    vv = jnp.concatenate([vp_ref[...], vc_ref[...]], axis=0)
    lane_head = lax.broadcasted_iota(jnp.int32, (w, 2 * LANES), 1) // SWA_HEAD_DIM
    parts = []
    for g in range(SWA_GROUP):
        qg = q_ref[:, g * 256:(g + 1) * 256]
        for h in range(SWA_KV_HEADS):
            parts.append(jnp.where(lane_head == h, qg, jnp.zeros_like(qg)))
    qs = jnp.concatenate(parts, axis=0)
    s = lax.dot_general(qs, kk, (((1,), (1,)), ((), ())), preferred_element_type=F32)
    s = s + bias_scr[...]
    sink = sink_scr[...]
    m = jnp.maximum(jnp.max(s, axis=1, keepdims=True), sink)
    p = jnp.exp(s - jnp.tile(m, (1, 2)))
    denom = jnp.sum(p, axis=1, keepdims=True) + jnp.exp(sink - m)
    o = jnp.dot(p.astype(BF16), vv, preferred_element_type=F32)
    o = o * jnp.tile(1.0 / denom, (1, 2))
    for g in range(SWA_GROUP):
        out_g = jnp.zeros((w, 2 * LANES), F32)
        for h in range(SWA_KV_HEADS):
            blk = g * SWA_KV_HEADS + h
            out_g = jnp.where(lane_head == h, o[blk * w:(blk + 1) * w, :], out_g)
        o_ref[:, g * 256:(g + 1) * 256] = out_g.astype(BF16)


def _swa(proj, sinks):
    s = proj.shape[0]
    w = WINDOW
    nb = s // w
    kcol = COL_SK // SWA_KV_WIDTH
    vcol = COL_SV // SWA_KV_WIDTH
    grid_spec = pltpu.PrefetchScalarGridSpec(
        num_scalar_prefetch=1,
        grid=(nb,),
        in_specs=[pl.BlockSpec((w, SWA_WIDTH), lambda n, sk: (n, COL_SQ // SWA_WIDTH)),
                  pl.BlockSpec((w, SWA_KV_WIDTH), lambda n, sk: (jnp.maximum(n - 1, 0), kcol)),
                  pl.BlockSpec((w, SWA_KV_WIDTH), lambda n, sk: (n, kcol)),
                  pl.BlockSpec((w, SWA_KV_WIDTH), lambda n, sk: (jnp.maximum(n - 1, 0), vcol)),
                  pl.BlockSpec((w, SWA_KV_WIDTH), lambda n, sk: (n, vcol))],
        out_specs=pl.BlockSpec((w, SWA_WIDTH), lambda n, sk: (n, 0)),
        scratch_shapes=[pltpu.VMEM((SWA_Q_HEADS * w, 2 * w), F32),
                        pltpu.VMEM((SWA_Q_HEADS * w, LANES), F32)],
    )
    return pl.pallas_call(
        _swa_kernel,
        out_shape=jax.ShapeDtypeStruct((s, SWA_WIDTH), BF16),
        grid_spec=grid_spec,
        compiler_params=pltpu.CompilerParams(dimension_semantics=("arbitrary",),
                                             vmem_limit_bytes=VMEM_LIMIT),
        name="swa",
    )(sinks, proj, proj, proj, proj, proj)


OUT_TM = 256


def _out_kernel(of_ref, gf_ref, os_ref, gs_ref, mf_ref, ms_ref, x_ref, gate_ref,
                wbf_ref, wbs_ref, wo_ref, lng_ref, lnb_ref, o_ref):
    af = (of_ref[...].astype(F32) * jax.nn.silu(gf_ref[...].astype(F32))).astype(BF16)
    yf = jnp.dot(af, wbf_ref[...], preferred_element_type=F32)
    a_s = (os_ref[...].astype(F32) * jax.nn.silu(gs_ref[...].astype(F32))).astype(BF16)
    ys = jnp.dot(a_s, wbs_ref[...], preferred_element_type=F32)
    merged = (jax.nn.sigmoid(mf_ref[...].astype(F32)) * yf
              + jax.nn.sigmoid(ms_ref[...].astype(F32)) * ys)
    sub = jnp.dot(merged.astype(BF16), wo_ref[...], preferred_element_type=F32)
    z = DEEPNORM_ALPHA * x_ref[...] + gate_ref[...] * sub
    mu = jnp.mean(z, axis=-1, keepdims=True)
    zc = z - mu
    var = jnp.mean(zc * zc, axis=-1, keepdims=True)
    o_ref[...] = zc * lax.rsqrt(var + LN_EPS) * lng_ref[...] + lnb_ref[...]


def _out(proj, o_fox, o_swa, x2, ada, wbf, wbs, wo, ln_g, ln_b):
    s = x2.shape[0]
    tm = OUT_TM
    const = lambda i: (0, 0)
    return pl.pallas_call(
        _out_kernel,
        out_shape=jax.ShapeDtypeStruct((s, D_MODEL), F32),
        grid=(s // tm,),
        in_specs=[pl.BlockSpec((tm, FOX_WIDTH), lambda i: (i, 0)),
                  pl.BlockSpec((tm, FOX_WIDTH), lambda i: (i, COL_GF // FOX_WIDTH)),
                  pl.BlockSpec((tm, SWA_WIDTH), lambda i: (i, 0)),
                  pl.BlockSpec((tm, SWA_WIDTH), lambda i: (i, COL_GS // SWA_WIDTH)),
                  pl.BlockSpec((tm, D_MODEL), lambda i: (i, COL_MF // D_MODEL)),
                  pl.BlockSpec((tm, D_MODEL), lambda i: (i, COL_MS // D_MODEL)),
                  pl.BlockSpec((tm, D_MODEL), lambda i: (i, 0)),
                  pl.BlockSpec((1, D_MODEL), lambda i: (0, 2)),
                  pl.BlockSpec((FOX_WIDTH, D_MODEL), const),
                  pl.BlockSpec((SWA_WIDTH, D_MODEL), const),
                  pl.BlockSpec((D_MODEL, D_MODEL), const),
                  pl.BlockSpec((1, D_MODEL), const),
                  pl.BlockSpec((1, D_MODEL), const)],
        out_specs=pl.BlockSpec((tm, D_MODEL), lambda i: (i, 0)),
        compiler_params=pltpu.CompilerParams(dimension_semantics=("arbitrary",),
                                             vmem_limit_bytes=VMEM_LIMIT),
        name="out",
    )(o_fox, proj, o_swa, proj, proj, proj, x2, ada, wbf, wbs, wo, ln_g, ln_b)


def kernel(x, c, w_ada, b_ada, w_in, b_f, attn_sinks, w_br_fox, w_br_swa, w_out, ln_g, ln_b):
    b, s, d = x.shape
    assert (b, s, d) == (1, SEQ, D_MODEL) and w_in.shape[0] == DEPTH
    x2 = x.reshape(s, d)

    wi = w_in[0]
    seg = lambda o, n: wi[:, o:o + n]
    w_perm = jnp.concatenate([_swa_cols(seg(_O_SQ, 1024)), _swa_cols(seg(_O_GS, 1024))], axis=1).astype(BF16)
    w_main = _wprep(wi, w_perm)
    w_flog = jnp.pad(seg(_O_FLOG, FOX_HEADS), ((0, 0), (0, LANES - FOX_HEADS))).astype(BF16)
    bf_pad = jnp.pad(b_f[0], (0, LANES - FOX_HEADS)).reshape(1, LANES)
    col_scale = jnp.asarray(_proj_col_scale())
    wbf = w_br_fox[0].astype(BF16)
    wbs = _swa_rows(w_br_swa[0]).astype(BF16)
    wo = w_out[0].astype(BF16)

    ada = _ada(c, w_ada[0], b_ada[0].reshape(1, -1))
    proj, flog = _proj(x2, ada, w_main, w_flog, col_scale)
    f_cum = _cum(flog, bf_pad)
    o_fox = _fox(proj, f_cum.reshape(FOX_HEADS, 1, s))
    o_swa = _swa(proj, attn_sinks[0])
    out = _out(proj, o_fox, o_swa, x2, ada, wbf, wbs, wo,
               ln_g[0].reshape(1, d), ln_b[0].reshape(1, d))
    return out.reshape(b, s, d)
```

```python
import math

import numpy as np
import jax
import jax.numpy as jnp
from jax import lax
from jax.experimental import pallas as pl
from jax.experimental.pallas import tpu as pltpu

F32 = jnp.float32
BF16 = jnp.bfloat16

D_MODEL = 2048
SEQ = 8192
FOX_HEADS = 8
FOX_HEAD_DIM = 128
FOX_WIDTH = FOX_HEADS * FOX_HEAD_DIM
SWA_Q_HEADS = 16
SWA_KV_HEADS = 4
SWA_HEAD_DIM = 64
SWA_GROUP = SWA_Q_HEADS // SWA_KV_HEADS
SWA_WIDTH = SWA_Q_HEADS * SWA_HEAD_DIM
SWA_KV_WIDTH = SWA_KV_HEADS * SWA_HEAD_DIM
WINDOW = 128
LN_EPS = 1e-5
NEG_INF = -1e30
DEPTH = 1
DEEPNORM_ALPHA = (2.0 * DEPTH) ** 0.25
LOG2E = math.log2(math.e)

LANES = 128
VMEM_LIMIT = 56 * 1024 * 1024

COL_FQ = 0
COL_FK = 1024
COL_FV = 2048
COL_SQ = 3072
COL_GF = 4096
COL_GS = 5120
COL_MF = 6144
COL_MS = 8192
COL_SK = 10240
COL_SV = 10496
PROJ_WIDTH = 10752

_O_FQ, _O_FK, _O_FV, _O_FLOG = 0, 1024, 2048, 3072
_O_SQ, _O_SK, _O_SV = 3080, 4104, 4360
_O_GF, _O_GS, _O_MF, _O_MS = 4616, 5640, 6664, 8712


def _swa_cols(w):
    r = w.shape[0]
    w4 = w.reshape(r, SWA_KV_HEADS, SWA_GROUP, SWA_HEAD_DIM)
    return w4.transpose(0, 2, 1, 3).reshape(r, SWA_WIDTH)


def _swa_rows(w):
    c = w.shape[1]
    w4 = w.reshape(SWA_KV_HEADS, SWA_GROUP, SWA_HEAD_DIM, c)
    return w4.transpose(1, 0, 2, 3).reshape(SWA_WIDTH, c)


def _proj_col_scale():
    s = np.ones((1, PROJ_WIDTH), np.float32)
    s[0, COL_FQ:COL_FQ + 1024] = FOX_HEAD_DIM ** -0.5 * LOG2E
    s[0, COL_SQ:COL_SQ + 1024] = SWA_HEAD_DIM ** -0.5
    return s


WP_TN = 512
WP_TC = 256
WP_SHIFT = FOX_HEADS
_WP_ALIGNED, _WP_SHIFTED, _WP_PERM0, _WP_PERM1 = 0, 1, 2, 3


def _wprep_tables():
    nb = PROJ_WIDTH // WP_TN
    blk_a = np.zeros((nb,), np.int32)
    blk_b = np.zeros((nb,), np.int32)
    blk_h = np.zeros((nb,), np.int32)
    mode = np.zeros((nb,), np.int32)

    def fill(col_out, col_src, width, shifted):
        for t in range(width // WP_TN):
            ob = col_out // WP_TN + t
            start = col_src + t * WP_TN - (WP_SHIFT if shifted else 0)
            assert start % WP_TN == 0
            mode[ob] = _WP_SHIFTED if shifted else _WP_ALIGNED
            blk_a[ob] = start // WP_TN
            blk_h[ob] = (start + WP_TN) // WP_SHIFT
            blk_b[ob] = -1

    def fill_perm(col_out, col_src):
        start = col_src - WP_SHIFT
        assert start % WP_TN == 0
        for t, m in enumerate((_WP_PERM0, _WP_PERM1)):
            ob = col_out // WP_TN + t
            mode[ob] = m
            blk_a[ob] = start // WP_TN
            blk_b[ob] = start // WP_TN + 1
            blk_h[ob] = (start + 2 * WP_TN) // WP_SHIFT
    fill(COL_FQ, _O_FQ, 3072, False)
    fill_perm(COL_SQ, _O_SQ)
    fill(COL_GF, _O_GF, 1024, True)
    fill_perm(COL_GS, _O_GS)
    fill(COL_MF, _O_MF, 2048, True)
    fill(COL_MS, _O_MS, 2048, True)
    fill(COL_SK, _O_SK, 512, True)
    for ob in range(nb):
        if blk_b[ob] < 0:
            blk_b[ob] = blk_b[ob - 1] if ob else 0
    return blk_a, blk_b, blk_h, mode


def _wprep_kernel(a_tab, b_tab, h_tab, mode_ref, a_ref, b_ref, h_ref, o_ref):
    mode = mode_ref[pl.program_id(0)]

    def emit(rows_of_chunk):
        for c in range(D_MODEL // WP_TC):
            cols = slice(c * WP_TC, (c + 1) * WP_TC)
            o_ref[cols, :] = rows_of_chunk(cols).T.astype(BF16)

    @pl.when(mode == _WP_ALIGNED)
    def _():
        emit(lambda cols: a_ref[:, cols])

    @pl.when(mode == _WP_SHIFTED)
    def _():
        emit(lambda cols: jnp.concatenate([a_ref[WP_SHIFT:, cols], h_ref[:, cols]], axis=0))

    def perm_rows(cols, t):
        parts = []
        for g in (2 * t, 2 * t + 1):
            for h in range(SWA_KV_HEADS):
                r0 = WP_SHIFT + (h * SWA_GROUP + g) * SWA_HEAD_DIM
                r1 = r0 + SWA_HEAD_DIM
                if r1 <= WP_TN:
                    parts.append(a_ref[r0:r1, cols])
                elif r0 >= WP_TN:
                    if r1 <= 2 * WP_TN:
                        parts.append(b_ref[r0 - WP_TN:r1 - WP_TN, cols])
                    else:
                        parts.append(jnp.concatenate([b_ref[r0 - WP_TN:, cols], h_ref[:, cols]], axis=0))
                else:
                    parts.append(jnp.concatenate([a_ref[r0:, cols], b_ref[:r1 - WP_TN, cols]], axis=0))
        return jnp.concatenate(parts, axis=0)

    @pl.when(mode == _WP_PERM0)
    def _():
        emit(lambda cols: perm_rows(cols, 0))

    @pl.when(mode == _WP_PERM1)
    def _():
        emit(lambda cols: perm_rows(cols, 1))


def _wprep(wt):
    tabs = _wprep_tables()
    grid_spec = pltpu.PrefetchScalarGridSpec(
        num_scalar_prefetch=4,
        grid=(PROJ_WIDTH // WP_TN,),
        in_specs=[pl.BlockSpec((WP_TN, D_MODEL), lambda i, a, b, h, m: (a[i], 0)),
                  pl.BlockSpec((WP_TN, D_MODEL), lambda i, a, b, h, m: (b[i], 0)),
                  pl.BlockSpec((WP_SHIFT, D_MODEL), lambda i, a, b, h, m: (h[i], 0))],
        out_specs=pl.BlockSpec((D_MODEL, WP_TN), lambda i, a, b, h, m: (0, i)),
    )
    return pl.pallas_call(
        _wprep_kernel,
        out_shape=jax.ShapeDtypeStruct((D_MODEL, PROJ_WIDTH), BF16),
        grid_spec=grid_spec,
        compiler_params=pltpu.CompilerParams(dimension_semantics=("arbitrary",),
                                             vmem_limit_bytes=VMEM_LIMIT),
        name="wprep",
    )(*[jnp.asarray(t) for t in tabs], wt, wt, wt)


def _ada_kernel(c_ref, w_ref, b_ref, o_ref):
    c8 = jnp.broadcast_to(c_ref[...], (8, D_MODEL))
    r = jnp.dot(c8, w_ref[...], preferred_element_type=F32)
    o_ref[...] = r[0:1, :] + b_ref[...]


def _ada(c, w_ada, b_ada):
    tn = 512
    n = w_ada.shape[1]
    return pl.pallas_call(
        _ada_kernel,
        out_shape=jax.ShapeDtypeStruct((1, n), F32),
        grid=(n // tn,),
        in_specs=[pl.BlockSpec((1, D_MODEL), lambda j: (0, 0)),
                  pl.BlockSpec((D_MODEL, tn), lambda j: (0, j)),
                  pl.BlockSpec((1, tn), lambda j: (0, j))],
        out_specs=pl.BlockSpec((1, tn), lambda j: (0, j)),
        compiler_params=pltpu.CompilerParams(dimension_semantics=("arbitrary",),
                                             vmem_limit_bytes=VMEM_LIMIT),
        name="ada",
    )(c, w_ada, b_ada)


PROJ_TM = 1024
PROJ_TN = 1536
PROJ_RC = 256


def _proj_kernel(x_ref, shift_ref, scale_ref, w_ref, wf_ref, cs_ref, o_ref, flog_ref, h_scr):
    j = pl.program_id(1)

    @pl.when(j == 0)
    def _():
        mod = 1.0 + scale_ref[...]
        shift = shift_ref[...]

        def body(r, carry):
            rows = pl.ds(pl.multiple_of(r * PROJ_RC, PROJ_RC), PROJ_RC)
            x = x_ref[rows, :]
            mu = jnp.mean(x, axis=-1, keepdims=True)
            xc = x - mu
            var = jnp.mean(xc * xc, axis=-1, keepdims=True)
            h = xc * lax.rsqrt(var + LN_EPS) * mod + shift
            h_scr[rows, :] = h.astype(BF16)
            return carry

        lax.fori_loop(0, PROJ_TM // PROJ_RC, body, 0)
        wf = jnp.concatenate([wf_ref[...], jnp.zeros((LANES - FOX_HEADS, D_MODEL), F32)], axis=0)
        flog_ref[...] = lax.dot_general(h_scr[...], wf.astype(BF16), (((1,), (1,)), ((), ())),
                                        preferred_element_type=F32)

    acc = jnp.dot(h_scr[...], w_ref[...], preferred_element_type=F32)
    o_ref[...] = (acc * cs_ref[...]).astype(BF16)


def _proj(x2, ada, w_main, w_flog, col_scale):
    s = x2.shape[0]
    return pl.pallas_call(
        _proj_kernel,
        out_shape=(jax.ShapeDtypeStruct((s, PROJ_WIDTH), BF16),
                   jax.ShapeDtypeStruct((s, LANES), F32)),
        grid=(s // PROJ_TM, PROJ_WIDTH // PROJ_TN),
        in_specs=[pl.BlockSpec((PROJ_TM, D_MODEL), lambda i, j: (i, 0)),
                  pl.BlockSpec((1, D_MODEL), lambda i, j: (0, 0)),
                  pl.BlockSpec((1, D_MODEL), lambda i, j: (0, 1)),
                  pl.BlockSpec((D_MODEL, PROJ_TN), lambda i, j: (0, j)),
                  pl.BlockSpec((FOX_HEADS, D_MODEL), lambda i, j: (_O_FLOG // FOX_HEADS, 0)),
                  pl.BlockSpec((1, PROJ_TN), lambda i, j: (0, j))],
        out_specs=(pl.BlockSpec((PROJ_TM, PROJ_TN), lambda i, j: (i, j)),
                   pl.BlockSpec((PROJ_TM, LANES), lambda i, j: (i, 0))),
        scratch_shapes=[pltpu.VMEM((PROJ_TM, D_MODEL), BF16)],
        compiler_params=pltpu.CompilerParams(dimension_semantics=("arbitrary", "arbitrary"),
                                             vmem_limit_bytes=VMEM_LIMIT),
        name="proj",
    )(x2, ada, ada, w_main, w_flog, col_scale)


def _cum_kernel(flog_ref, bf_ref, f_ref):
    s = flog_ref.shape[0]
    lf = jax.nn.log_sigmoid(flog_ref[...] + bf_ref[...])
    acc = lf.T[0:FOX_HEADS, :]
    lane = lax.broadcasted_iota(jnp.int32, acc.shape, 1)
    sh = 1
    while sh < s:
        rolled = pltpu.roll(acc, sh, axis=1)
        acc = acc + jnp.where(lane >= sh, rolled, 0.0)
        sh *= 2
    f_ref[...] = acc


def _cum(flog, bf_pad):
    s = flog.shape[0]
    return pl.pallas_call(
        _cum_kernel,
        out_shape=jax.ShapeDtypeStruct((FOX_HEADS, s), F32),
        in_specs=[pl.BlockSpec((s, LANES), lambda: (0, 0)),
                  pl.BlockSpec((1, LANES), lambda: (0, 0))],
        out_specs=pl.BlockSpec((FOX_HEADS, s), lambda: (0, 0)),
        compiler_params=pltpu.CompilerParams(vmem_limit_bytes=VMEM_LIMIT),
        name="cum",
    )(flog, bf_pad)


FOX_TK = 512
FOX_TQ = 2 * FOX_TK


def _fox_kernel(q_ref, k_ref, v_ref, f_ref, o_ref, sa_scr, sb_scr, m_scr, acc_scr):
    i = pl.program_id(1)
    tq, tk = FOX_TQ, FOX_TK
    q0 = pl.multiple_of(i * tq, tq)
    f_q = f_ref[0, :, pl.ds(q0, tq)]
    f_base = jnp.max(f_q, axis=1, keepdims=True)
    ones = jnp.ones((tk, LANES), BF16)
    top, bot = slice(0, tk), slice(tk, tq)

    m_scr[...] = jnp.full(m_scr.shape, NEG_INF, F32)
    acc_scr[...] = jnp.zeros(acc_scr.shape, F32)

    def scores(k0, dst, rows=slice(0, FOX_TQ)):
        k_t = k_ref[pl.ds(k0, tk), :]
        s = lax.dot_general(q_ref[rows, :], k_t, (((1,), (1,)), ((), ())), preferred_element_type=F32)
        bias = (f_base - f_ref[0, :, pl.ds(k0, tk)]) * LOG2E
        dst[rows, :] = s + bias

    def softmax_pv(src, k0, rows=slice(0, FOX_TQ), causal=False):
        s = src[rows, :]
        if causal:
            n = rows.stop - rows.start
            keep = (lax.broadcasted_iota(jnp.int32, (n, tk), 1)
                    <= lax.broadcasted_iota(jnp.int32, (n, tk), 0))
            s = jnp.where(keep, s, NEG_INF)
        m_prev = m_scr[rows, :]
        m_new = jnp.maximum(m_prev, jnp.max(s, axis=1, keepdims=True))
        alpha = jnp.exp2(m_prev - m_new)
        p = jnp.exp2(s - jnp.tile(m_new, (1, tk // LANES))).astype(BF16)
        v_aug = jnp.concatenate([v_ref[pl.ds(k0, tk), :], ones], axis=1)
        pv = jnp.dot(p, v_aug, preferred_element_type=F32)
        acc_scr[rows, :] = acc_scr[rows, :] * jnp.tile(alpha, (1, 2)) + pv
        m_scr[rows, :] = m_new

    scores(0, sa_scr)

    def body(j, carry):
        ka = pl.multiple_of(j * tq, tq)
        kb = pl.multiple_of(ka + tk, tk)
        scores(kb, sb_scr)
        softmax_pv(sa_scr, ka)
        scores(pl.multiple_of(ka + tq, tk), sa_scr)
        softmax_pv(sb_scr, kb)
        return carry

    lax.fori_loop(0, i, body, 0)

    kd = pl.multiple_of(q0 + tk, tk)
    scores(kd, sb_scr, bot)
    softmax_pv(sa_scr, q0, top, causal=True)
    softmax_pv(sa_scr, q0, bot)
    softmax_pv(sb_scr, kd, bot, causal=True)

    acc = acc_scr[...]
    o_ref[...] = (acc[:, :FOX_HEAD_DIM] / acc[:, FOX_HEAD_DIM:]).astype(BF16)


def _fox(proj, f3):
    s = proj.shape[0]
    tq, tk = FOX_TQ, FOX_TK
    return pl.pallas_call(
        _fox_kernel,
        out_shape=jax.ShapeDtypeStruct((s, FOX_WIDTH), BF16),
        grid=(FOX_HEADS, s // tq),
        in_specs=[pl.BlockSpec((tq, LANES), lambda h, i: (i, COL_FQ // LANES + h)),
                  pl.BlockSpec((s, LANES), lambda h, i: (0, COL_FK // LANES + h)),
                  pl.BlockSpec((s, LANES), lambda h, i: (0, COL_FV // LANES + h)),
                  pl.BlockSpec((1, 1, s), lambda h, i: (h, 0, 0))],
        out_specs=pl.BlockSpec((tq, LANES), lambda h, i: (i, h)),
        scratch_shapes=[pltpu.VMEM((tq, tk), F32), pltpu.VMEM((tq, tk), F32),
                        pltpu.VMEM((tq, LANES), F32), pltpu.VMEM((tq, 2 * LANES), F32)],
        compiler_params=pltpu.CompilerParams(dimension_semantics=("arbitrary", "arbitrary"),
                                             vmem_limit_bytes=VMEM_LIMIT),
        name="fox",
    )(proj, proj, proj, f3)


def _swa_kernel(sink_ref, q_ref, kp_ref, kc_ref, vp_ref, vc_ref, o_ref, bias_scr, sink_scr):
    n = pl.program_id(0)
    w = WINDOW

    @pl.when(n <= 1)
    def _():
        row = lax.broadcasted_iota(jnp.int32, (w, 2 * w), 0)
        col = lax.broadcasted_iota(jnp.int32, (w, 2 * w), 1)
        dist = row - col + w
        valid = (dist >= 0) & (dist < w) & ((col >= w) | (n > 0))
        distf = dist.astype(F32)
        for g in range(SWA_GROUP):
            for h in range(SWA_KV_HEADS):
                blk = g * SWA_KV_HEADS + h
                hq = h * SWA_GROUP + g
                slope = 2.0 ** (-8.0 * (hq + 1.0) / SWA_Q_HEADS)
                bias_scr[blk * w:(blk + 1) * w, :] = jnp.where(valid, -slope * distf, NEG_INF)
                sink_scr[blk * w:(blk + 1) * w, :] = jnp.full((w, LANES), sink_ref[hq], F32)

    kk = jnp.concatenate([kp_ref[...], kc_ref[...]], axis=0)
    vv = jnp.concatenate([vp_ref[...], vc_ref[...]], axis=0)
    lane_head = lax.broadcasted_iota(jnp.int32, (w, 2 * LANES), 1) // SWA_HEAD_DIM
    parts = []
    for g in range(SWA_GROUP):
        qg = q_ref[:, g * 256:(g + 1) * 256]
        for h in range(SWA_KV_HEADS):
            parts.append(jnp.where(lane_head == h, qg, jnp.zeros_like(qg)))
    qs = jnp.concatenate(parts, axis=0)
    s = lax.dot_general(qs, kk, (((1,), (1,)), ((), ())), preferred_element_type=F32)
    s = s + bias_scr[...]
    sink = sink_scr[...]
    m = jnp.maximum(jnp.max(s, axis=1, keepdims=True), sink)
    p = jnp.exp(s - jnp.tile(m, (1, 2)))
    denom = jnp.sum(p, axis=1, keepdims=True) + jnp.exp(sink - m)
    o = jnp.dot(p.astype(BF16), vv, preferred_element_type=F32)
    o = o * jnp.tile(1.0 / denom, (1, 2))
    for g in range(SWA_GROUP):
        out_g = jnp.zeros((w, 2 * LANES), F32)
        for h in range(SWA_KV_HEADS):
            blk = g * SWA_KV_HEADS + h
            out_g = jnp.where(lane_head == h, o[blk * w:(blk + 1) * w, :], out_g)
        o_ref[:, g * 256:(g + 1) * 256] = out_g.astype(BF16)


def _swa(proj, sinks):
    s = proj.shape[0]
    w = WINDOW
    nb = s // w
    kcol = COL_SK // SWA_KV_WIDTH
    vcol = COL_SV // SWA_KV_WIDTH
    grid_spec = pltpu.PrefetchScalarGridSpec(
        num_scalar_prefetch=1,
        grid=(nb,),
        in_specs=[pl.BlockSpec((w, SWA_WIDTH), lambda n, sk: (n, COL_SQ // SWA_WIDTH)),
                  pl.BlockSpec((w, SWA_KV_WIDTH), lambda n, sk: (jnp.maximum(n - 1, 0), kcol)),
                  pl.BlockSpec((w, SWA_KV_WIDTH), lambda n, sk: (n, kcol)),
                  pl.BlockSpec((w, SWA_KV_WIDTH), lambda n, sk: (jnp.maximum(n - 1, 0), vcol)),
                  pl.BlockSpec((w, SWA_KV_WIDTH), lambda n, sk: (n, vcol))],
        out_specs=pl.BlockSpec((w, SWA_WIDTH), lambda n, sk: (n, 0)),
        scratch_shapes=[pltpu.VMEM((SWA_Q_HEADS * w, 2 * w), F32),
                        pltpu.VMEM((SWA_Q_HEADS * w, LANES), F32)],
    )
    return pl.pallas_call(
        _swa_kernel,
        out_shape=jax.ShapeDtypeStruct((s, SWA_WIDTH), BF16),
        grid_spec=grid_spec,
        compiler_params=pltpu.CompilerParams(dimension_semantics=("arbitrary",),
                                             vmem_limit_bytes=VMEM_LIMIT),
        name="swa",
    )(sinks, proj, proj, proj, proj, proj)


OUT_TM = 256


def _out_kernel(of_ref, gf_ref, os_ref, gs_ref, mf_ref, ms_ref, x_ref, gate_ref,
                wbf_ref, wbs_ref, wo_ref, lng_ref, lnb_ref, o_ref):
    af = (of_ref[...].astype(F32) * jax.nn.silu(gf_ref[...].astype(F32))).astype(BF16)
    yf = jnp.dot(af, wbf_ref[...], preferred_element_type=F32)
    a_s = (os_ref[...].astype(F32) * jax.nn.silu(gs_ref[...].astype(F32))).astype(BF16)
    ys = jnp.dot(a_s, wbs_ref[...], preferred_element_type=F32)
    merged = (jax.nn.sigmoid(mf_ref[...].astype(F32)) * yf
              + jax.nn.sigmoid(ms_ref[...].astype(F32)) * ys)
    sub = jnp.dot(merged.astype(BF16), wo_ref[...], preferred_element_type=F32)
    z = DEEPNORM_ALPHA * x_ref[...] + gate_ref[...] * sub
    mu = jnp.mean(z, axis=-1, keepdims=True)
    zc = z - mu
    var = jnp.mean(zc * zc, axis=-1, keepdims=True)
    o_ref[...] = zc * lax.rsqrt(var + LN_EPS) * lng_ref[...] + lnb_ref[...]


def _out(proj, o_fox, o_swa, x2, ada, wbf, wbs, wo, ln_g, ln_b):
    s = x2.shape[0]
    tm = OUT_TM
    const = lambda i: (0, 0)
    return pl.pallas_call(
        _out_kernel,
        out_shape=jax.ShapeDtypeStruct((s, D_MODEL), F32),
        grid=(s // tm,),
        in_specs=[pl.BlockSpec((tm, FOX_WIDTH), lambda i: (i, 0)),
                  pl.BlockSpec((tm, FOX_WIDTH), lambda i: (i, COL_GF // FOX_WIDTH)),
                  pl.BlockSpec((tm, SWA_WIDTH), lambda i: (i, 0)),
                  pl.BlockSpec((tm, SWA_WIDTH), lambda i: (i, COL_GS // SWA_WIDTH)),
                  pl.BlockSpec((tm, D_MODEL), lambda i: (i, COL_MF // D_MODEL)),
                  pl.BlockSpec((tm, D_MODEL), lambda i: (i, COL_MS // D_MODEL)),
                  pl.BlockSpec((tm, D_MODEL), lambda i: (i, 0)),
                  pl.BlockSpec((1, D_MODEL), lambda i: (0, 2)),
                  pl.BlockSpec((FOX_WIDTH, D_MODEL), const),
                  pl.BlockSpec((SWA_WIDTH, D_MODEL), const),
                  pl.BlockSpec((D_MODEL, D_MODEL), const),
                  pl.BlockSpec((1, D_MODEL), const),
                  pl.BlockSpec((1, D_MODEL), const)],
        out_specs=pl.BlockSpec((tm, D_MODEL), lambda i: (i, 0)),
        compiler_params=pltpu.CompilerParams(dimension_semantics=("arbitrary",),
                                             vmem_limit_bytes=VMEM_LIMIT),
        name="out",
    )(o_fox, proj, o_swa, proj, proj, proj, x2, ada, wbf, wbs, wo, ln_g, ln_b)


def kernel(x, c, w_ada, b_ada, w_in, b_f, attn_sinks, w_br_fox, w_br_swa, w_out, ln_g, ln_b):
    b, s, d = x.shape
    assert (b, s, d) == (1, SEQ, D_MODEL) and w_in.shape[0] == DEPTH
    x2 = x.reshape(s, d)

    wt = jnp.swapaxes(w_in, 1, 2)[0]
    w_main = _wprep(wt)
    bf_pad = jnp.pad(b_f[0], (0, LANES - FOX_HEADS)).reshape(1, LANES)
    col_scale = jnp.asarray(_proj_col_scale())
    wbf = w_br_fox[0].astype(BF16)
    wbs = _swa_rows(w_br_swa[0]).astype(BF16)
    wo = w_out[0].astype(BF16)

    ada = _ada(c, w_ada[0], b_ada[0].reshape(1, -1))
    proj, flog = _proj(x2, ada, w_main, wt, col_scale)
    f_cum = _cum(flog, bf_pad)
    o_fox = _fox(proj, f_cum.reshape(FOX_HEADS, 1, s))
    o_swa = _swa(proj, attn_sinks[0])
    out = _out(proj, o_fox, o_swa, x2, ada, wbf, wbs, wo,
               ln_g[0].reshape(1, d), ln_b[0].reshape(1, d))
    return out.reshape(b, s, d)
```

```python
import math

import numpy as np
import jax
import jax.numpy as jnp
from jax import lax
from jax.experimental import pallas as pl
from jax.experimental.pallas import tpu as pltpu

F32 = jnp.float32
BF16 = jnp.bfloat16

D_MODEL = 2048
SEQ = 8192
FOX_HEADS = 8
FOX_HEAD_DIM = 128
FOX_WIDTH = FOX_HEADS * FOX_HEAD_DIM
SWA_Q_HEADS = 16
SWA_KV_HEADS = 4
SWA_HEAD_DIM = 64
SWA_GROUP = SWA_Q_HEADS // SWA_KV_HEADS
SWA_WIDTH = SWA_Q_HEADS * SWA_HEAD_DIM
SWA_KV_WIDTH = SWA_KV_HEADS * SWA_HEAD_DIM
WINDOW = 128
LN_EPS = 1e-5
NEG_INF = -1e30
DEPTH = 1
DEEPNORM_ALPHA = (2.0 * DEPTH) ** 0.25
LOG2E = math.log2(math.e)

LANES = 128
VMEM_LIMIT = 56 * 1024 * 1024

COL_FQ = 0
COL_FK = 1024
COL_FV = 2048
COL_SQ = 3072
COL_GF = 4096
COL_GS = 5120
COL_MF = 6144
COL_MS = 8192
COL_SK = 10240
COL_SV = 10496
PROJ_WIDTH = 10752

_O_FQ, _O_FK, _O_FV, _O_FLOG = 0, 1024, 2048, 3072
_O_SQ, _O_SK, _O_SV = 3080, 4104, 4360
_O_GF, _O_GS, _O_MF, _O_MS = 4616, 5640, 6664, 8712


def _swa_cols(w):
    r = w.shape[0]
    w4 = w.reshape(r, SWA_KV_HEADS, SWA_GROUP, SWA_HEAD_DIM)
    return w4.transpose(0, 2, 1, 3).reshape(r, SWA_WIDTH)


def _swa_rows(w):
    c = w.shape[1]
    w4 = w.reshape(SWA_KV_HEADS, SWA_GROUP, SWA_HEAD_DIM, c)
    return w4.transpose(1, 0, 2, 3).reshape(SWA_WIDTH, c)


def _proj_col_scale():
    s = np.ones((1, PROJ_WIDTH), np.float32)
    s[0, COL_FQ:COL_FQ + 1024] = FOX_HEAD_DIM ** -0.5 * LOG2E
    s[0, COL_SQ:COL_SQ + 1024] = SWA_HEAD_DIM ** -0.5
    return s


WP_TN = 512
WP_TC = 256
WP_SHIFT = FOX_HEADS
_WP_ALIGNED, _WP_SHIFTED, _WP_PERM0, _WP_PERM1 = 0, 1, 2, 3


def _wprep_tables():
    nb = PROJ_WIDTH // WP_TN
    blk_a = np.zeros((nb,), np.int32)
    blk_b = np.zeros((nb,), np.int32)
    blk_h = np.zeros((nb,), np.int32)
    mode = np.zeros((nb,), np.int32)

    def fill(col_out, col_src, width, shifted):
        for t in range(width // WP_TN):
            ob = col_out // WP_TN + t
            start = col_src + t * WP_TN - (WP_SHIFT if shifted else 0)
            assert start % WP_TN == 0
            mode[ob] = _WP_SHIFTED if shifted else _WP_ALIGNED
            blk_a[ob] = start // WP_TN
            blk_h[ob] = (start + WP_TN) // WP_SHIFT
            blk_b[ob] = -1

    def fill_perm(col_out, col_src):
        start = col_src - WP_SHIFT
        assert start % WP_TN == 0
        for t, m in enumerate((_WP_PERM0, _WP_PERM1)):
            ob = col_out // WP_TN + t
            mode[ob] = m
            blk_a[ob] = start // WP_TN
            blk_b[ob] = start // WP_TN + 1
            blk_h[ob] = (start + 2 * WP_TN) // WP_SHIFT
    fill(COL_FQ, _O_FQ, 3072, False)
    fill_perm(COL_SQ, _O_SQ)
    fill(COL_GF, _O_GF, 1024, True)
    fill_perm(COL_GS, _O_GS)
    fill(COL_MF, _O_MF, 2048, True)
    fill(COL_MS, _O_MS, 2048, True)
    fill(COL_SK, _O_SK, 512, True)
    for ob in range(nb):
        if blk_b[ob] < 0:
            blk_b[ob] = blk_b[ob - 1] if ob else 0
    return blk_a, blk_b, blk_h, mode


def _wprep_kernel(a_tab, b_tab, h_tab, mode_ref, a_ref, b_ref, h_ref, o_ref):
    mode = mode_ref[pl.program_id(0)]

    def emit(rows_of_chunk):
        for c in range(D_MODEL // WP_TC):
            cols = slice(c * WP_TC, (c + 1) * WP_TC)
            o_ref[cols, :] = rows_of_chunk(cols).T.astype(BF16)

    @pl.when(mode == _WP_ALIGNED)
    def _():
        emit(lambda cols: a_ref[:, cols])

    @pl.when(mode == _WP_SHIFTED)
    def _():
        emit(lambda cols: jnp.concatenate([a_ref[WP_SHIFT:, cols], h_ref[:, cols]], axis=0))

    def perm_rows(cols, t):
        parts = []
        for g in (2 * t, 2 * t + 1):
            for h in range(SWA_KV_HEADS):
                r0 = WP_SHIFT + (h * SWA_GROUP + g) * SWA_HEAD_DIM
                r1 = r0 + SWA_HEAD_DIM
                if r1 <= WP_TN:
                    parts.append(a_ref[r0:r1, cols])
                elif r0 >= WP_TN:
                    if r1 <= 2 * WP_TN:
                        parts.append(b_ref[r0 - WP_TN:r1 - WP_TN, cols])
                    else:
                        parts.append(jnp.concatenate([b_ref[r0 - WP_TN:, cols], h_ref[:, cols]], axis=0))
                else:
                    parts.append(jnp.concatenate([a_ref[r0:, cols], b_ref[:r1 - WP_TN, cols]], axis=0))
        return jnp.concatenate(parts, axis=0)

    @pl.when(mode == _WP_PERM0)
    def _():
        emit(lambda cols: perm_rows(cols, 0))

    @pl.when(mode == _WP_PERM1)
    def _():
        emit(lambda cols: perm_rows(cols, 1))


def _wprep(wt):
    tabs = _wprep_tables()
    grid_spec = pltpu.PrefetchScalarGridSpec(
        num_scalar_prefetch=4,
        grid=(PROJ_WIDTH // WP_TN,),
        in_specs=[pl.BlockSpec((WP_TN, D_MODEL), lambda i, a, b, h, m: (a[i], 0)),
                  pl.BlockSpec((WP_TN, D_MODEL), lambda i, a, b, h, m: (b[i], 0)),
                  pl.BlockSpec((WP_SHIFT, D_MODEL), lambda i, a, b, h, m: (h[i], 0))],
        out_specs=pl.BlockSpec((D_MODEL, WP_TN), lambda i, a, b, h, m: (0, i)),
    )
    return pl.pallas_call(
        _wprep_kernel,
        out_shape=jax.ShapeDtypeStruct((D_MODEL, PROJ_WIDTH), BF16),
        grid_spec=grid_spec,
        compiler_params=pltpu.CompilerParams(dimension_semantics=("arbitrary",),
                                             vmem_limit_bytes=VMEM_LIMIT),
        name="wprep",
    )(*[jnp.asarray(t) for t in tabs], wt, wt, wt)


def _ada_kernel(c_ref, w_ref, b_ref, o_ref):
    c8 = jnp.broadcast_to(c_ref[...], (8, D_MODEL))
    r = jnp.dot(c8, w_ref[...], preferred_element_type=F32)
    o_ref[...] = r[0:1, :] + b_ref[...]


def _ada(c, w_ada, b_ada):
    tn = 512
    n = w_ada.shape[1]
    return pl.pallas_call(
        _ada_kernel,
        out_shape=jax.ShapeDtypeStruct((1, n), F32),
        grid=(n // tn,),
        in_specs=[pl.BlockSpec((1, D_MODEL), lambda j: (0, 0)),
                  pl.BlockSpec((D_MODEL, tn), lambda j: (0, j)),
                  pl.BlockSpec((1, tn), lambda j: (0, j))],
        out_specs=pl.BlockSpec((1, tn), lambda j: (0, j)),
        compiler_params=pltpu.CompilerParams(dimension_semantics=("arbitrary",),
                                             vmem_limit_bytes=VMEM_LIMIT),
        name="ada",
    )(c, w_ada, b_ada)


PROJ_TM = 1024
PROJ_TN = 1536
PROJ_RC = 256


def _proj_kernel(x_ref, shift_ref, scale_ref, w_ref, wf_ref, cs_ref, o_ref, flog_ref, h_scr):
    j = pl.program_id(1)

    @pl.when(j == 0)
    def _():
        mod = 1.0 + scale_ref[...]
        shift = shift_ref[...]

        def body(r, carry):
            rows = pl.ds(pl.multiple_of(r * PROJ_RC, PROJ_RC), PROJ_RC)
            x = x_ref[rows, :]
            mu = jnp.mean(x, axis=-1, keepdims=True)
            xc = x - mu
            var = jnp.mean(xc * xc, axis=-1, keepdims=True)
            h = xc * lax.rsqrt(var + LN_EPS) * mod + shift
            h_scr[rows, :] = h.astype(BF16)
            return carry

        lax.fori_loop(0, PROJ_TM // PROJ_RC, body, 0)
        wf = jnp.concatenate([wf_ref[...], jnp.zeros((LANES - FOX_HEADS, D_MODEL), F32)], axis=0)
        flog_ref[...] = lax.dot_general(h_scr[...], wf.astype(BF16), (((1,), (1,)), ((), ())),
                                        preferred_element_type=F32)

    acc = jnp.dot(h_scr[...], w_ref[...], preferred_element_type=F32)
    o_ref[...] = (acc * cs_ref[...]).astype(BF16)


def _proj(x2, ada, w_main, w_flog, col_scale):
    s = x2.shape[0]
    return pl.pallas_call(
        _proj_kernel,
        out_shape=(jax.ShapeDtypeStruct((s, PROJ_WIDTH), BF16),
                   jax.ShapeDtypeStruct((s, LANES), F32)),
        grid=(s // PROJ_TM, PROJ_WIDTH // PROJ_TN),
        in_specs=[pl.BlockSpec((PROJ_TM, D_MODEL), lambda i, j: (i, 0)),
                  pl.BlockSpec((1, D_MODEL), lambda i, j: (0, 0)),
                  pl.BlockSpec((1, D_MODEL), lambda i, j: (0, 1)),
                  pl.BlockSpec((D_MODEL, PROJ_TN), lambda i, j: (0, j)),
                  pl.BlockSpec((FOX_HEADS, D_MODEL), lambda i, j: (_O_FLOG // FOX_HEADS, 0)),
                  pl.BlockSpec((1, PROJ_TN), lambda i, j: (0, j))],
        out_specs=(pl.BlockSpec((PROJ_TM, PROJ_TN), lambda i, j: (i, j)),
                   pl.BlockSpec((PROJ_TM, LANES), lambda i, j: (i, 0))),
        scratch_shapes=[pltpu.VMEM((PROJ_TM, D_MODEL), BF16)],
        compiler_params=pltpu.CompilerParams(dimension_semantics=("arbitrary", "arbitrary"),
                                             vmem_limit_bytes=VMEM_LIMIT),
        name="proj",
    )(x2, ada, ada, w_main, w_flog, col_scale)


def _cum_kernel(flog_ref, bf_ref, f_ref):
    s = flog_ref.shape[0]
    lf = jax.nn.log_sigmoid(flog_ref[...] + bf_ref[...])
    acc = lf.T[0:FOX_HEADS, :]
    lane = lax.broadcasted_iota(jnp.int32, acc.shape, 1)
    sh = 1
    while sh < s:
        rolled = pltpu.roll(acc, sh, axis=1)
        acc = acc + jnp.where(lane >= sh, rolled, 0.0)
        sh *= 2
    f_ref[...] = acc


def _cum(flog, bf_pad):
    s = flog.shape[0]
    return pl.pallas_call(
        _cum_kernel,
        out_shape=jax.ShapeDtypeStruct((FOX_HEADS, s), F32),
        in_specs=[pl.BlockSpec((s, LANES), lambda: (0, 0)),
                  pl.BlockSpec((1, LANES), lambda: (0, 0))],
        out_specs=pl.BlockSpec((FOX_HEADS, s), lambda: (0, 0)),
        compiler_params=pltpu.CompilerParams(vmem_limit_bytes=VMEM_LIMIT),
        name="cum",
    )(flog, bf_pad)


FOX_TK = 512
FOX_TQ = 2 * FOX_TK
FOX_SKIP_LOG2 = 160.0
FOX_KN_CHUNK = 1024


def _fox_kernel(q_ref, k_ref, v_ref, f_ref, o_ref, sa_scr, sb_scr, m_scr, acc_scr, kn_scr):
    i = pl.program_id(1)
    tq, tk = FOX_TQ, FOX_TK
    s_len = k_ref.shape[0]

    @pl.when(i == 0)
    def _():
        def body(r, mx):
            kf = k_ref[pl.ds(pl.multiple_of(r * FOX_KN_CHUNK, FOX_KN_CHUNK), FOX_KN_CHUNK), :].astype(F32)
            n2 = jnp.sum(kf * kf, axis=1, keepdims=True)
            return jnp.maximum(mx, jnp.max(n2, axis=0, keepdims=True))
        kn2 = lax.fori_loop(0, s_len // FOX_KN_CHUNK, body, jnp.zeros((1, 1), F32))
        kn_scr[...] = jnp.broadcast_to(kn2, kn_scr.shape)

    q0 = pl.multiple_of(i * tq, tq)
    f_q = f_ref[0, :, pl.ds(q0, tq)]
    f_base = jnp.max(f_q, axis=1, keepdims=True)
    ones = jnp.ones((tk, LANES), BF16)
    top, bot = slice(0, tk), slice(tk, tq)

    m_scr[...] = jnp.full(m_scr.shape, NEG_INF, F32)
    acc_scr[...] = jnp.zeros(acc_scr.shape, F32)

    def scores(k0, dst, rows=slice(0, FOX_TQ)):
        k_t = k_ref[pl.ds(k0, tk), :]
        s = lax.dot_general(q_ref[rows, :], k_t, (((1,), (1,)), ((), ())), preferred_element_type=F32)
        bias = (f_base - f_ref[0, :, pl.ds(k0, tk)]) * LOG2E
        dst[rows, :] = s + bias

    def softmax_pv(src, k0, rows=slice(0, FOX_TQ), causal=False):
        s = src[rows, :]
        if causal:
            n = rows.stop - rows.start
            keep = (lax.broadcasted_iota(jnp.int32, (n, tk), 1)
                    <= lax.broadcasted_iota(jnp.int32, (n, tk), 0))
            s = jnp.where(keep, s, NEG_INF)
        m_prev = m_scr[rows, :]
        m_new = jnp.maximum(m_prev, jnp.max(s, axis=1, keepdims=True))
        alpha = jnp.exp2(m_prev - m_new)
        p = jnp.exp2(s - jnp.tile(m_new, (1, tk // LANES))).astype(BF16)
        v_aug = jnp.concatenate([v_ref[pl.ds(k0, tk), :], ones], axis=1)
        pv = jnp.dot(p, v_aug, preferred_element_type=F32)
        acc_scr[rows, :] = acc_scr[rows, :] * jnp.tile(alpha, (1, 2)) + pv
        m_scr[rows, :] = m_new

    kd = pl.multiple_of(q0 + tk, tk)
    scores(q0, sa_scr)
    scores(kd, sb_scr, bot)
    softmax_pv(sa_scr, q0, top, causal=True)
    softmax_pv(sa_scr, q0, bot)
    softmax_pv(sb_scr, kd, bot, causal=True)
    scores(pl.multiple_of(jnp.maximum(q0 - tq, 0), tq), sa_scr)

    qf = q_ref[...].astype(F32)
    qn2 = jnp.max(jnp.sum(qf * qf, axis=1, keepdims=True), axis=0, keepdims=True)
    m_low = jnp.min(jnp.min(m_scr[...], axis=0, keepdims=True), axis=1, keepdims=True)
    thr = m_low - FOX_SKIP_LOG2 - jnp.sqrt(qn2 * kn_scr[0:1, 0:1])
    pos = lax.broadcasted_iota(jnp.int32, (1, s_len), 1)
    live = ((f_base - f_ref[0]) * LOG2E >= thr) & (pos < q0)
    n_live = jnp.sum(live.astype(jnp.int32), axis=1, keepdims=True)
    n_pairs = (n_live[0, 0] + (tq - 1)) // tq

    def body(t, carry):
        ka = pl.multiple_of(q0 - (t + 1) * tq, tq)
        kb = pl.multiple_of(ka + tk, tk)
        scores(kb, sb_scr)
        softmax_pv(sa_scr, ka)
        scores(pl.multiple_of(jnp.maximum(ka - tq, 0), tq), sa_scr)
        softmax_pv(sb_scr, kb)
        return carry

    lax.fori_loop(0, n_pairs, body, 0)

    acc = acc_scr[...]
    o_ref[...] = (acc[:, :FOX_HEAD_DIM] / acc[:, FOX_HEAD_DIM:]).astype(BF16)


def _fox(proj, f3):
    s = proj.shape[0]
    tq, tk = FOX_TQ, FOX_TK
    return pl.pallas_call(
        _fox_kernel,
        out_shape=jax.ShapeDtypeStruct((s, FOX_WIDTH), BF16),
        grid=(FOX_HEADS, s // tq),
        in_specs=[pl.BlockSpec((tq, LANES), lambda h, i: (i, COL_FQ // LANES + h)),
                  pl.BlockSpec((s, LANES), lambda h, i: (0, COL_FK // LANES + h)),
                  pl.BlockSpec((s, LANES), lambda h, i: (0, COL_FV // LANES + h)),
                  pl.BlockSpec((1, 1, s), lambda h, i: (h, 0, 0))],
        out_specs=pl.BlockSpec((tq, LANES), lambda h, i: (i, h)),
        scratch_shapes=[pltpu.VMEM((tq, tk), F32), pltpu.VMEM((tq, tk), F32),
                        pltpu.VMEM((tq, LANES), F32), pltpu.VMEM((tq, 2 * LANES), F32),
                        pltpu.VMEM((8, LANES), F32)],
        compiler_params=pltpu.CompilerParams(dimension_semantics=("arbitrary", "arbitrary"),
                                             vmem_limit_bytes=VMEM_LIMIT),
        name="fox",
    )(proj, proj, proj, f3)


def _swa_kernel(sink_ref, q_ref, kp_ref, kc_ref, vp_ref, vc_ref, o_ref, bias_scr, sink_scr):
    n = pl.program_id(0)
    w = WINDOW

    @pl.when(n <= 1)
    def _():
        row = lax.broadcasted_iota(jnp.int32, (w, 2 * w), 0)
        col = lax.broadcasted_iota(jnp.int32, (w, 2 * w), 1)
        dist = row - col + w
        valid = (dist >= 0) & (dist < w) & ((col >= w) | (n > 0))
        distf = dist.astype(F32)
        for g in range(SWA_GROUP):
            for h in range(SWA_KV_HEADS):
                blk = g * SWA_KV_HEADS + h
                hq = h * SWA_GROUP + g
                slope = 2.0 ** (-8.0 * (hq + 1.0) / SWA_Q_HEADS)
                bias_scr[blk * w:(blk + 1) * w, :] = jnp.where(valid, -slope * distf, NEG_INF)
                sink_scr[blk * w:(blk + 1) * w, :] = jnp.full((w, LANES), sink_ref[hq], F32)

    kk = jnp.concatenate([kp_ref[...], kc_ref[...]], axis=0)
    vv = jnp.concatenate([vp_ref[...], vc_ref[...]], axis=0)
    lane_head = lax.broadcasted_iota(jnp.int32, (w, 2 * LANES), 1) // SWA_HEAD_DIM
    parts = []
    for g in range(SWA_GROUP):
        qg = q_ref[:, g * 256:(g + 1) * 256]
        for h in range(SWA_KV_HEADS):
            parts.append(jnp.where(lane_head == h, qg, jnp.zeros_like(qg)))
    qs = jnp.concatenate(parts, axis=0)
    s = lax.dot_general(qs, kk, (((1,), (1,)), ((), ())), preferred_element_type=F32)
    s = s + bias_scr[...]
    sink = sink_scr[...]
    m = jnp.maximum(jnp.max(s, axis=1, keepdims=True), sink)
    p = jnp.exp(s - jnp.tile(m, (1, 2)))
    denom = jnp.sum(p, axis=1, keepdims=True) + jnp.exp(sink - m)
    o = jnp.dot(p.astype(BF16), vv, preferred_element_type=F32)
    o = o * jnp.tile(1.0 / denom, (1, 2))
    for g in range(SWA_GROUP):
        out_g = jnp.zeros((w, 2 * LANES), F32)
        for h in range(SWA_KV_HEADS):
            blk = g * SWA_KV_HEADS + h
            out_g = jnp.where(lane_head == h, o[blk * w:(blk + 1) * w, :], out_g)
        o_ref[:, g * 256:(g + 1) * 256] = out_g.astype(BF16)


def _swa(proj, sinks):
    s = proj.shape[0]
    w = WINDOW
    nb = s // w
    kcol = COL_SK // SWA_KV_WIDTH
    vcol = COL_SV // SWA_KV_WIDTH
    grid_spec = pltpu.PrefetchScalarGridSpec(
        num_scalar_prefetch=1,
        grid=(nb,),
        in_specs=[pl.BlockSpec((w, SWA_WIDTH), lambda n, sk: (n, COL_SQ // SWA_WIDTH)),
                  pl.BlockSpec((w, SWA_KV_WIDTH), lambda n, sk: (jnp.maximum(n - 1, 0), kcol)),
                  pl.BlockSpec((w, SWA_KV_WIDTH), lambda n, sk: (n, kcol)),
                  pl.BlockSpec((w, SWA_KV_WIDTH), lambda n, sk: (jnp.maximum(n - 1, 0), vcol)),
                  pl.BlockSpec((w, SWA_KV_WIDTH), lambda n, sk: (n, vcol))],
        out_specs=pl.BlockSpec((w, SWA_WIDTH), lambda n, sk: (n, 0)),
        scratch_shapes=[pltpu.VMEM((SWA_Q_HEADS * w, 2 * w), F32),
                        pltpu.VMEM((SWA_Q_HEADS * w, LANES), F32)],
    )
    return pl.pallas_call(
        _swa_kernel,
        out_shape=jax.ShapeDtypeStruct((s, SWA_WIDTH), BF16),
        grid_spec=grid_spec,
        compiler_params=pltpu.CompilerParams(dimension_semantics=("arbitrary",),
                                             vmem_limit_bytes=VMEM_LIMIT),
        name="swa",
    )(sinks, proj, proj, proj, proj, proj)


OUT_TM = 256


def _out_kernel(of_ref, gf_ref, os_ref, gs_ref, mf_ref, ms_ref, x_ref, gate_ref,
                wbf_ref, wbs_ref, wo_ref, lng_ref, lnb_ref, o_ref):
    af = (of_ref[...].astype(F32) * jax.nn.silu(gf_ref[...].astype(F32))).astype(BF16)
    yf = jnp.dot(af, wbf_ref[...], preferred_element_type=F32)
    a_s = (os_ref[...].astype(F32) * jax.nn.silu(gs_ref[...].astype(F32))).astype(BF16)
    ys = jnp.dot(a_s, wbs_ref[...], preferred_element_type=F32)
    merged = (jax.nn.sigmoid(mf_ref[...].astype(F32)) * yf
              + jax.nn.sigmoid(ms_ref[...].astype(F32)) * ys)
    sub = jnp.dot(merged.astype(BF16), wo_ref[...], preferred_element_type=F32)
    z = DEEPNORM_ALPHA * x_ref[...] + gate_ref[...] * sub
    mu = jnp.mean(z, axis=-1, keepdims=True)
    zc = z - mu
    var = jnp.mean(zc * zc, axis=-1, keepdims=True)
    o_ref[...] = zc * lax.rsqrt(var + LN_EPS) * lng_ref[...] + lnb_ref[...]


def _out(proj, o_fox, o_swa, x2, ada, wbf, wbs, wo, ln_g, ln_b):
    s = x2.shape[0]
    tm = OUT_TM
    const = lambda i: (0, 0)
    return pl.pallas_call(
        _out_kernel,
        out_shape=jax.ShapeDtypeStruct((s, D_MODEL), F32),
        grid=(s // tm,),
        in_specs=[pl.BlockSpec((tm, FOX_WIDTH), lambda i: (i, 0)),
                  pl.BlockSpec((tm, FOX_WIDTH), lambda i: (i, COL_GF // FOX_WIDTH)),
                  pl.BlockSpec((tm, SWA_WIDTH), lambda i: (i, 0)),
                  pl.BlockSpec((tm, SWA_WIDTH), lambda i: (i, COL_GS // SWA_WIDTH)),
                  pl.BlockSpec((tm, D_MODEL), lambda i: (i, COL_MF // D_MODEL)),
                  pl.BlockSpec((tm, D_MODEL), lambda i: (i, COL_MS // D_MODEL)),
                  pl.BlockSpec((tm, D_MODEL), lambda i: (i, 0)),
                  pl.BlockSpec((1, D_MODEL), lambda i: (0, 2)),
                  pl.BlockSpec((FOX_WIDTH, D_MODEL), const),
                  pl.BlockSpec((SWA_WIDTH, D_MODEL), const),
                  pl.BlockSpec((D_MODEL, D_MODEL), const),
                  pl.BlockSpec((1, D_MODEL), const),
                  pl.BlockSpec((1, D_MODEL), const)],
        out_specs=pl.BlockSpec((tm, D_MODEL), lambda i: (i, 0)),
        compiler_params=pltpu.CompilerParams(dimension_semantics=("arbitrary",),
                                             vmem_limit_bytes=VMEM_LIMIT),
        name="out",
    )(o_fox, proj, o_swa, proj, proj, proj, x2, ada, wbf, wbs, wo, ln_g, ln_b)


def kernel(x, c, w_ada, b_ada, w_in, b_f, attn_sinks, w_br_fox, w_br_swa, w_out, ln_g, ln_b):
    b, s, d = x.shape
    assert (b, s, d) == (1, SEQ, D_MODEL) and w_in.shape[0] == DEPTH
    x2 = x.reshape(s, d)

    wt = jnp.swapaxes(w_in, 1, 2)[0]
    w_main = _wprep(wt)
    bf_pad = jnp.pad(b_f[0], (0, LANES - FOX_HEADS)).reshape(1, LANES)
    col_scale = jnp.asarray(_proj_col_scale())
    wbf = w_br_fox[0].astype(BF16)
    wbs = _swa_rows(w_br_swa[0]).astype(BF16)
    wo = w_out[0].astype(BF16)

    ada = _ada(c, w_ada[0], b_ada[0].reshape(1, -1))
    proj, flog = _proj(x2, ada, w_main, wt, col_scale)
    f_cum = _cum(flog, bf_pad)
    o_fox = _fox(proj, f_cum.reshape(FOX_HEADS, 1, s))
    o_swa = _swa(proj, attn_sinks[0])
    out = _out(proj, o_fox, o_swa, x2, ada, wbf, wbs, wo,
               ln_g[0].reshape(1, d), ln_b[0].reshape(1, d))
    return out.reshape(b, s, d)
```

```python
import math

import numpy as np
import jax
import jax.numpy as jnp
from jax import lax
from jax.experimental import pallas as pl
from jax.experimental.pallas import tpu as pltpu

F32 = jnp.float32
BF16 = jnp.bfloat16

D_MODEL = 2048
SEQ = 8192
FOX_HEADS = 8
FOX_HEAD_DIM = 128
FOX_WIDTH = FOX_HEADS * FOX_HEAD_DIM
SWA_Q_HEADS = 16
SWA_KV_HEADS = 4
SWA_HEAD_DIM = 64
SWA_GROUP = SWA_Q_HEADS // SWA_KV_HEADS
SWA_WIDTH = SWA_Q_HEADS * SWA_HEAD_DIM
SWA_KV_WIDTH = SWA_KV_HEADS * SWA_HEAD_DIM
WINDOW = 128
LN_EPS = 1e-5
NEG_INF = -1e30
DEPTH = 1
DEEPNORM_ALPHA = (2.0 * DEPTH) ** 0.25
LOG2E = math.log2(math.e)

LANES = 128
VMEM_LIMIT = 56 * 1024 * 1024

COL_FQ = 0
COL_FK = 1024
COL_FV = 2048
COL_SQ = 3072
COL_GF = 4096
COL_GS = 5120
COL_MF = 6144
COL_MS = 8192
COL_SK = 10240
COL_SV = 10496
PROJ_WIDTH = 10752

_O_FQ, _O_FK, _O_FV, _O_FLOG = 0, 1024, 2048, 3072
_O_SQ, _O_SK, _O_SV = 3080, 4104, 4360
_O_GF, _O_GS, _O_MF, _O_MS = 4616, 5640, 6664, 8712


def _swa_cols(w):
    r = w.shape[0]
    w4 = w.reshape(r, SWA_KV_HEADS, SWA_GROUP, SWA_HEAD_DIM)
    return w4.transpose(0, 2, 1, 3).reshape(r, SWA_WIDTH)


def _swa_rows(w):
    c = w.shape[1]
    w4 = w.reshape(SWA_KV_HEADS, SWA_GROUP, SWA_HEAD_DIM, c)
    return w4.transpose(1, 0, 2, 3).reshape(SWA_WIDTH, c)


def _proj_col_scale():
    s = np.ones((1, PROJ_WIDTH), np.float32)
    s[0, COL_FQ:COL_FQ + 1024] = FOX_HEAD_DIM ** -0.5 * LOG2E
    s[0, COL_SQ:COL_SQ + 1024] = SWA_HEAD_DIM ** -0.5 * LOG2E
    return s


WP_TN = 512
WP_TC = 256
WP_SHIFT = FOX_HEADS
_WP_ALIGNED, _WP_SHIFTED, _WP_PERM0, _WP_PERM1 = 0, 1, 2, 3


def _wprep_tables():
    nb = PROJ_WIDTH // WP_TN
    blk_a = np.zeros((nb,), np.int32)
    blk_b = np.zeros((nb,), np.int32)
    blk_h = np.zeros((nb,), np.int32)
    mode = np.zeros((nb,), np.int32)

    def fill(col_out, col_src, width, shifted):
        for t in range(width // WP_TN):
            ob = col_out // WP_TN + t
            start = col_src + t * WP_TN - (WP_SHIFT if shifted else 0)
            assert start % WP_TN == 0
            mode[ob] = _WP_SHIFTED if shifted else _WP_ALIGNED
            blk_a[ob] = start // WP_TN
            blk_h[ob] = (start + WP_TN) // WP_SHIFT
            blk_b[ob] = -1

    def fill_perm(col_out, col_src):
        start = col_src - WP_SHIFT
        assert start % WP_TN == 0
        for t, m in enumerate((_WP_PERM0, _WP_PERM1)):
            ob = col_out // WP_TN + t
            mode[ob] = m
            blk_a[ob] = start // WP_TN
            blk_b[ob] = start // WP_TN + 1
            blk_h[ob] = (start + 2 * WP_TN) // WP_SHIFT
    fill(COL_FQ, _O_FQ, 3072, False)
    fill_perm(COL_SQ, _O_SQ)
    fill(COL_GF, _O_GF, 1024, True)
    fill_perm(COL_GS, _O_GS)
    fill(COL_MF, _O_MF, 2048, True)
    fill(COL_MS, _O_MS, 2048, True)
    fill(COL_SK, _O_SK, 512, True)
    for ob in range(nb):
        if blk_b[ob] < 0:
            blk_b[ob] = blk_b[ob - 1] if ob else 0
    return blk_a, blk_b, blk_h, mode


def _wprep_kernel(a_tab, b_tab, h_tab, mode_ref, a_ref, b_ref, h_ref, o_ref):
    mode = mode_ref[pl.program_id(0)]

    def emit(rows_of_chunk):
        for c in range(D_MODEL // WP_TC):
            cols = slice(c * WP_TC, (c + 1) * WP_TC)
            o_ref[cols, :] = rows_of_chunk(cols).T.astype(BF16)

    @pl.when(mode == _WP_ALIGNED)
    def _():
        emit(lambda cols: a_ref[:, cols])

    @pl.when(mode == _WP_SHIFTED)
    def _():
        emit(lambda cols: jnp.concatenate([a_ref[WP_SHIFT:, cols], h_ref[:, cols]], axis=0))

    def perm_rows(cols, t):
        parts = []
        for g in (2 * t, 2 * t + 1):
            for h in range(SWA_KV_HEADS):
                r0 = WP_SHIFT + (h * SWA_GROUP + g) * SWA_HEAD_DIM
                r1 = r0 + SWA_HEAD_DIM
                if r1 <= WP_TN:
                    parts.append(a_ref[r0:r1, cols])
                elif r0 >= WP_TN:
                    if r1 <= 2 * WP_TN:
                        parts.append(b_ref[r0 - WP_TN:r1 - WP_TN, cols])
                    else:
                        parts.append(jnp.concatenate([b_ref[r0 - WP_TN:, cols], h_ref[:, cols]], axis=0))
                else:
                    parts.append(jnp.concatenate([a_ref[r0:, cols], b_ref[:r1 - WP_TN, cols]], axis=0))
        return jnp.concatenate(parts, axis=0)

    @pl.when(mode == _WP_PERM0)
    def _():
        emit(lambda cols: perm_rows(cols, 0))

    @pl.when(mode == _WP_PERM1)
    def _():
        emit(lambda cols: perm_rows(cols, 1))


def _wprep(wt):
    tabs = _wprep_tables()
    grid_spec = pltpu.PrefetchScalarGridSpec(
        num_scalar_prefetch=4,
        grid=(PROJ_WIDTH // WP_TN,),
        in_specs=[pl.BlockSpec((WP_TN, D_MODEL), lambda i, a, b, h, m: (a[i], 0)),
                  pl.BlockSpec((WP_TN, D_MODEL), lambda i, a, b, h, m: (b[i], 0)),
                  pl.BlockSpec((WP_SHIFT, D_MODEL), lambda i, a, b, h, m: (h[i], 0))],
        out_specs=pl.BlockSpec((D_MODEL, WP_TN), lambda i, a, b, h, m: (0, i)),
    )
    return pl.pallas_call(
        _wprep_kernel,
        out_shape=jax.ShapeDtypeStruct((D_MODEL, PROJ_WIDTH), BF16),
        grid_spec=grid_spec,
        compiler_params=pltpu.CompilerParams(dimension_semantics=("arbitrary",),
                                             vmem_limit_bytes=VMEM_LIMIT),
        name="wprep",
    )(*[jnp.asarray(t) for t in tabs], wt, wt, wt)


def _ada_kernel(c_ref, w_ref, b_ref, o_ref):
    c8 = jnp.broadcast_to(c_ref[...], (8, D_MODEL))
    r = jnp.dot(c8, w_ref[...], preferred_element_type=F32)
    o_ref[...] = r[0:1, :] + b_ref[...]


def _ada(c, w_ada, b_ada):
    tn = 512
    n = w_ada.shape[1]
    return pl.pallas_call(
        _ada_kernel,
        out_shape=jax.ShapeDtypeStruct((1, n), F32),
        grid=(n // tn,),
        in_specs=[pl.BlockSpec((1, D_MODEL), lambda j: (0, 0)),
                  pl.BlockSpec((D_MODEL, tn), lambda j: (0, j)),
                  pl.BlockSpec((1, tn), lambda j: (0, j))],
        out_specs=pl.BlockSpec((1, tn), lambda j: (0, j)),
        compiler_params=pltpu.CompilerParams(dimension_semantics=("arbitrary",),
                                             vmem_limit_bytes=VMEM_LIMIT),
        name="ada",
    )(c, w_ada, b_ada)


PROJ_TM = 1024
PROJ_TN = 1536
PROJ_NJ = PROJ_WIDTH // PROJ_TN
PROJ_LN_ROWS = 160
assert PROJ_NJ * PROJ_LN_ROWS >= PROJ_TM and PROJ_LN_ROWS % 16 == 0 and (PROJ_TM - PROJ_LN_ROWS) % 16 == 0


def _proj_kernel(x_ref, shift_ref, scale_ref, w_ref, wf_ref, cs_ref, o_ref, flog_ref, h0_scr, h1_scr):
    r = pl.program_id(0)
    j = pl.program_id(1)
    nb = pl.num_programs(0) - 1

    def layer_norm_slice(h_dst):
        start = pl.multiple_of(jnp.minimum(j * PROJ_LN_ROWS, PROJ_TM - PROJ_LN_ROWS), 16)
        rows = pl.ds(start, PROJ_LN_ROWS)
        x = x_ref[rows, :]
        mu = jnp.mean(x, axis=-1, keepdims=True)
        xc = x - mu
        var = jnp.mean(xc * xc, axis=-1, keepdims=True)
        h = xc * lax.rsqrt(var + LN_EPS) * (1.0 + scale_ref[...]) + shift_ref[...]
        h_dst[rows, :] = h.astype(BF16)

    def project(h_src):
        acc = jnp.dot(h_src[...], w_ref[...], preferred_element_type=F32)
        o_ref[...] = (acc * cs_ref[...]).astype(BF16)

        @pl.when(j == 0)
        def _():
            wf = jnp.concatenate([wf_ref[...], jnp.zeros((LANES - FOX_HEADS, D_MODEL), F32)], axis=0)
            flog_ref[...] = lax.dot_general(h_src[...], wf.astype(BF16), (((1,), (1,)), ((), ())),
                                            preferred_element_type=F32)

    @pl.when(r == 0)
    def _():
        layer_norm_slice(h0_scr)

    @pl.when((r > 0) & (r < nb) & (r % 2 == 1))
    def _():
        layer_norm_slice(h1_scr)
        project(h0_scr)

    @pl.when((r > 0) & (r < nb) & (r % 2 == 0))
    def _():
        layer_norm_slice(h0_scr)
        project(h1_scr)

    @pl.when(r == nb)
    def _():
        project(h1_scr if (SEQ // PROJ_TM) % 2 == 0 else h0_scr)


def _proj(x2, ada, w_main, wt, col_scale):
    s = x2.shape[0]
    nb = s // PROJ_TM
    prev = lambda r: jnp.maximum(r - 1, 0)
    col = lambda r, j: jnp.where(r == 0, 0, j)
    return pl.pallas_call(
        _proj_kernel,
        out_shape=(jax.ShapeDtypeStruct((s, PROJ_WIDTH), BF16),
                   jax.ShapeDtypeStruct((s, LANES), F32)),
        grid=(nb + 1, PROJ_NJ),
        in_specs=[pl.BlockSpec((PROJ_TM, D_MODEL), lambda r, j: (jnp.minimum(r, nb - 1), 0)),
                  pl.BlockSpec((1, D_MODEL), lambda r, j: (0, 0)),
                  pl.BlockSpec((1, D_MODEL), lambda r, j: (0, 1)),
                  pl.BlockSpec((D_MODEL, PROJ_TN), lambda r, j: (0, col(r, j))),
                  pl.BlockSpec((FOX_HEADS, D_MODEL), lambda r, j: (_O_FLOG // FOX_HEADS, 0)),
                  pl.BlockSpec((1, PROJ_TN), lambda r, j: (0, col(r, j)))],
        out_specs=(pl.BlockSpec((PROJ_TM, PROJ_TN), lambda r, j: (prev(r), col(r, j))),
                   pl.BlockSpec((PROJ_TM, LANES), lambda r, j: (prev(r), 0))),
        scratch_shapes=[pltpu.VMEM((PROJ_TM, D_MODEL), BF16), pltpu.VMEM((PROJ_TM, D_MODEL), BF16)],
        compiler_params=pltpu.CompilerParams(dimension_semantics=("arbitrary", "arbitrary"),
                                             vmem_limit_bytes=VMEM_LIMIT),
        name="proj",
    )(x2, ada, ada, w_main, wt, col_scale)


def _cum_kernel(flog_ref, bf_ref, f_ref):
    s = flog_ref.shape[0]
    lf = jax.nn.log_sigmoid(flog_ref[...] + bf_ref[...])
    acc = lf.T[0:FOX_HEADS, :]
    lane = lax.broadcasted_iota(jnp.int32, acc.shape, 1)
    sh = 1
    while sh < s:
        rolled = pltpu.roll(acc, sh, axis=1)
        acc = acc + jnp.where(lane >= sh, rolled, 0.0)
        sh *= 2
    f_ref[...] = acc


def _cum(flog, bf_pad):
    s = flog.shape[0]
    return pl.pallas_call(
        _cum_kernel,
        out_shape=jax.ShapeDtypeStruct((FOX_HEADS, s), F32),
        in_specs=[pl.BlockSpec((s, LANES), lambda: (0, 0)),
                  pl.BlockSpec((1, LANES), lambda: (0, 0))],
        out_specs=pl.BlockSpec((FOX_HEADS, s), lambda: (0, 0)),
        compiler_params=pltpu.CompilerParams(vmem_limit_bytes=VMEM_LIMIT),
        name="cum",
    )(flog, bf_pad)


FOX_TK = 512
FOX_TQ = 2 * FOX_TK
FOX_SKIP_LOG2 = 160.0
FOX_KN_CHUNK = 1024


def _fox_kernel(q_ref, k_ref, v_ref, f_ref, o_ref, sa_scr, sb_scr, m_scr, acc_scr, kn_scr):
    i = pl.program_id(1)
    tq, tk = FOX_TQ, FOX_TK
    s_len = k_ref.shape[0]

    @pl.when(i == 0)
    def _():
        def body(r, mx):
            kf = k_ref[pl.ds(pl.multiple_of(r * FOX_KN_CHUNK, FOX_KN_CHUNK), FOX_KN_CHUNK), :].astype(F32)
            n2 = jnp.sum(kf * kf, axis=1, keepdims=True)
            return jnp.maximum(mx, jnp.max(n2, axis=0, keepdims=True))
        kn2 = lax.fori_loop(0, s_len // FOX_KN_CHUNK, body, jnp.zeros((1, 1), F32))
        kn_scr[...] = jnp.broadcast_to(kn2, kn_scr.shape)

    q0 = pl.multiple_of(i * tq, tq)
    f_q = f_ref[0, :, pl.ds(q0, tq)]
    f_base = jnp.max(f_q, axis=1, keepdims=True)
    ones = jnp.ones((tk, LANES), BF16)
    top, bot = slice(0, tk), slice(tk, tq)

    m_scr[...] = jnp.full(m_scr.shape, NEG_INF, F32)
    acc_scr[...] = jnp.zeros(acc_scr.shape, F32)

    def scores(k0, dst, rows=slice(0, FOX_TQ)):
        k_t = k_ref[pl.ds(k0, tk), :]
        s = lax.dot_general(q_ref[rows, :], k_t, (((1,), (1,)), ((), ())), preferred_element_type=F32)
        bias = (f_base - f_ref[0, :, pl.ds(k0, tk)]) * LOG2E
        dst[rows, :] = s + bias

    def softmax_pv(src, k0, rows=slice(0, FOX_TQ), causal=False):
        s = src[rows, :]
        if causal:
            n = rows.stop - rows.start
            keep = (lax.broadcasted_iota(jnp.int32, (n, tk), 1)
                    <= lax.broadcasted_iota(jnp.int32, (n, tk), 0))
            s = jnp.where(keep, s, NEG_INF)
        m_prev = m_scr[rows, :]
        m_new = jnp.maximum(m_prev, jnp.max(s, axis=1, keepdims=True))
        alpha = jnp.exp2(m_prev - m_new)
        p = jnp.exp2(s - jnp.tile(m_new, (1, tk // LANES))).astype(BF16)
        v_aug = jnp.concatenate([v_ref[pl.ds(k0, tk), :], ones], axis=1)
        pv = jnp.dot(p, v_aug, preferred_element_type=F32)
        acc_scr[rows, :] = acc_scr[rows, :] * jnp.tile(alpha, (1, 2)) + pv
        m_scr[rows, :] = m_new

    kd = pl.multiple_of(q0 + tk, tk)
    scores(q0, sa_scr)
    scores(kd, sb_scr, bot)
    softmax_pv(sa_scr, q0, top, causal=True)
    softmax_pv(sa_scr, q0, bot)
    softmax_pv(sb_scr, kd, bot, causal=True)
    scores(pl.multiple_of(jnp.maximum(q0 - tq, 0), tq), sa_scr)

    qf = q_ref[...].astype(F32)
    qn2 = jnp.max(jnp.sum(qf * qf, axis=1, keepdims=True), axis=0, keepdims=True)
    m_low = jnp.min(jnp.min(m_scr[...], axis=0, keepdims=True), axis=1, keepdims=True)
    thr = m_low - FOX_SKIP_LOG2 - jnp.sqrt(qn2 * kn_scr[0:1, 0:1])
    pos = lax.broadcasted_iota(jnp.int32, (1, s_len), 1)
    live = ((f_base - f_ref[0]) * LOG2E >= thr) & (pos < q0)
    n_live = jnp.sum(live.astype(jnp.int32), axis=1, keepdims=True)
    n_pairs = (n_live[0, 0] + (tq - 1)) // tq

    def body(t, carry):
        ka = pl.multiple_of(q0 - (t + 1) * tq, tq)
        kb = pl.multiple_of(ka + tk, tk)
        scores(kb, sb_scr)
        softmax_pv(sa_scr, ka)
        scores(pl.multiple_of(jnp.maximum(ka - tq, 0), tq), sa_scr)
        softmax_pv(sb_scr, kb)
        return carry

    lax.fori_loop(0, n_pairs, body, 0)

    acc = acc_scr[...]
    o_ref[...] = (acc[:, :FOX_HEAD_DIM] / acc[:, FOX_HEAD_DIM:]).astype(BF16)


def _fox(proj, f3):
    s = proj.shape[0]
    tq, tk = FOX_TQ, FOX_TK
    return pl.pallas_call(
        _fox_kernel,
        out_shape=jax.ShapeDtypeStruct((s, FOX_WIDTH), BF16),
        grid=(FOX_HEADS, s // tq),
        in_specs=[pl.BlockSpec((tq, LANES), lambda h, i: (i, COL_FQ // LANES + h)),
                  pl.BlockSpec((s, LANES), lambda h, i: (0, COL_FK // LANES + h)),
                  pl.BlockSpec((s, LANES), lambda h, i: (0, COL_FV // LANES + h)),
                  pl.BlockSpec((1, 1, s), lambda h, i: (h, 0, 0))],
        out_specs=pl.BlockSpec((tq, LANES), lambda h, i: (i, h)),
        scratch_shapes=[pltpu.VMEM((tq, tk), F32), pltpu.VMEM((tq, tk), F32),
                        pltpu.VMEM((tq, LANES), F32), pltpu.VMEM((tq, 2 * LANES), F32),
                        pltpu.VMEM((8, LANES), F32)],
        compiler_params=pltpu.CompilerParams(dimension_semantics=("arbitrary", "arbitrary"),
                                             vmem_limit_bytes=VMEM_LIMIT),
        name="fox",
    )(proj, proj, proj, f3)


def _swa_kernel(sink_ref, q_ref, kp_ref, kc_ref, vp_ref, vc_ref, o_ref, bias_scr, sink_scr):
    n = pl.program_id(0)
    w = WINDOW
    gw = SWA_GROUP * w

    @pl.when(n <= 1)
    def _():
        row = lax.broadcasted_iota(jnp.int32, (w, 2 * w), 0)
        col = lax.broadcasted_iota(jnp.int32, (w, 2 * w), 1)
        dist = row - col + w
        valid = (dist >= 0) & (dist < w) & ((col >= w) | (n > 0))
        distf = dist.astype(F32)
        for hq in range(SWA_Q_HEADS):
            slope = 2.0 ** (-8.0 * (hq + 1.0) / SWA_Q_HEADS)
            bias_scr[hq * w:(hq + 1) * w, :] = jnp.where(valid, (-slope * LOG2E) * distf, NEG_INF)
            sink_scr[hq * w:(hq + 1) * w, :] = jnp.full((w, LANES), sink_ref[hq] * LOG2E, F32)

    kk = jnp.concatenate([kp_ref[...], kc_ref[...]], axis=0)
    vv = jnp.concatenate([vp_ref[...], vc_ref[...]], axis=0)
    lane_head = lax.broadcasted_iota(jnp.int32, (w, 2 * LANES), 1) // SWA_HEAD_DIM
    key_head = lax.broadcasted_iota(jnp.int32, (2 * w, 2 * LANES), 1) // SWA_HEAD_DIM
    parts = []
    for h in range(SWA_KV_HEADS):
        for g in range(SWA_GROUP):
            qg = q_ref[:, g * 256:(g + 1) * 256]
            parts.append(jnp.where(lane_head == h, qg, jnp.zeros_like(qg)))
    qs = jnp.concatenate(parts, axis=0)
    s = lax.dot_general(qs, kk, (((1,), (1,)), ((), ())), preferred_element_type=F32)
    s = s + bias_scr[...]
    sink = sink_scr[...]
    m = jnp.maximum(jnp.max(s, axis=1, keepdims=True), sink)
    p = jnp.exp2(s - jnp.tile(m, (1, 2)))
    inv = 1.0 / (jnp.sum(p, axis=1, keepdims=True) + jnp.exp2(sink - m))
    pb = p.astype(BF16)
    p_all = jnp.concatenate([pb[h * gw:(h + 1) * gw, :] for h in range(SWA_KV_HEADS)], axis=1)
    v_blk = jnp.concatenate([jnp.where(key_head == h, vv, jnp.zeros_like(vv))
                             for h in range(SWA_KV_HEADS)], axis=0)
    o = jnp.dot(p_all, v_blk, preferred_element_type=F32)
    for g in range(SWA_GROUP):
        inv_g = jnp.tile(inv[g * w:(g + 1) * w, :], (1, 2))
        for h in range(1, SWA_KV_HEADS):
            r0 = h * gw + g * w
            inv_g = jnp.where(lane_head == h, jnp.tile(inv[r0:r0 + w, :], (1, 2)), inv_g)
        o_ref[:, g * 256:(g + 1) * 256] = (o[g * w:(g + 1) * w, :] * inv_g).astype(BF16)


def _swa(proj, sinks):
    s = proj.shape[0]
    w = WINDOW
    nb = s // w
    kcol = COL_SK // SWA_KV_WIDTH
    vcol = COL_SV // SWA_KV_WIDTH
    grid_spec = pltpu.PrefetchScalarGridSpec(
        num_scalar_prefetch=1,
        grid=(nb,),
        in_specs=[pl.BlockSpec((w, SWA_WIDTH), lambda n, sk: (n, COL_SQ // SWA_WIDTH)),
                  pl.BlockSpec((w, SWA_KV_WIDTH), lambda n, sk: (jnp.maximum(n - 1, 0), kcol)),
                  pl.BlockSpec((w, SWA_KV_WIDTH), lambda n, sk: (n, kcol)),
                  pl.BlockSpec((w, SWA_KV_WIDTH), lambda n, sk: (jnp.maximum(n - 1, 0), vcol)),
                  pl.BlockSpec((w, SWA_KV_WIDTH), lambda n, sk: (n, vcol))],
        out_specs=pl.BlockSpec((w, SWA_WIDTH), lambda n, sk: (n, 0)),
        scratch_shapes=[pltpu.VMEM((SWA_Q_HEADS * w, 2 * w), F32),
                        pltpu.VMEM((SWA_Q_HEADS * w, LANES), F32)],
    )
    return pl.pallas_call(
        _swa_kernel,
        out_shape=jax.ShapeDtypeStruct((s, SWA_WIDTH), BF16),
        grid_spec=grid_spec,
        compiler_params=pltpu.CompilerParams(dimension_semantics=("arbitrary",),
                                             vmem_limit_bytes=VMEM_LIMIT),
        name="swa",
    )(sinks, proj, proj, proj, proj, proj)


OUT_TM = 256


def _out_kernel(of_ref, gf_ref, os_ref, gs_ref, mf_ref, ms_ref, x_ref, gate_ref,
                wbf_ref, wbs_ref, wo_ref, lng_ref, lnb_ref, o_ref):
    af = (of_ref[...].astype(F32) * jax.nn.silu(gf_ref[...].astype(F32))).astype(BF16)
    yf = jnp.dot(af, wbf_ref[...], preferred_element_type=F32)
    a_s = (os_ref[...].astype(F32) * jax.nn.silu(gs_ref[...].astype(F32))).astype(BF16)
    ys = jnp.dot(a_s, wbs_ref[...], preferred_element_type=F32)
    merged = (jax.nn.sigmoid(mf_ref[...].astype(F32)) * yf
              + jax.nn.sigmoid(ms_ref[...].astype(F32)) * ys)
    sub = jnp.dot(merged.astype(BF16), wo_ref[...], preferred_element_type=F32)
    z = DEEPNORM_ALPHA * x_ref[...] + gate_ref[...] * sub
    mu = jnp.mean(z, axis=-1, keepdims=True)
    zc = z - mu
    var = jnp.mean(zc * zc, axis=-1, keepdims=True)
    o_ref[...] = zc * lax.rsqrt(var + LN_EPS) * lng_ref[...] + lnb_ref[...]


def _out(proj, o_fox, o_swa, x2, ada, wbf, wbs, wo, ln_g, ln_b):
    s = x2.shape[0]
    tm = OUT_TM
    const = lambda i: (0, 0)
    return pl.pallas_call(
        _out_kernel,
        out_shape=jax.ShapeDtypeStruct((s, D_MODEL), F32),
        grid=(s // tm,),
        in_specs=[pl.BlockSpec((tm, FOX_WIDTH), lambda i: (i, 0)),
                  pl.BlockSpec((tm, FOX_WIDTH), lambda i: (i, COL_GF // FOX_WIDTH)),
                  pl.BlockSpec((tm, SWA_WIDTH), lambda i: (i, 0)),
                  pl.BlockSpec((tm, SWA_WIDTH), lambda i: (i, COL_GS // SWA_WIDTH)),
                  pl.BlockSpec((tm, D_MODEL), lambda i: (i, COL_MF // D_MODEL)),
                  pl.BlockSpec((tm, D_MODEL), lambda i: (i, COL_MS // D_MODEL)),
                  pl.BlockSpec((tm, D_MODEL), lambda i: (i, 0)),
                  pl.BlockSpec((1, D_MODEL), lambda i: (0, 2)),
                  pl.BlockSpec((FOX_WIDTH, D_MODEL), const),
                  pl.BlockSpec((SWA_WIDTH, D_MODEL), const),
                  pl.BlockSpec((D_MODEL, D_MODEL), const),
                  pl.BlockSpec((1, D_MODEL), const),
                  pl.BlockSpec((1, D_MODEL), const)],
        out_specs=pl.BlockSpec((tm, D_MODEL), lambda i: (i, 0)),
        compiler_params=pltpu.CompilerParams(dimension_semantics=("arbitrary",),
                                             vmem_limit_bytes=VMEM_LIMIT),
        name="out",
    )(o_fox, proj, o_swa, proj, proj, proj, x2, ada, wbf, wbs, wo, ln_g, ln_b)


def kernel(x, c, w_ada, b_ada, w_in, b_f, attn_sinks, w_br_fox, w_br_swa, w_out, ln_g, ln_b):
    b, s, d = x.shape
    assert (b, s, d) == (1, SEQ, D_MODEL) and w_in.shape[0] == DEPTH
    x2 = x.reshape(s, d)

    wt = jnp.swapaxes(w_in, 1, 2)[0]
    w_main = _wprep(wt)
    bf_pad = jnp.pad(b_f[0], (0, LANES - FOX_HEADS)).reshape(1, LANES)
    col_scale = jnp.asarray(_proj_col_scale())
    wbf = w_br_fox[0].astype(BF16)
    wbs = _swa_rows(w_br_swa[0]).astype(BF16)
    wo = w_out[0].astype(BF16)

    ada = _ada(c, w_ada[0], b_ada[0].reshape(1, -1))
    proj, flog = _proj(x2, ada, w_main, wt, col_scale)
    f_cum = _cum(flog, bf_pad)
    o_fox = _fox(proj, f_cum.reshape(FOX_HEADS, 1, s))
    o_swa = _swa(proj, attn_sinks[0])
    out = _out(proj, o_fox, o_swa, x2, ada, wbf, wbs, wo,
               ln_g[0].reshape(1, d), ln_b[0].reshape(1, d))
    return out.reshape(b, s, d)
```

```python
import math

import numpy as np
import jax
import jax.numpy as jnp
from jax import lax
from jax.experimental import pallas as pl
from jax.experimental.pallas import tpu as pltpu

F32 = jnp.float32
BF16 = jnp.bfloat16

D_MODEL = 2048
SEQ = 8192
FOX_HEADS = 8
FOX_HEAD_DIM = 128
FOX_WIDTH = FOX_HEADS * FOX_HEAD_DIM
SWA_Q_HEADS = 16
SWA_KV_HEADS = 4
SWA_HEAD_DIM = 64
SWA_GROUP = SWA_Q_HEADS // SWA_KV_HEADS
SWA_WIDTH = SWA_Q_HEADS * SWA_HEAD_DIM
SWA_KV_WIDTH = SWA_KV_HEADS * SWA_HEAD_DIM
WINDOW = 128
LN_EPS = 1e-5
NEG_INF = -1e30
DEPTH = 1
DEEPNORM_ALPHA = (2.0 * DEPTH) ** 0.25
LOG2E = math.log2(math.e)

LANES = 128
VMEM_LIMIT = 56 * 1024 * 1024

COL_FQ = 0
COL_FK = 1024
COL_FV = 2048
COL_SQ = 3072
COL_GF = 4096
COL_GS = 5120
COL_MF = 6144
COL_MS = 8192
COL_SK = 10240
COL_SV = 10496
PROJ_WIDTH = 10752

_O_FQ, _O_FK, _O_FV, _O_FLOG = 0, 1024, 2048, 3072
_O_SQ, _O_SK, _O_SV = 3080, 4104, 4360
_O_GF, _O_GS, _O_MF, _O_MS = 4616, 5640, 6664, 8712


def _swa_cols(w):
    r = w.shape[0]
    w4 = w.reshape(r, SWA_KV_HEADS, SWA_GROUP, SWA_HEAD_DIM)
    return w4.transpose(0, 2, 1, 3).reshape(r, SWA_WIDTH)


def _swa_rows(w):
    c = w.shape[1]
    w4 = w.reshape(SWA_KV_HEADS, SWA_GROUP, SWA_HEAD_DIM, c)
    return w4.transpose(1, 0, 2, 3).reshape(SWA_WIDTH, c)


def _proj_col_scale():
    s = np.ones((1, PROJ_WIDTH), np.float32)
    s[0, COL_FQ:COL_FQ + 1024] = FOX_HEAD_DIM ** -0.5 * LOG2E
    s[0, COL_SQ:COL_SQ + 1024] = SWA_HEAD_DIM ** -0.5 * LOG2E
    return s


WP_TN = 512
WP_TC = 256
WP_SHIFT = FOX_HEADS
_WP_ALIGNED, _WP_SHIFTED, _WP_PERM0, _WP_PERM1 = 0, 1, 2, 3


def _wprep_tables():
    nb = PROJ_WIDTH // WP_TN
    blk_a = np.zeros((nb,), np.int32)
    blk_b = np.zeros((nb,), np.int32)
    blk_h = np.zeros((nb,), np.int32)
    mode = np.zeros((nb,), np.int32)

    def fill(col_out, col_src, width, shifted):
        for t in range(width // WP_TN):
            ob = col_out // WP_TN + t
            start = col_src + t * WP_TN - (WP_SHIFT if shifted else 0)
            assert start % WP_TN == 0
            mode[ob] = _WP_SHIFTED if shifted else _WP_ALIGNED
            blk_a[ob] = start // WP_TN
            blk_h[ob] = (start + WP_TN) // WP_SHIFT
            blk_b[ob] = -1

    def fill_perm(col_out, col_src):
        start = col_src - WP_SHIFT
        assert start % WP_TN == 0
        for t, m in enumerate((_WP_PERM0, _WP_PERM1)):
            ob = col_out // WP_TN + t
            mode[ob] = m
            blk_a[ob] = start // WP_TN
            blk_b[ob] = start // WP_TN + 1
            blk_h[ob] = (start + 2 * WP_TN) // WP_SHIFT
    fill(COL_FQ, _O_FQ, 3072, False)
    fill_perm(COL_SQ, _O_SQ)
    fill(COL_GF, _O_GF, 1024, True)
    fill_perm(COL_GS, _O_GS)
    fill(COL_MF, _O_MF, 2048, True)
    fill(COL_MS, _O_MS, 2048, True)
    fill(COL_SK, _O_SK, 512, True)
    for ob in range(nb):
        if blk_b[ob] < 0:
            blk_b[ob] = blk_b[ob - 1] if ob else 0
    return blk_a, blk_b, blk_h, mode


def _wprep_kernel(a_tab, b_tab, h_tab, mode_ref, a_ref, b_ref, h_ref, o_ref):
    mode = mode_ref[pl.program_id(0)]

    def emit(rows_of_chunk):
        for c in range(D_MODEL // WP_TC):
            cols = slice(c * WP_TC, (c + 1) * WP_TC)
            o_ref[cols, :] = rows_of_chunk(cols).T.astype(BF16)

    @pl.when(mode == _WP_ALIGNED)
    def _():
        emit(lambda cols: a_ref[:, cols])

    @pl.when(mode == _WP_SHIFTED)
    def _():
        emit(lambda cols: jnp.concatenate([a_ref[WP_SHIFT:, cols], h_ref[:, cols]], axis=0))

    def perm_rows(cols, t):
        parts = []
        for g in (2 * t, 2 * t + 1):
            for h in range(SWA_KV_HEADS):
                r0 = WP_SHIFT + (h * SWA_GROUP + g) * SWA_HEAD_DIM
                r1 = r0 + SWA_HEAD_DIM
                if r1 <= WP_TN:
                    parts.append(a_ref[r0:r1, cols])
                elif r0 >= WP_TN:
                    if r1 <= 2 * WP_TN:
                        parts.append(b_ref[r0 - WP_TN:r1 - WP_TN, cols])
                    else:
                        parts.append(jnp.concatenate([b_ref[r0 - WP_TN:, cols], h_ref[:, cols]], axis=0))
                else:
                    parts.append(jnp.concatenate([a_ref[r0:, cols], b_ref[:r1 - WP_TN, cols]], axis=0))
        return jnp.concatenate(parts, axis=0)

    @pl.when(mode == _WP_PERM0)
    def _():
        emit(lambda cols: perm_rows(cols, 0))

    @pl.when(mode == _WP_PERM1)
    def _():
        emit(lambda cols: perm_rows(cols, 1))


def _wprep(wt):
    tabs = _wprep_tables()
    grid_spec = pltpu.PrefetchScalarGridSpec(
        num_scalar_prefetch=4,
        grid=(PROJ_WIDTH // WP_TN,),
        in_specs=[pl.BlockSpec((WP_TN, D_MODEL), lambda i, a, b, h, m: (a[i], 0)),
                  pl.BlockSpec((WP_TN, D_MODEL), lambda i, a, b, h, m: (b[i], 0)),
                  pl.BlockSpec((WP_SHIFT, D_MODEL), lambda i, a, b, h, m: (h[i], 0))],
        out_specs=pl.BlockSpec((D_MODEL, WP_TN), lambda i, a, b, h, m: (0, i)),
    )
    return pl.pallas_call(
        _wprep_kernel,
        out_shape=jax.ShapeDtypeStruct((D_MODEL, PROJ_WIDTH), BF16),
        grid_spec=grid_spec,
        compiler_params=pltpu.CompilerParams(dimension_semantics=("arbitrary",),
                                             vmem_limit_bytes=VMEM_LIMIT),
        name="wprep",
    )(*[jnp.asarray(t) for t in tabs], wt, wt, wt)


def _ada_kernel(c_ref, w_ref, b_ref, o_ref):
    c8 = jnp.broadcast_to(c_ref[...], (8, D_MODEL))
    r = jnp.dot(c8, w_ref[...], preferred_element_type=F32)
    o_ref[...] = r[0:1, :] + b_ref[...]


def _ada(c, w_ada, b_ada):
    tn = 512
    n = w_ada.shape[1]
    return pl.pallas_call(
        _ada_kernel,
        out_shape=jax.ShapeDtypeStruct((1, n), F32),
        grid=(n // tn,),
        in_specs=[pl.BlockSpec((1, D_MODEL), lambda j: (0, 0)),
                  pl.BlockSpec((D_MODEL, tn), lambda j: (0, j)),
                  pl.BlockSpec((1, tn), lambda j: (0, j))],
        out_specs=pl.BlockSpec((1, tn), lambda j: (0, j)),
        compiler_params=pltpu.CompilerParams(dimension_semantics=("arbitrary",),
                                             vmem_limit_bytes=VMEM_LIMIT),
        name="ada",
    )(c, w_ada, b_ada)


PROJ_TM = 1024
PROJ_TN = 1536
PROJ_NJ = PROJ_WIDTH // PROJ_TN
PROJ_LN_ROWS = 160
assert PROJ_NJ * PROJ_LN_ROWS >= PROJ_TM and PROJ_LN_ROWS % 16 == 0 and (PROJ_TM - PROJ_LN_ROWS) % 16 == 0


def _proj_kernel(x_ref, shift_ref, scale_ref, w_ref, wf_ref, cs_ref, o_ref, flog_ref, h0_scr, h1_scr):
    r = pl.program_id(0)
    j = pl.program_id(1)
    nb = pl.num_programs(0) - 1

    def layer_norm_slice(h_dst):
        start = pl.multiple_of(jnp.minimum(j * PROJ_LN_ROWS, PROJ_TM - PROJ_LN_ROWS), 16)
        rows = pl.ds(start, PROJ_LN_ROWS)
        x = x_ref[rows, :]
        mu = jnp.mean(x, axis=-1, keepdims=True)
        xc = x - mu
        var = jnp.mean(xc * xc, axis=-1, keepdims=True)
        h = xc * lax.rsqrt(var + LN_EPS) * (1.0 + scale_ref[...]) + shift_ref[...]
        h_dst[rows, :] = h.astype(BF16)

    def project(h_src):
        acc = jnp.dot(h_src[...], w_ref[...], preferred_element_type=F32)
        o_ref[...] = (acc * cs_ref[...]).astype(BF16)

        @pl.when(j == 0)
        def _():
            wf = jnp.concatenate([wf_ref[...], jnp.zeros((LANES - FOX_HEADS, D_MODEL), F32)], axis=0)
            flog_ref[...] = lax.dot_general(h_src[...], wf.astype(BF16), (((1,), (1,)), ((), ())),
                                            preferred_element_type=F32)

    @pl.when(r == 0)
    def _():
        layer_norm_slice(h0_scr)

    @pl.when((r > 0) & (r < nb) & (r % 2 == 1))
    def _():
        layer_norm_slice(h1_scr)
        project(h0_scr)

    @pl.when((r > 0) & (r < nb) & (r % 2 == 0))
    def _():
        layer_norm_slice(h0_scr)
        project(h1_scr)

    @pl.when(r == nb)
    def _():
        project(h1_scr if (SEQ // PROJ_TM) % 2 == 0 else h0_scr)


def _proj(x2, ada, w_main, wt, col_scale):
    s = x2.shape[0]
    nb = s // PROJ_TM
    prev = lambda r: jnp.maximum(r - 1, 0)
    col = lambda r, j: jnp.where(r == 0, 0, j)
    return pl.pallas_call(
        _proj_kernel,
        out_shape=(jax.ShapeDtypeStruct((s, PROJ_WIDTH), BF16),
                   jax.ShapeDtypeStruct((s, LANES), F32)),
        grid=(nb + 1, PROJ_NJ),
        in_specs=[pl.BlockSpec((PROJ_TM, D_MODEL), lambda r, j: (jnp.minimum(r, nb - 1), 0)),
                  pl.BlockSpec((1, D_MODEL), lambda r, j: (0, 0)),
                  pl.BlockSpec((1, D_MODEL), lambda r, j: (0, 1)),
                  pl.BlockSpec((D_MODEL, PROJ_TN), lambda r, j: (0, col(r, j))),
                  pl.BlockSpec((FOX_HEADS, D_MODEL), lambda r, j: (_O_FLOG // FOX_HEADS, 0)),
                  pl.BlockSpec((1, PROJ_TN), lambda r, j: (0, col(r, j)))],
        out_specs=(pl.BlockSpec((PROJ_TM, PROJ_TN), lambda r, j: (prev(r), col(r, j))),
                   pl.BlockSpec((PROJ_TM, LANES), lambda r, j: (prev(r), 0))),
        scratch_shapes=[pltpu.VMEM((PROJ_TM, D_MODEL), BF16), pltpu.VMEM((PROJ_TM, D_MODEL), BF16)],
        compiler_params=pltpu.CompilerParams(dimension_semantics=("arbitrary", "arbitrary"),
                                             vmem_limit_bytes=VMEM_LIMIT),
        name="proj",
    )(x2, ada, ada, w_main, wt, col_scale)


def _cum_kernel(flog_ref, bf_ref, f_ref):
    s = flog_ref.shape[0]
    lf = jax.nn.log_sigmoid(flog_ref[...] + bf_ref[...])
    acc = lf.T[0:FOX_HEADS, :]
    lane = lax.broadcasted_iota(jnp.int32, acc.shape, 1)
    sh = 1
    while sh < s:
        rolled = pltpu.roll(acc, sh, axis=1)
        acc = acc + jnp.where(lane >= sh, rolled, 0.0)
        sh *= 2
    f_ref[...] = acc


def _cum(flog, bf_pad):
    s = flog.shape[0]
    return pl.pallas_call(
        _cum_kernel,
        out_shape=jax.ShapeDtypeStruct((FOX_HEADS, s), F32),
        in_specs=[pl.BlockSpec((s, LANES), lambda: (0, 0)),
                  pl.BlockSpec((1, LANES), lambda: (0, 0))],
        out_specs=pl.BlockSpec((FOX_HEADS, s), lambda: (0, 0)),
        compiler_params=pltpu.CompilerParams(vmem_limit_bytes=VMEM_LIMIT),
        name="cum",
    )(flog, bf_pad)


FOX_TK = 512
FOX_TQ = 2 * FOX_TK
FOX_SKIP_LOG2 = 160.0
FOX_KN_CHUNK = 1024


def _attn_kernel(sink_ref, q_ref, k_ref, v_ref, f_ref, sq_ref, skp_ref, skc_ref, svp_ref, svc_ref,
                 o_ref, os_ref, sa_scr, sb_scr, m_scr, acc_scr, kn_scr, bias_scr, sink_scr):
    i = pl.program_id(1)
    n_blk = pl.program_id(0) * pl.num_programs(1) + i
    tq, tk = FOX_TQ, FOX_TK
    s_len = k_ref.shape[0]

    @pl.when(n_blk <= 1)
    def _():
        _swa_tables(n_blk, sink_ref, bias_scr, sink_scr)

    @pl.when(i == 0)
    def _():
        def body(r, mx):
            kf = k_ref[pl.ds(pl.multiple_of(r * FOX_KN_CHUNK, FOX_KN_CHUNK), FOX_KN_CHUNK), :].astype(F32)
            n2 = jnp.sum(kf * kf, axis=1, keepdims=True)
            return jnp.maximum(mx, jnp.max(n2, axis=0, keepdims=True))
        kn2 = lax.fori_loop(0, s_len // FOX_KN_CHUNK, body, jnp.zeros((1, 1), F32))
        kn_scr[...] = jnp.broadcast_to(kn2, kn_scr.shape)

    q0 = pl.multiple_of(i * tq, tq)
    f_q = f_ref[0, :, pl.ds(q0, tq)]
    f_base = jnp.max(f_q, axis=1, keepdims=True)
    ones = jnp.ones((tk, LANES), BF16)
    top, bot = slice(0, tk), slice(tk, tq)

    m_scr[...] = jnp.full(m_scr.shape, NEG_INF, F32)
    acc_scr[...] = jnp.zeros(acc_scr.shape, F32)

    def scores(k0, dst, rows=slice(0, FOX_TQ)):
        k_t = k_ref[pl.ds(k0, tk), :]
        s = lax.dot_general(q_ref[rows, :], k_t, (((1,), (1,)), ((), ())), preferred_element_type=F32)
        bias = (f_base - f_ref[0, :, pl.ds(k0, tk)]) * LOG2E
        dst[rows, :] = s + bias

    def softmax_pv(src, k0, rows=slice(0, FOX_TQ), causal=False):
        s = src[rows, :]
        if causal:
            n = rows.stop - rows.start
            keep = (lax.broadcasted_iota(jnp.int32, (n, tk), 1)
                    <= lax.broadcasted_iota(jnp.int32, (n, tk), 0))
            s = jnp.where(keep, s, NEG_INF)
        m_prev = m_scr[rows, :]
        m_new = jnp.maximum(m_prev, jnp.max(s, axis=1, keepdims=True))
        alpha = jnp.exp2(m_prev - m_new)
        p = jnp.exp2(s - jnp.tile(m_new, (1, tk // LANES))).astype(BF16)
        v_aug = jnp.concatenate([v_ref[pl.ds(k0, tk), :], ones], axis=1)
        pv = jnp.dot(p, v_aug, preferred_element_type=F32)
        acc_scr[rows, :] = acc_scr[rows, :] * jnp.tile(alpha, (1, 2)) + pv
        m_scr[rows, :] = m_new

    kd = pl.multiple_of(q0 + tk, tk)
    scores(q0, sa_scr)
    scores(kd, sb_scr, bot)
    softmax_pv(sa_scr, q0, top, causal=True)
    softmax_pv(sa_scr, q0, bot)
    softmax_pv(sb_scr, kd, bot, causal=True)
    scores(pl.multiple_of(jnp.maximum(q0 - tq, 0), tq), sa_scr)

    _swa_block(sq_ref, skp_ref, skc_ref, svp_ref, svc_ref, os_ref, bias_scr, sink_scr)

    qf = q_ref[...].astype(F32)
    qn2 = jnp.max(jnp.sum(qf * qf, axis=1, keepdims=True), axis=0, keepdims=True)
    m_low = jnp.min(jnp.min(m_scr[...], axis=0, keepdims=True), axis=1, keepdims=True)
    thr = m_low - FOX_SKIP_LOG2 - jnp.sqrt(qn2 * kn_scr[0:1, 0:1])
    pos = lax.broadcasted_iota(jnp.int32, (1, s_len), 1)
    live = ((f_base - f_ref[0]) * LOG2E >= thr) & (pos < q0)
    n_live = jnp.sum(live.astype(jnp.int32), axis=1, keepdims=True)
    n_pairs = (n_live[0, 0] + (tq - 1)) // tq

    def body(t, carry):
        ka = pl.multiple_of(q0 - (t + 1) * tq, tq)
        kb = pl.multiple_of(ka + tk, tk)
        scores(kb, sb_scr)
        softmax_pv(sa_scr, ka)
        scores(pl.multiple_of(jnp.maximum(ka - tq, 0), tq), sa_scr)
        softmax_pv(sb_scr, kb)
        return carry

    lax.fori_loop(0, n_pairs, body, 0)

    acc = acc_scr[...]
    o_ref[...] = (acc[:, :FOX_HEAD_DIM] / acc[:, FOX_HEAD_DIM:]).astype(BF16)


def _attn(proj, f3, sinks):
    s = proj.shape[0]
    tq, tk, w = FOX_TQ, FOX_TK, WINDOW
    nq = s // tq
    assert FOX_HEADS * nq == s // w
    kcol = COL_SK // SWA_KV_WIDTH
    vcol = COL_SV // SWA_KV_WIDTH
    blk = lambda h, i: h * nq + i
    prv = lambda h, i: jnp.maximum(h * nq + i - 1, 0)
    grid_spec = pltpu.PrefetchScalarGridSpec(
        num_scalar_prefetch=1,
        grid=(FOX_HEADS, nq),
        in_specs=[pl.BlockSpec((tq, LANES), lambda h, i, sk: (i, COL_FQ // LANES + h)),
                  pl.BlockSpec((s, LANES), lambda h, i, sk: (0, COL_FK // LANES + h)),
                  pl.BlockSpec((s, LANES), lambda h, i, sk: (0, COL_FV // LANES + h)),
                  pl.BlockSpec((1, 1, s), lambda h, i, sk: (h, 0, 0)),
                  pl.BlockSpec((w, SWA_WIDTH), lambda h, i, sk: (blk(h, i), COL_SQ // SWA_WIDTH)),
                  pl.BlockSpec((w, SWA_KV_WIDTH), lambda h, i, sk: (prv(h, i), kcol)),
                  pl.BlockSpec((w, SWA_KV_WIDTH), lambda h, i, sk: (blk(h, i), kcol)),
                  pl.BlockSpec((w, SWA_KV_WIDTH), lambda h, i, sk: (prv(h, i), vcol)),
                  pl.BlockSpec((w, SWA_KV_WIDTH), lambda h, i, sk: (blk(h, i), vcol))],
        out_specs=(pl.BlockSpec((tq, LANES), lambda h, i, sk: (i, h)),
                   pl.BlockSpec((w, SWA_WIDTH), lambda h, i, sk: (blk(h, i), 0))),
        scratch_shapes=[pltpu.VMEM((tq, tk), F32), pltpu.VMEM((tq, tk), F32),
                        pltpu.VMEM((tq, LANES), F32), pltpu.VMEM((tq, 2 * LANES), F32),
                        pltpu.VMEM((8, LANES), F32),
                        pltpu.VMEM((SWA_Q_HEADS * w, 2 * w), F32),
                        pltpu.VMEM((SWA_Q_HEADS * w, LANES), F32)],
    )
    return pl.pallas_call(
        _attn_kernel,
        out_shape=(jax.ShapeDtypeStruct((s, FOX_WIDTH), BF16),
                   jax.ShapeDtypeStruct((s, SWA_WIDTH), BF16)),
        grid_spec=grid_spec,
        compiler_params=pltpu.CompilerParams(dimension_semantics=("arbitrary", "arbitrary"),
                                             vmem_limit_bytes=VMEM_LIMIT),
        name="attn",
    )(sinks, proj, proj, proj, f3, proj, proj, proj, proj, proj)


def _swa_tables(n, sink_ref, bias_scr, sink_scr):
    w = WINDOW
    row = lax.broadcasted_iota(jnp.int32, (w, 2 * w), 0)
    col = lax.broadcasted_iota(jnp.int32, (w, 2 * w), 1)
    dist = row - col + w
    valid = (dist >= 0) & (dist < w) & ((col >= w) | (n > 0))
    distf = dist.astype(F32)
    for hq in range(SWA_Q_HEADS):
        slope = 2.0 ** (-8.0 * (hq + 1.0) / SWA_Q_HEADS)
        bias_scr[hq * w:(hq + 1) * w, :] = jnp.where(valid, (-slope * LOG2E) * distf, NEG_INF)
        sink_scr[hq * w:(hq + 1) * w, :] = jnp.full((w, LANES), sink_ref[hq] * LOG2E, F32)


def _swa_block(q_ref, kp_ref, kc_ref, vp_ref, vc_ref, o_ref, bias_scr, sink_scr):
    w = WINDOW
    gw = SWA_GROUP * w
    kk = jnp.concatenate([kp_ref[...], kc_ref[...]], axis=0)
    vv = jnp.concatenate([vp_ref[...], vc_ref[...]], axis=0)
    lane_head = lax.broadcasted_iota(jnp.int32, (w, 2 * LANES), 1) // SWA_HEAD_DIM
    key_head = lax.broadcasted_iota(jnp.int32, (2 * w, 2 * LANES), 1) // SWA_HEAD_DIM
    parts = []
    for h in range(SWA_KV_HEADS):
        for g in range(SWA_GROUP):
            qg = q_ref[:, g * 256:(g + 1) * 256]
            parts.append(jnp.where(lane_head == h, qg, jnp.zeros_like(qg)))
    qs = jnp.concatenate(parts, axis=0)
    s = lax.dot_general(qs, kk, (((1,), (1,)), ((), ())), preferred_element_type=F32)
    s = s + bias_scr[...]
    sink = sink_scr[...]
    m = jnp.maximum(jnp.max(s, axis=1, keepdims=True), sink)
    p = jnp.exp2(s - jnp.tile(m, (1, 2)))
    inv = 1.0 / (jnp.sum(p, axis=1, keepdims=True) + jnp.exp2(sink - m))
    pb = p.astype(BF16)
    p_all = jnp.concatenate([pb[h * gw:(h + 1) * gw, :] for h in range(SWA_KV_HEADS)], axis=1)
    v_blk = jnp.concatenate([jnp.where(key_head == h, vv, jnp.zeros_like(vv))
                             for h in range(SWA_KV_HEADS)], axis=0)
    o = jnp.dot(p_all, v_blk, preferred_element_type=F32)
    for g in range(SWA_GROUP):
        inv_g = jnp.tile(inv[g * w:(g + 1) * w, :], (1, 2))
        for h in range(1, SWA_KV_HEADS):
            r0 = h * gw + g * w
            inv_g = jnp.where(lane_head == h, jnp.tile(inv[r0:r0 + w, :], (1, 2)), inv_g)
        o_ref[:, g * 256:(g + 1) * 256] = (o[g * w:(g + 1) * w, :] * inv_g).astype(BF16)


OUT_TM = 256


def _out_kernel(of_ref, gf_ref, os_ref, gs_ref, mf_ref, ms_ref, x_ref, gate_ref,
                wbf_ref, wbs_ref, wo_ref, lng_ref, lnb_ref, o_ref):
    af = (of_ref[...].astype(F32) * jax.nn.silu(gf_ref[...].astype(F32))).astype(BF16)
    yf = jnp.dot(af, wbf_ref[...], preferred_element_type=F32)
    a_s = (os_ref[...].astype(F32) * jax.nn.silu(gs_ref[...].astype(F32))).astype(BF16)
    ys = jnp.dot(a_s, wbs_ref[...], preferred_element_type=F32)
    merged = (jax.nn.sigmoid(mf_ref[...].astype(F32)) * yf
              + jax.nn.sigmoid(ms_ref[...].astype(F32)) * ys)
    sub = jnp.dot(merged.astype(BF16), wo_ref[...], preferred_element_type=F32)
    z = DEEPNORM_ALPHA * x_ref[...] + gate_ref[...] * sub
    mu = jnp.mean(z, axis=-1, keepdims=True)
    zc = z - mu
    var = jnp.mean(zc * zc, axis=-1, keepdims=True)
    o_ref[...] = zc * lax.rsqrt(var + LN_EPS) * lng_ref[...] + lnb_ref[...]


def _out(proj, o_fox, o_swa, x2, ada, wbf, wbs, wo, ln_g, ln_b):
    s = x2.shape[0]
    tm = OUT_TM
    const = lambda i: (0, 0)
    return pl.pallas_call(
        _out_kernel,
        out_shape=jax.ShapeDtypeStruct((s, D_MODEL), F32),
        grid=(s // tm,),
        in_specs=[pl.BlockSpec((tm, FOX_WIDTH), lambda i: (i, 0)),
                  pl.BlockSpec((tm, FOX_WIDTH), lambda i: (i, COL_GF // FOX_WIDTH)),
                  pl.BlockSpec((tm, SWA_WIDTH), lambda i: (i, 0)),
                  pl.BlockSpec((tm, SWA_WIDTH), lambda i: (i, COL_GS // SWA_WIDTH)),
                  pl.BlockSpec((tm, D_MODEL), lambda i: (i, COL_MF // D_MODEL)),
                  pl.BlockSpec((tm, D_MODEL), lambda i: (i, COL_MS // D_MODEL)),
                  pl.BlockSpec((tm, D_MODEL), lambda i: (i, 0)),
                  pl.BlockSpec((1, D_MODEL), lambda i: (0, 2)),
                  pl.BlockSpec((FOX_WIDTH, D_MODEL), const),
                  pl.BlockSpec((SWA_WIDTH, D_MODEL), const),
                  pl.BlockSpec((D_MODEL, D_MODEL), const),
                  pl.BlockSpec((1, D_MODEL), const),
                  pl.BlockSpec((1, D_MODEL), const)],
        out_specs=pl.BlockSpec((tm, D_MODEL), lambda i: (i, 0)),
        compiler_params=pltpu.CompilerParams(dimension_semantics=("arbitrary",),
                                             vmem_limit_bytes=VMEM_LIMIT),
        name="out",
    )(o_fox, proj, o_swa, proj, proj, proj, x2, ada, wbf, wbs, wo, ln_g, ln_b)


def kernel(x, c, w_ada, b_ada, w_in, b_f, attn_sinks, w_br_fox, w_br_swa, w_out, ln_g, ln_b):
    b, s, d = x.shape
    assert (b, s, d) == (1, SEQ, D_MODEL) and w_in.shape[0] == DEPTH
    x2 = x.reshape(s, d)

    wt = jnp.swapaxes(w_in, 1, 2)[0]
    w_main = _wprep(wt)
    bf_pad = jnp.pad(b_f[0], (0, LANES - FOX_HEADS)).reshape(1, LANES)
    col_scale = jnp.asarray(_proj_col_scale())
    wbf = w_br_fox[0].astype(BF16)
    wbs = _swa_rows(w_br_swa[0]).astype(BF16)
    wo = w_out[0].astype(BF16)

    ada = _ada(c, w_ada[0], b_ada[0].reshape(1, -1))
    proj, flog = _proj(x2, ada, w_main, wt, col_scale)
    f_cum = _cum(flog, bf_pad)
    o_fox, o_swa = _attn(proj, f_cum.reshape(FOX_HEADS, 1, s), attn_sinks[0])
    out = _out(proj, o_fox, o_swa, x2, ada, wbf, wbs, wo,
               ln_g[0].reshape(1, d), ln_b[0].reshape(1, d))
    return out.reshape(b, s, d)
```

```python
import math

import numpy as np
import jax
import jax.numpy as jnp
from jax import lax
from jax.experimental import pallas as pl
from jax.experimental.pallas import tpu as pltpu

F32 = jnp.float32
BF16 = jnp.bfloat16

D_MODEL = 2048
SEQ = 8192
FOX_HEADS = 8
FOX_HEAD_DIM = 128
FOX_WIDTH = FOX_HEADS * FOX_HEAD_DIM
SWA_Q_HEADS = 16
SWA_KV_HEADS = 4
SWA_HEAD_DIM = 64
SWA_GROUP = SWA_Q_HEADS // SWA_KV_HEADS
SWA_WIDTH = SWA_Q_HEADS * SWA_HEAD_DIM
SWA_KV_WIDTH = SWA_KV_HEADS * SWA_HEAD_DIM
WINDOW = 128
LN_EPS = 1e-5
NEG_INF = -1e30
DEPTH = 1
DEEPNORM_ALPHA = (2.0 * DEPTH) ** 0.25
LOG2E = math.log2(math.e)

LANES = 128
VMEM_LIMIT = 56 * 1024 * 1024

COL_FQ = 0
COL_FK = 1024
COL_FV = 2048
COL_SQ = 3072
COL_GF = 4096
COL_GS = 5120
COL_MF = 6144
COL_MS = 8192
COL_SK = 10240
COL_SV = 10496
PROJ_WIDTH = 10752

_O_FQ, _O_FK, _O_FV, _O_FLOG = 0, 1024, 2048, 3072
_O_SQ, _O_SK, _O_SV = 3080, 4104, 4360
_O_GF, _O_GS, _O_MF, _O_MS = 4616, 5640, 6664, 8712


def _proj_col_scale():
    s = np.ones((1, PROJ_WIDTH), np.float32)
    s[0, COL_FQ:COL_FQ + 1024] = FOX_HEAD_DIM ** -0.5 * LOG2E
    s[0, COL_SQ:COL_SQ + 1024] = SWA_HEAD_DIM ** -0.5 * LOG2E
    return s


WP_TN = 512
WP_TC = 256
WP_SHIFT = FOX_HEADS
_WP_ALIGNED, _WP_SHIFTED, _WP_PERM0, _WP_PERM1 = 0, 1, 2, 3


def _wprep_tables():
    nb = PROJ_WIDTH // WP_TN
    blk_a = np.zeros((nb,), np.int32)
    blk_b = np.zeros((nb,), np.int32)
    blk_h = np.zeros((nb,), np.int32)
    mode = np.zeros((nb,), np.int32)

    def fill(col_out, col_src, width, shifted):
        for t in range(width // WP_TN):
            ob = col_out // WP_TN + t
            start = col_src + t * WP_TN - (WP_SHIFT if shifted else 0)
            assert start % WP_TN == 0
            mode[ob] = _WP_SHIFTED if shifted else _WP_ALIGNED
            blk_a[ob] = start // WP_TN
            blk_h[ob] = (start + WP_TN) // WP_SHIFT
            blk_b[ob] = -1

    def fill_perm(col_out, col_src):
        start = col_src - WP_SHIFT
        assert start % WP_TN == 0
        for t, m in enumerate((_WP_PERM0, _WP_PERM1)):
            ob = col_out // WP_TN + t
            mode[ob] = m
            blk_a[ob] = start // WP_TN
            blk_b[ob] = start // WP_TN + 1
            blk_h[ob] = (start + 2 * WP_TN) // WP_SHIFT
    fill(COL_FQ, _O_FQ, 3072, False)
    fill_perm(COL_SQ, _O_SQ)
    fill(COL_GF, _O_GF, 1024, True)
    fill_perm(COL_GS, _O_GS)
    fill(COL_MF, _O_MF, 2048, True)
    fill(COL_MS, _O_MS, 2048, True)
    fill(COL_SK, _O_SK, 512, True)
    for ob in range(nb):
        if blk_b[ob] < 0:
            blk_b[ob] = blk_b[ob - 1] if ob else 0
    return blk_a, blk_b, blk_h, mode


def _wprep_kernel(a_tab, b_tab, h_tab, mode_ref, a_ref, b_ref, h_ref, o_ref):
    mode = mode_ref[pl.program_id(0)]

    def emit(rows_of_chunk):
        for c in range(D_MODEL // WP_TC):
            cols = slice(c * WP_TC, (c + 1) * WP_TC)
            o_ref[cols, :] = rows_of_chunk(cols).T.astype(BF16)

    @pl.when(mode == _WP_ALIGNED)
    def _():
        emit(lambda cols: a_ref[:, cols])

    @pl.when(mode == _WP_SHIFTED)
    def _():
        emit(lambda cols: jnp.concatenate([a_ref[WP_SHIFT:, cols], h_ref[:, cols]], axis=0))

    def perm_rows(cols, t):
        parts = []
        for g in (2 * t, 2 * t + 1):
            for h in range(SWA_KV_HEADS):
                r0 = WP_SHIFT + (h * SWA_GROUP + g) * SWA_HEAD_DIM
                r1 = r0 + SWA_HEAD_DIM
                if r1 <= WP_TN:
                    parts.append(a_ref[r0:r1, cols])
                elif r0 >= WP_TN:
                    if r1 <= 2 * WP_TN:
                        parts.append(b_ref[r0 - WP_TN:r1 - WP_TN, cols])
                    else:
                        parts.append(jnp.concatenate([b_ref[r0 - WP_TN:, cols], h_ref[:, cols]], axis=0))
                else:
                    parts.append(jnp.concatenate([a_ref[r0:, cols], b_ref[:r1 - WP_TN, cols]], axis=0))
        return jnp.concatenate(parts, axis=0)

    @pl.when(mode == _WP_PERM0)
    def _():
        emit(lambda cols: perm_rows(cols, 0))

    @pl.when(mode == _WP_PERM1)
    def _():
        emit(lambda cols: perm_rows(cols, 1))


def _wprep(wt):
    tabs = _wprep_tables()
    grid_spec = pltpu.PrefetchScalarGridSpec(
        num_scalar_prefetch=4,
        grid=(PROJ_WIDTH // WP_TN,),
        in_specs=[pl.BlockSpec((WP_TN, D_MODEL), lambda i, a, b, h, m: (a[i], 0)),
                  pl.BlockSpec((WP_TN, D_MODEL), lambda i, a, b, h, m: (b[i], 0)),
                  pl.BlockSpec((WP_SHIFT, D_MODEL), lambda i, a, b, h, m: (h[i], 0))],
        out_specs=pl.BlockSpec((D_MODEL, WP_TN), lambda i, a, b, h, m: (0, i)),
    )
    return pl.pallas_call(
        _wprep_kernel,
        out_shape=jax.ShapeDtypeStruct((D_MODEL, PROJ_WIDTH), BF16),
        grid_spec=grid_spec,
        compiler_params=pltpu.CompilerParams(dimension_semantics=("arbitrary",),
                                             vmem_limit_bytes=VMEM_LIMIT),
        name="wprep",
    )(*[jnp.asarray(t) for t in tabs], wt, wt, wt)


def _ada_kernel(c_ref, w_ref, b_ref, o_ref):
    c8 = jnp.broadcast_to(c_ref[...], (8, D_MODEL))
    r = jnp.dot(c8, w_ref[...], preferred_element_type=F32)
    o_ref[...] = r[0:1, :] + b_ref[...]


def _ada(c, w_ada, b_ada):
    tn = 1024
    n = w_ada.shape[1]
    return pl.pallas_call(
        _ada_kernel,
        out_shape=jax.ShapeDtypeStruct((1, n), F32),
        grid=(n // tn,),
        in_specs=[pl.BlockSpec((1, D_MODEL), lambda j: (0, 0)),
                  pl.BlockSpec((D_MODEL, tn), lambda j: (0, j)),
                  pl.BlockSpec((1, tn), lambda j: (0, j))],
        out_specs=pl.BlockSpec((1, tn), lambda j: (0, j)),
        compiler_params=pltpu.CompilerParams(dimension_semantics=("arbitrary",),
                                             vmem_limit_bytes=VMEM_LIMIT),
        name="ada",
    )(c, w_ada, b_ada)


PROJ_TM = 1024
PROJ_TN = 1536
PROJ_NJ = PROJ_WIDTH // PROJ_TN
PROJ_LN_ROWS = 160
assert PROJ_NJ * PROJ_LN_ROWS >= PROJ_TM and PROJ_LN_ROWS % 16 == 0 and (PROJ_TM - PROJ_LN_ROWS) % 16 == 0


def _proj_kernel(x_ref, shift_ref, scale_ref, w_ref, wf_ref, cs_ref, o_ref, flog_ref, h0_scr, h1_scr):
    r = pl.program_id(0)
    j = pl.program_id(1)
    nb = pl.num_programs(0) - 1

    rows = pl.ds(pl.multiple_of(jnp.minimum(j * PROJ_LN_ROWS, PROJ_TM - PROJ_LN_ROWS), 16), PROJ_LN_ROWS)

    def layer_norm_slice(h_dst):
        x = x_ref[rows, :]
        mu = jnp.mean(x, axis=-1, keepdims=True)
        xc = x - mu
        var = jnp.mean(xc * xc, axis=-1, keepdims=True)
        h = xc * lax.rsqrt(var + LN_EPS) * (1.0 + scale_ref[...]) + shift_ref[...]
        h_dst[rows, :] = h.astype(BF16)

    def project(h_src):
        acc = jnp.dot(h_src[...], w_ref[...], preferred_element_type=F32)
        o_ref[...] = (acc * cs_ref[...]).astype(BF16)
        wf = jnp.concatenate([wf_ref[...], jnp.zeros((LANES - FOX_HEADS, D_MODEL), F32)], axis=0)
        flog_ref[rows, :] = lax.dot_general(h_src[rows, :], wf.astype(BF16), (((1,), (1,)), ((), ())),
                                            preferred_element_type=F32)

    @pl.when(r == 0)
    def _():
        layer_norm_slice(h0_scr)

    @pl.when((r > 0) & (r < nb) & (r % 2 == 1))
    def _():
        layer_norm_slice(h1_scr)
        project(h0_scr)

    @pl.when((r > 0) & (r < nb) & (r % 2 == 0))
    def _():
        layer_norm_slice(h0_scr)
        project(h1_scr)

    @pl.when(r == nb)
    def _():
        project(h1_scr if (SEQ // PROJ_TM) % 2 == 0 else h0_scr)


def _proj(x2, ada, w_main, wt, col_scale):
    s = x2.shape[0]
    nb = s // PROJ_TM
    prev = lambda r: jnp.maximum(r - 1, 0)
    col = lambda r, j: jnp.where(r == 0, 0, j)
    return pl.pallas_call(
        _proj_kernel,
        out_shape=(jax.ShapeDtypeStruct((s, PROJ_WIDTH), BF16),
                   jax.ShapeDtypeStruct((s, LANES), F32)),
        grid=(nb + 1, PROJ_NJ),
        in_specs=[pl.BlockSpec((PROJ_TM, D_MODEL), lambda r, j: (jnp.minimum(r, nb - 1), 0)),
                  pl.BlockSpec((1, D_MODEL), lambda r, j: (0, 0)),
                  pl.BlockSpec((1, D_MODEL), lambda r, j: (0, 1)),
                  pl.BlockSpec((D_MODEL, PROJ_TN), lambda r, j: (0, col(r, j))),
                  pl.BlockSpec((FOX_HEADS, D_MODEL), lambda r, j: (_O_FLOG // FOX_HEADS, 0)),
                  pl.BlockSpec((1, PROJ_TN), lambda r, j: (0, col(r, j)))],
        out_specs=(pl.BlockSpec((PROJ_TM, PROJ_TN), lambda r, j: (prev(r), col(r, j))),
                   pl.BlockSpec((PROJ_TM, LANES), lambda r, j: (prev(r), 0))),
        scratch_shapes=[pltpu.VMEM((PROJ_TM, D_MODEL), BF16), pltpu.VMEM((PROJ_TM, D_MODEL), BF16)],
        compiler_params=pltpu.CompilerParams(dimension_semantics=("arbitrary", "arbitrary"),
                                             vmem_limit_bytes=VMEM_LIMIT),
        name="proj",
    )(x2, ada, ada, w_main, wt, col_scale)


def _cum_kernel(flog_ref, bf_ref, f_ref):
    s = flog_ref.shape[0]
    lf = jax.nn.log_sigmoid(flog_ref[...] + bf_ref[...])
    acc = lf.T[0:FOX_HEADS, :]
    lane = lax.broadcasted_iota(jnp.int32, acc.shape, 1)
    sh = 1
    while sh < s:
        rolled = pltpu.roll(acc, sh, axis=1)
        acc = acc + jnp.where(lane >= sh, rolled, 0.0)
        sh *= 2
    for h in range(FOX_HEADS):
        f_ref[h] = acc[h:h + 1, :]


def _cum(flog, bf_pad):
    s = flog.shape[0]
    return pl.pallas_call(
        _cum_kernel,
        out_shape=jax.ShapeDtypeStruct((FOX_HEADS, 1, s), F32),
        in_specs=[pl.BlockSpec((s, LANES), lambda: (0, 0)),
                  pl.BlockSpec((1, LANES), lambda: (0, 0))],
        out_specs=pl.BlockSpec((FOX_HEADS, 1, s), lambda: (0, 0, 0)),
        compiler_params=pltpu.CompilerParams(vmem_limit_bytes=VMEM_LIMIT),
        name="cum",
    )(flog, bf_pad)


FOX_TK = 512
FOX_TQ = 2 * FOX_TK
FOX_SKIP_LOG2 = 152.0


def _attn_kernel(sink_ref, q_ref, k_ref, v_ref, f_ref, sq_ref, skp_ref, skc_ref, svp_ref, svc_ref,
                 wbf32_ref, wbs32_ref, wo32_ref,
                 o_ref, os_ref, wbf_ref, wbs_ref, wo_ref,
                 sa_scr, sb_scr, m_scr, acc_scr, kn_scr, bias_scr):
    i = pl.program_id(1)
    n_blk = pl.program_id(0) * pl.num_programs(1) + i
    tq, tk = FOX_TQ, FOX_TK
    s_len = k_ref.shape[0]

    @pl.when(n_blk <= 1)
    def _():
        _swa_tables(n_blk, sink_ref, bias_scr)

    @pl.when(i == 0)
    def _():
        kn_scr[...] = jnp.zeros(kn_scr.shape, F32)

    q0 = pl.multiple_of(i * tq, tq)
    f_q = f_ref[0, :, pl.ds(q0, tq)]
    f_base = jnp.max(f_q, axis=1, keepdims=True)
    ones = jnp.ones((tk, LANES), BF16)
    top, bot = slice(0, tk), slice(tk, tq)

    m_scr[...] = jnp.full(m_scr.shape, NEG_INF, F32)
    acc_scr[...] = jnp.zeros(acc_scr.shape, F32)

    def scores(k0, dst, rows=slice(0, FOX_TQ)):
        k_t = k_ref[pl.ds(k0, tk), :]
        s = lax.dot_general(q_ref[rows, :], k_t, (((1,), (1,)), ((), ())), preferred_element_type=F32)
        bias = (f_base - f_ref[0, :, pl.ds(k0, tk)]) * LOG2E
        dst[rows, :] = s + bias

    def softmax_pv(src, k0, rows=slice(0, FOX_TQ), causal=False):
        s = src[rows, :]
        if causal:
            n = rows.stop - rows.start
            keep = (lax.broadcasted_iota(jnp.int32, (n, tk), 1)
                    <= lax.broadcasted_iota(jnp.int32, (n, tk), 0))
            s = jnp.where(keep, s, NEG_INF)
        m_prev = m_scr[rows, :]
        m_new = jnp.maximum(m_prev, jnp.max(s, axis=1, keepdims=True))
        alpha = jnp.exp2(m_prev - m_new)
        p = jnp.exp2(s - jnp.tile(m_new, (1, tk // LANES))).astype(BF16)
        v_aug = jnp.concatenate([v_ref[pl.ds(k0, tk), :], ones], axis=1)
        pv = jnp.dot(p, v_aug, preferred_element_type=F32)
        acc_scr[rows, :] = acc_scr[rows, :] * jnp.tile(alpha, (1, 2)) + pv
        m_scr[rows, :] = m_new

    kd = pl.multiple_of(q0 + tk, tk)
    scores(q0, sa_scr)
    scores(kd, sb_scr, bot)
    softmax_pv(sa_scr, q0, top, causal=True)
    softmax_pv(sa_scr, q0, bot)
    softmax_pv(sb_scr, kd, bot, causal=True)
    scores(pl.multiple_of(jnp.maximum(q0 - tq, 0), tq), sa_scr)

    _swa_block(sq_ref, skp_ref, skc_ref, svp_ref, svc_ref, os_ref, bias_scr)

    wbf_ref[...] = wbf32_ref[...].astype(BF16)
    wbs_ref[...] = wbs32_ref[...].astype(BF16)
    wo_ref[...] = wo32_ref[...].astype(BF16)

    qf = q_ref[...].astype(F32)
    qn2 = jnp.max(jnp.sum(qf * qf, axis=1, keepdims=True), axis=0, keepdims=True)
    m_low = jnp.min(jnp.min(m_scr[...], axis=0, keepdims=True), axis=1, keepdims=True)
    kn2 = kn_scr[0:1, 0:1]
    thr = m_low - FOX_SKIP_LOG2 - jnp.sqrt(qn2 * kn2)
    kf = k_ref[pl.ds(q0, tq), :].astype(F32)
    kn2_tile = jnp.max(jnp.sum(kf * kf, axis=1, keepdims=True), axis=0, keepdims=True)
    kn_scr[...] = jnp.broadcast_to(jnp.maximum(kn2, kn2_tile), kn_scr.shape)
    pos = lax.broadcasted_iota(jnp.int32, (1, s_len), 1)
    live = ((f_base - f_ref[0]) * LOG2E >= thr) & (pos < q0)
    n_live = jnp.sum(live.astype(jnp.int32), axis=1, keepdims=True)
    n_pairs = (n_live[0, 0] + (tq - 1)) // tq

    def pair(t):
        ka = pl.multiple_of(q0 - (t + 1) * tq, tq)
        kb = pl.multiple_of(ka + tk, tk)
        scores(kb, sb_scr)
        softmax_pv(sa_scr, ka)
        scores(pl.multiple_of(jnp.maximum(ka - tq, 0), tq), sa_scr)
        softmax_pv(sb_scr, kb)

    def two_pairs(u, carry):
        pair(2 * u)
        pair(2 * u + 1)
        return carry

    lax.fori_loop(0, n_pairs // 2, two_pairs, 0)

    @pl.when(n_pairs % 2 == 1)
    def _():
        pair(n_pairs - 1)

    acc = acc_scr[...]
    o_ref[...] = (acc[:, :FOX_HEAD_DIM] / acc[:, FOX_HEAD_DIM:]).astype(BF16)


def _attn(proj, f3, sinks, w_br_fox, w_br_swa, w_out):
    s = proj.shape[0]
    tq, tk, w = FOX_TQ, FOX_TK, WINDOW
    nq = s // tq
    nsteps = FOX_HEADS * nq
    assert nsteps == s // w
    kcol = COL_SK // SWA_KV_WIDTH
    vcol = COL_SV // SWA_KV_WIDTH
    blk = lambda h, i: h * nq + i
    prv = lambda h, i: jnp.maximum(h * nq + i - 1, 0)
    rb = FOX_WIDTH // nsteps
    ro = D_MODEL // nsteps
    per64 = SWA_HEAD_DIM // rb
    assert rb % 16 == 0 and SWA_HEAD_DIM % rb == 0

    def swa_src(h, i):
        n = blk(h, i)
        g, hk = (n // per64) // SWA_KV_HEADS, (n // per64) % SWA_KV_HEADS
        return (hk * SWA_GROUP + g) * per64 + n % per64
    grid_spec = pltpu.PrefetchScalarGridSpec(
        num_scalar_prefetch=1,
        grid=(FOX_HEADS, nq),
        in_specs=[pl.BlockSpec((tq, LANES), lambda h, i, sk: (i, COL_FQ // LANES + h)),
                  pl.BlockSpec((s, LANES), lambda h, i, sk: (0, COL_FK // LANES + h)),
                  pl.BlockSpec((s, LANES), lambda h, i, sk: (0, COL_FV // LANES + h)),
                  pl.BlockSpec((1, 1, s), lambda h, i, sk: (h, 0, 0)),
                  pl.BlockSpec((w, SWA_WIDTH), lambda h, i, sk: (blk(h, i), COL_SQ // SWA_WIDTH)),
                  pl.BlockSpec((w, SWA_KV_WIDTH), lambda h, i, sk: (prv(h, i), kcol)),
                  pl.BlockSpec((w, SWA_KV_WIDTH), lambda h, i, sk: (blk(h, i), kcol)),
                  pl.BlockSpec((w, SWA_KV_WIDTH), lambda h, i, sk: (prv(h, i), vcol)),
                  pl.BlockSpec((w, SWA_KV_WIDTH), lambda h, i, sk: (blk(h, i), vcol)),
                  pl.BlockSpec((rb, D_MODEL), lambda h, i, sk: (blk(h, i), 0)),
                  pl.BlockSpec((rb, D_MODEL), lambda h, i, sk: (swa_src(h, i), 0)),
                  pl.BlockSpec((ro, D_MODEL), lambda h, i, sk: (blk(h, i), 0))],
        out_specs=(pl.BlockSpec((tq, LANES), lambda h, i, sk: (i, h)),
                   pl.BlockSpec((w, SWA_WIDTH), lambda h, i, sk: (blk(h, i), 0)),
                   pl.BlockSpec((rb, D_MODEL), lambda h, i, sk: (blk(h, i), 0)),
                   pl.BlockSpec((rb, D_MODEL), lambda h, i, sk: (blk(h, i), 0)),
                   pl.BlockSpec((ro, D_MODEL), lambda h, i, sk: (blk(h, i), 0))),
        scratch_shapes=[pltpu.VMEM((tq, tk), F32), pltpu.VMEM((tq, tk), F32),
                        pltpu.VMEM((tq, LANES), F32), pltpu.VMEM((tq, 2 * LANES), F32),
                        pltpu.VMEM((8, LANES), F32),
                        pltpu.VMEM((SWA_Q_HEADS * w, 2 * w), F32)],
    )
    return pl.pallas_call(
        _attn_kernel,
        out_shape=(jax.ShapeDtypeStruct((s, FOX_WIDTH), BF16),
                   jax.ShapeDtypeStruct((s, SWA_WIDTH), BF16),
                   jax.ShapeDtypeStruct((FOX_WIDTH, D_MODEL), BF16),
                   jax.ShapeDtypeStruct((SWA_WIDTH, D_MODEL), BF16),
                   jax.ShapeDtypeStruct((D_MODEL, D_MODEL), BF16)),
        grid_spec=grid_spec,
        compiler_params=pltpu.CompilerParams(dimension_semantics=("arbitrary", "arbitrary"),
                                             vmem_limit_bytes=VMEM_LIMIT),
        name="attn",
    )(sinks, proj, proj, proj, f3, proj, proj, proj, proj, proj, w_br_fox, w_br_swa, w_out)


def _swa_tables(n, sink_ref, bias_scr):
    w = WINDOW
    row = lax.broadcasted_iota(jnp.int32, (w, 2 * w), 0)
    col = lax.broadcasted_iota(jnp.int32, (w, 2 * w), 1)
    dist = row - col + w
    valid = (dist >= 0) & (dist < w) & ((col >= w) | (n > 0))
    distf = dist.astype(F32)
    for hq in range(SWA_Q_HEADS):
        slope = 2.0 ** (-8.0 * (hq + 1.0) / SWA_Q_HEADS)
        band = jnp.where(valid, (-slope * LOG2E) * distf, NEG_INF)
        bias_scr[hq * w:(hq + 1) * w, :] = jnp.where(col == 0, sink_ref[hq] * LOG2E, band)


def _swa_block(q_ref, kp_ref, kc_ref, vp_ref, vc_ref, o_ref, bias_scr):
    w = WINDOW
    gw = SWA_GROUP * w
    first = lax.broadcasted_iota(jnp.int32, (w, SWA_KV_WIDTH), 0) == 0
    kp = jnp.where(first, jnp.zeros_like(kp_ref[...]), kp_ref[...])
    vp = jnp.where(first, jnp.zeros_like(vp_ref[...]), vp_ref[...])
    kk = jnp.concatenate([kp, kc_ref[...]], axis=0)
    vv = jnp.concatenate([vp, vc_ref[...]], axis=0)
    lane_head = lax.broadcasted_iota(jnp.int32, (w, 2 * LANES), 1) // SWA_HEAD_DIM
    key_head = lax.broadcasted_iota(jnp.int32, (2 * w, 2 * LANES), 1) // SWA_HEAD_DIM
    parts = []
    for h in range(SWA_KV_HEADS):
        for g in range(SWA_GROUP):
            qg = q_ref[:, g * 256:(g + 1) * 256]
            parts.append(jnp.where(lane_head == h, qg, jnp.zeros_like(qg)))
    qs = jnp.concatenate(parts, axis=0)
    s = lax.dot_general(qs, kk, (((1,), (1,)), ((), ())), preferred_element_type=F32)
    s = s + bias_scr[...]
    m = jnp.max(s, axis=1, keepdims=True)
    p = jnp.exp2(s - m)
    inv = 1.0 / jnp.sum(p, axis=1, keepdims=True)
    pb = p.astype(BF16)
    p_all = jnp.concatenate([pb[h * gw:(h + 1) * gw, :] for h in range(SWA_KV_HEADS)], axis=1)
    v_blk = jnp.concatenate([jnp.where(key_head == h, vv, jnp.zeros_like(vv))
                             for h in range(SWA_KV_HEADS)], axis=0)
    o = jnp.dot(p_all, v_blk, preferred_element_type=F32)
    for g in range(SWA_GROUP):
        inv_g = jnp.broadcast_to(inv[g * w:(g + 1) * w, :], (w, 2 * LANES))
        for h in range(1, SWA_KV_HEADS):
            r0 = h * gw + g * w
            inv_g = jnp.where(lane_head == h, jnp.broadcast_to(inv[r0:r0 + w, :], (w, 2 * LANES)), inv_g)
        o_ref[:, g * 256:(g + 1) * 256] = (o[g * w:(g + 1) * w, :] * inv_g).astype(BF16)


OUT_TM = 256


def _out_kernel(of_ref, gf_ref, os_ref, gs_ref, mf_ref, ms_ref, x_ref, gate_ref,
                wbf_ref, wbs_ref, wo_ref, lng_ref, lnb_ref, o_ref):
    af = (of_ref[...].astype(F32) * jax.nn.silu(gf_ref[...].astype(F32))).astype(BF16)
    yf = jnp.dot(af, wbf_ref[...], preferred_element_type=F32)
    a_s = (os_ref[...].astype(F32) * jax.nn.silu(gs_ref[...].astype(F32))).astype(BF16)
    ys = jnp.dot(a_s, wbs_ref[...], preferred_element_type=F32)
    merged = (jax.nn.sigmoid(mf_ref[...].astype(F32)) * yf
              + jax.nn.sigmoid(ms_ref[...].astype(F32)) * ys)
    sub = jnp.dot(merged.astype(BF16), wo_ref[...], preferred_element_type=F32)
    z = DEEPNORM_ALPHA * x_ref[...] + gate_ref[...] * sub
    mu = jnp.mean(z, axis=-1, keepdims=True)
    zc = z - mu
    var = jnp.mean(zc * zc, axis=-1, keepdims=True)
    o_ref[...] = zc * lax.rsqrt(var + LN_EPS) * lng_ref[...] + lnb_ref[...]


def _out(proj, o_fox, o_swa, x2, ada, wbf, wbs, wo, ln_g, ln_b):
    s = x2.shape[0]
    tm = OUT_TM
    const = lambda i: (0, 0)
    return pl.pallas_call(
        _out_kernel,
        out_shape=jax.ShapeDtypeStruct((s, D_MODEL), F32),
        grid=(s // tm,),
        in_specs=[pl.BlockSpec((tm, FOX_WIDTH), lambda i: (i, 0)),
                  pl.BlockSpec((tm, FOX_WIDTH), lambda i: (i, COL_GF // FOX_WIDTH)),
                  pl.BlockSpec((tm, SWA_WIDTH), lambda i: (i, 0)),
                  pl.BlockSpec((tm, SWA_WIDTH), lambda i: (i, COL_GS // SWA_WIDTH)),
                  pl.BlockSpec((tm, D_MODEL), lambda i: (i, COL_MF // D_MODEL)),
                  pl.BlockSpec((tm, D_MODEL), lambda i: (i, COL_MS // D_MODEL)),
                  pl.BlockSpec((tm, D_MODEL), lambda i: (i, 0)),
                  pl.BlockSpec((1, D_MODEL), lambda i: (0, 2)),
                  pl.BlockSpec((FOX_WIDTH, D_MODEL), const),
                  pl.BlockSpec((SWA_WIDTH, D_MODEL), const),
                  pl.BlockSpec((D_MODEL, D_MODEL), const),
                  pl.BlockSpec((1, D_MODEL), const),
                  pl.BlockSpec((1, D_MODEL), const)],
        out_specs=pl.BlockSpec((tm, D_MODEL), lambda i: (i, 0)),
        compiler_params=pltpu.CompilerParams(dimension_semantics=("arbitrary",),
                                             vmem_limit_bytes=VMEM_LIMIT),
        name="out",
    )(o_fox, proj, o_swa, proj, proj, proj, x2, ada, wbf, wbs, wo, ln_g, ln_b)


def kernel(x, c, w_ada, b_ada, w_in, b_f, attn_sinks, w_br_fox, w_br_swa, w_out, ln_g, ln_b):
    b, s, d = x.shape
    assert (b, s, d) == (1, SEQ, D_MODEL) and w_in.shape[0] == DEPTH
    x2 = x.reshape(s, d)

    wt = jnp.swapaxes(w_in, 1, 2)[0]
    w_main = _wprep(wt)
    bf_pad = jnp.pad(b_f[0], (0, LANES - FOX_HEADS)).reshape(1, LANES)
    col_scale = jnp.asarray(_proj_col_scale())

    ada = _ada(c, w_ada[0], b_ada[0].reshape(1, -1))
    proj, flog = _proj(x2, ada, w_main, wt, col_scale)
    f_cum = _cum(flog, bf_pad)
    o_fox, o_swa, wbf, wbs, wo = _attn(proj, f_cum, attn_sinks[0], w_br_fox[0], w_br_swa[0], w_out[0])
    out = _out(proj, o_fox, o_swa, x2, ada, wbf, wbs, wo,
               ln_g[0].reshape(1, d), ln_b[0].reshape(1, d))
    return out.reshape(b, s, d)
```

```python
import math

import numpy as np
import jax
import jax.numpy as jnp
from jax import lax
from jax.experimental import pallas as pl
from jax.experimental.pallas import tpu as pltpu

F32 = jnp.float32
BF16 = jnp.bfloat16

D_MODEL = 2048
SEQ = 8192
FOX_HEADS = 8
FOX_HEAD_DIM = 128
FOX_WIDTH = FOX_HEADS * FOX_HEAD_DIM
SWA_Q_HEADS = 16
SWA_KV_HEADS = 4
SWA_HEAD_DIM = 64
SWA_GROUP = SWA_Q_HEADS // SWA_KV_HEADS
SWA_WIDTH = SWA_Q_HEADS * SWA_HEAD_DIM
SWA_KV_WIDTH = SWA_KV_HEADS * SWA_HEAD_DIM
WINDOW = 128
LN_EPS = 1e-5
NEG_INF = -1e30
DEPTH = 1
DEEPNORM_ALPHA = (2.0 * DEPTH) ** 0.25
LOG2E = math.log2(math.e)

LANES = 128
BF16_SUBLANES = 16
VMEM_LIMIT = 56 * 1024 * 1024

COL_FQ = 0
COL_FK = 1024
COL_FV = 2048
COL_SQ = 3072
COL_GF = 4096
COL_GS = 5120
COL_MF = 6144
COL_MS = 8192
COL_SK = 10240
COL_SV = 10496
PROJ_WIDTH = 10752

_O_FQ, _O_FK, _O_FV, _O_FLOG = 0, 1024, 2048, 3072
_O_SQ, _O_SK, _O_SV = 3080, 4104, 4360
_O_GF, _O_GS, _O_MF, _O_MS = 4616, 5640, 6664, 8712


def _proj_col_scale():
    s = np.ones((1, PROJ_WIDTH), np.float32)
    s[0, COL_FQ:COL_FQ + 1024] = FOX_HEAD_DIM ** -0.5 * LOG2E
    s[0, COL_SQ:COL_SQ + 1024] = SWA_HEAD_DIM ** -0.5 * LOG2E
    return s


WP_TN = 512
WP_TC = 256
WP_SHIFT = FOX_HEADS
_WP_ALIGNED, _WP_SHIFTED, _WP_PERM0, _WP_PERM1 = 0, 1, 2, 3
ADA_TN = 512
ADA_STEPS = 3 * D_MODEL // ADA_TN


def _wprep_tables():
    nb = PROJ_WIDTH // WP_TN
    blk_a = np.zeros((nb,), np.int32)
    blk_b = np.zeros((nb,), np.int32)
    blk_h = np.zeros((nb,), np.int32)
    mode = np.zeros((nb,), np.int32)

    def fill(col_out, col_src, width, shifted):
        for t in range(width // WP_TN):
            ob = col_out // WP_TN + t
            start = col_src + t * WP_TN - (WP_SHIFT if shifted else 0)
            assert start % WP_TN == 0
            mode[ob] = _WP_SHIFTED if shifted else _WP_ALIGNED
            blk_a[ob] = start // WP_TN
            blk_h[ob] = (start + WP_TN) // WP_SHIFT
            blk_b[ob] = -1

    def fill_perm(col_out, col_src):
        start = col_src - WP_SHIFT
        assert start % WP_TN == 0
        for t, m in enumerate((_WP_PERM0, _WP_PERM1)):
            ob = col_out // WP_TN + t
            mode[ob] = m
            blk_a[ob] = start // WP_TN
            blk_b[ob] = start // WP_TN + 1
            blk_h[ob] = (start + 2 * WP_TN) // WP_SHIFT
    fill(COL_FQ, _O_FQ, 3072, False)
    fill_perm(COL_SQ, _O_SQ)
    fill(COL_GF, _O_GF, 1024, True)
    fill_perm(COL_GS, _O_GS)
    fill(COL_MF, _O_MF, 2048, True)
    fill(COL_MS, _O_MS, 2048, True)
    fill(COL_SK, _O_SK, 512, True)
    for ob in range(nb):
        if blk_b[ob] < 0:
            blk_b[ob] = blk_b[ob - 1] if ob else 0
    return blk_a, blk_b, blk_h, mode


def _wprep_kernel(a_tab, b_tab, h_tab, mode_ref, a_ref, b_ref, h_ref, c_ref, wada_ref, bada_ref,
                  o_ref, ada_ref):
    mode = mode_ref[pl.program_id(0)]

    @pl.when(pl.program_id(0) < ADA_STEPS)
    def _():
        c8 = jnp.broadcast_to(c_ref[...], (8, D_MODEL))
        r = jnp.dot(c8, wada_ref[...], preferred_element_type=F32)
        ada_ref[...] = r[0:1, :] + bada_ref[...]

    def emit(rows_of_chunk):
        for c in range(D_MODEL // WP_TC):
            cols = slice(c * WP_TC, (c + 1) * WP_TC)
            o_ref[cols, :] = rows_of_chunk(cols).T.astype(BF16)

    @pl.when(mode == _WP_ALIGNED)
    def _():
        emit(lambda cols: a_ref[:, cols])

    @pl.when(mode == _WP_SHIFTED)
    def _():
        emit(lambda cols: jnp.concatenate([a_ref[WP_SHIFT:, cols], h_ref[:, cols]], axis=0))

    def perm_rows(cols, t):
        parts = []
        for g in (2 * t, 2 * t + 1):
            for h in range(SWA_KV_HEADS):
                r0 = WP_SHIFT + (h * SWA_GROUP + g) * SWA_HEAD_DIM
                r1 = r0 + SWA_HEAD_DIM
                if r1 <= WP_TN:
                    parts.append(a_ref[r0:r1, cols])
                elif r0 >= WP_TN:
                    if r1 <= 2 * WP_TN:
                        parts.append(b_ref[r0 - WP_TN:r1 - WP_TN, cols])
                    else:
                        parts.append(jnp.concatenate([b_ref[r0 - WP_TN:, cols], h_ref[:, cols]], axis=0))
                else:
                    parts.append(jnp.concatenate([a_ref[r0:, cols], b_ref[:r1 - WP_TN, cols]], axis=0))
        return jnp.concatenate(parts, axis=0)

    @pl.when(mode == _WP_PERM0)
    def _():
        emit(lambda cols: perm_rows(cols, 0))

    @pl.when(mode == _WP_PERM1)
    def _():
        emit(lambda cols: perm_rows(cols, 1))


def _wprep(wt, c, w_ada, b_ada):
    tabs = _wprep_tables()
    n_ada = w_ada.shape[1]
    assert n_ada == ADA_STEPS * ADA_TN and ADA_STEPS <= PROJ_WIDTH // WP_TN
    ada_blk = lambda i, a, b, h, m: (0, jnp.minimum(i, ADA_STEPS - 1))
    grid_spec = pltpu.PrefetchScalarGridSpec(
        num_scalar_prefetch=4,
        grid=(PROJ_WIDTH // WP_TN,),
        in_specs=[pl.BlockSpec((WP_TN, D_MODEL), lambda i, a, b, h, m: (a[i], 0)),
                  pl.BlockSpec((WP_TN, D_MODEL), lambda i, a, b, h, m: (b[i], 0)),
                  pl.BlockSpec((WP_SHIFT, D_MODEL), lambda i, a, b, h, m: (h[i], 0)),
                  pl.BlockSpec((1, D_MODEL), lambda i, a, b, h, m: (0, 0)),
                  pl.BlockSpec((D_MODEL, ADA_TN), ada_blk),
                  pl.BlockSpec((1, ADA_TN), ada_blk)],
        out_specs=(pl.BlockSpec((D_MODEL, WP_TN), lambda i, a, b, h, m: (0, i)),
                   pl.BlockSpec((1, ADA_TN), ada_blk)),
    )
    return pl.pallas_call(
        _wprep_kernel,
        out_shape=(jax.ShapeDtypeStruct((D_MODEL, PROJ_WIDTH), BF16),
                   jax.ShapeDtypeStruct((1, n_ada), F32)),
        grid_spec=grid_spec,
        compiler_params=pltpu.CompilerParams(dimension_semantics=("arbitrary",),
                                             vmem_limit_bytes=VMEM_LIMIT),
        name="wprep",
    )(*[jnp.asarray(t) for t in tabs], wt, wt, wt, c, w_ada, b_ada)


PROJ_TM = 1024
PROJ_TN = 1536
PROJ_NJ = PROJ_WIDTH // PROJ_TN
PROJ_LN_ROWS = 160
assert (PROJ_NJ * PROJ_LN_ROWS >= PROJ_TM and PROJ_LN_ROWS % BF16_SUBLANES == 0
        and (PROJ_TM - PROJ_LN_ROWS) % BF16_SUBLANES == 0)


def _proj_kernel(x_ref, shift_ref, scale_ref, w_ref, wf_ref, cs_ref, o_ref, flog_ref, h0_scr, h1_scr):
    r = pl.program_id(0)
    j = pl.program_id(1)
    nb = pl.num_programs(0) - 1

    rows = pl.ds(pl.multiple_of(jnp.minimum(j * PROJ_LN_ROWS, PROJ_TM - PROJ_LN_ROWS), BF16_SUBLANES),
                 PROJ_LN_ROWS)

    def layer_norm_slice(h_dst):
        x = x_ref[rows, :]
        mu = jnp.mean(x, axis=-1, keepdims=True)
        xc = x - mu
        var = jnp.mean(xc * xc, axis=-1, keepdims=True)
        h = xc * lax.rsqrt(var + LN_EPS) * (1.0 + scale_ref[...]) + shift_ref[...]
        h_dst[rows, :] = h.astype(BF16)

    def project(h_src):
        acc = jnp.dot(h_src[...], w_ref[...], preferred_element_type=F32)
        o_ref[...] = (acc * cs_ref[...]).astype(BF16)
        wf = jnp.concatenate([wf_ref[...], jnp.zeros((LANES - FOX_HEADS, D_MODEL), F32)], axis=0)
        flog_ref[rows, :] = lax.dot_general(h_src[rows, :], wf.astype(BF16), (((1,), (1,)), ((), ())),
                                            preferred_element_type=F32)

    @pl.when(r == 0)
    def _():
        layer_norm_slice(h0_scr)

    @pl.when((r > 0) & (r < nb) & (r % 2 == 1))
    def _():
        layer_norm_slice(h1_scr)
        project(h0_scr)

    @pl.when((r > 0) & (r < nb) & (r % 2 == 0))
    def _():
        layer_norm_slice(h0_scr)
        project(h1_scr)

    @pl.when(r == nb)
    def _():
        project(h1_scr if (SEQ // PROJ_TM) % 2 == 0 else h0_scr)


def _proj(x2, ada, w_main, wt, col_scale):
    s = x2.shape[0]
    nb = s // PROJ_TM
    prev = lambda r: jnp.maximum(r - 1, 0)
    col = lambda r, j: jnp.where(r == 0, 0, j)
    return pl.pallas_call(
        _proj_kernel,
        out_shape=(jax.ShapeDtypeStruct((s, PROJ_WIDTH), BF16),
                   jax.ShapeDtypeStruct((s, LANES), F32)),
        grid=(nb + 1, PROJ_NJ),
        in_specs=[pl.BlockSpec((PROJ_TM, D_MODEL), lambda r, j: (jnp.minimum(r, nb - 1), 0)),
                  pl.BlockSpec((1, D_MODEL), lambda r, j: (0, 0)),
                  pl.BlockSpec((1, D_MODEL), lambda r, j: (0, 1)),
                  pl.BlockSpec((D_MODEL, PROJ_TN), lambda r, j: (0, col(r, j))),
                  pl.BlockSpec((FOX_HEADS, D_MODEL), lambda r, j: (_O_FLOG // FOX_HEADS, 0)),
                  pl.BlockSpec((1, PROJ_TN), lambda r, j: (0, col(r, j)))],
        out_specs=(pl.BlockSpec((PROJ_TM, PROJ_TN), lambda r, j: (prev(r), col(r, j))),
                   pl.BlockSpec((PROJ_TM, LANES), lambda r, j: (prev(r), 0))),
        scratch_shapes=[pltpu.VMEM((PROJ_TM, D_MODEL), BF16), pltpu.VMEM((PROJ_TM, D_MODEL), BF16)],
        compiler_params=pltpu.CompilerParams(dimension_semantics=("arbitrary", "arbitrary"),
                                             vmem_limit_bytes=VMEM_LIMIT),
        name="proj",
    )(x2, ada, ada, w_main, wt, col_scale)


def _cum_kernel(flog_ref, bf_ref, f_ref):
    s = flog_ref.shape[0]
    lf = jax.nn.log_sigmoid(flog_ref[...] + bf_ref[...])
    acc = lf.T[0:FOX_HEADS, :]
    lane = lax.broadcasted_iota(jnp.int32, acc.shape, 1)
    sh = 1
    while sh < s:
        rolled = pltpu.roll(acc, sh, axis=1)
        acc = acc + jnp.where(lane >= sh, rolled, 0.0)
        sh *= 2
    for h in range(FOX_HEADS):
        f_ref[h] = acc[h:h + 1, :]


def _cum(flog, bf_pad):
    s = flog.shape[0]
    return pl.pallas_call(
        _cum_kernel,
        out_shape=jax.ShapeDtypeStruct((FOX_HEADS, 1, s), F32),
        in_specs=[pl.BlockSpec((s, LANES), lambda: (0, 0)),
                  pl.BlockSpec((1, LANES), lambda: (0, 0))],
        out_specs=pl.BlockSpec((FOX_HEADS, 1, s), lambda: (0, 0, 0)),
        compiler_params=pltpu.CompilerParams(vmem_limit_bytes=VMEM_LIMIT),
        name="cum",
    )(flog, bf_pad)


FOX_TK = 512
FOX_TQ = 2 * FOX_TK
FOX_SKIP_LOG2 = 152.0


def _attn_kernel(sink_ref, q_ref, k_ref, v_ref, f_ref, sq_ref, skp_ref, skc_ref, svp_ref, svc_ref,
                 wbf32_ref, wbs32_ref, wo32_ref,
                 o_ref, os_ref, wbf_ref, wbs_ref, wo_ref,
                 sa_scr, sb_scr, m_scr, acc_scr, kn_scr, bias_scr):
    i = pl.program_id(1)
    n_blk = pl.program_id(0) * pl.num_programs(1) + i
    tq, tk = FOX_TQ, FOX_TK
    s_len = k_ref.shape[0]

    @pl.when(n_blk <= 1)
    def _():
        _swa_tables(n_blk, sink_ref, bias_scr)

    @pl.when(i == 0)
    def _():
        kn_scr[...] = jnp.zeros(kn_scr.shape, F32)

    q0 = pl.multiple_of(i * tq, tq)
    f_q = f_ref[0, :, pl.ds(q0, tq)]
    f_base = jnp.max(f_q, axis=1, keepdims=True)
    ones = jnp.ones((tk, LANES), BF16)
    top, bot = slice(0, tk), slice(tk, tq)

    m_scr[...] = jnp.full(m_scr.shape, NEG_INF, F32)
    acc_scr[...] = jnp.zeros(acc_scr.shape, F32)

    def scores(k0, dst, rows=slice(0, FOX_TQ)):
        k_t = k_ref[pl.ds(k0, tk), :]
        s = lax.dot_general(q_ref[rows, :], k_t, (((1,), (1,)), ((), ())), preferred_element_type=F32)
        bias = (f_base - f_ref[0, :, pl.ds(k0, tk)]) * LOG2E
        dst[rows, :] = s + bias

    def softmax_pv(src, k0, rows=slice(0, FOX_TQ), causal=False):
        s = src[rows, :]
        if causal:
            n = rows.stop - rows.start
            keep = (lax.broadcasted_iota(jnp.int32, (n, tk), 1)
                    <= lax.broadcasted_iota(jnp.int32, (n, tk), 0))
            s = jnp.where(keep, s, NEG_INF)
        m_prev = m_scr[rows, :]
        m_new = jnp.maximum(m_prev, jnp.max(s, axis=1, keepdims=True))
        alpha = jnp.exp2(m_prev - m_new)
        p = jnp.exp2(s - jnp.tile(m_new, (1, tk // LANES))).astype(BF16)
        v_aug = jnp.concatenate([v_ref[pl.ds(k0, tk), :], ones], axis=1)
        pv = jnp.dot(p, v_aug, preferred_element_type=F32)
        acc_scr[rows, :] = acc_scr[rows, :] * jnp.tile(alpha, (1, 2)) + pv
        m_scr[rows, :] = m_new

    kd = pl.multiple_of(q0 + tk, tk)
    scores(q0, sa_scr)
    scores(kd, sb_scr, bot)
    softmax_pv(sa_scr, q0, top, causal=True)
    softmax_pv(sa_scr, q0, bot)
    softmax_pv(sb_scr, kd, bot, causal=True)
    scores(pl.multiple_of(jnp.maximum(q0 - tq, 0), tq), sa_scr)

    _swa_block(sq_ref, skp_ref, skc_ref, svp_ref, svc_ref, os_ref, bias_scr)

    wbf_ref[...] = wbf32_ref[...].astype(BF16)
    wbs_ref[...] = wbs32_ref[...].astype(BF16)
    wo_ref[...] = wo32_ref[...].astype(BF16)

    qf = q_ref[...].astype(F32)
    qn2 = jnp.max(jnp.sum(qf * qf, axis=1, keepdims=True), axis=0, keepdims=True)
    m_low = jnp.min(jnp.min(m_scr[...], axis=0, keepdims=True), axis=1, keepdims=True)
    kn2 = kn_scr[0:1, 0:1]
    thr = m_low - FOX_SKIP_LOG2 - jnp.sqrt(qn2 * kn2)
    kf = k_ref[pl.ds(q0, tq), :].astype(F32)
    kn2_tile = jnp.max(jnp.sum(kf * kf, axis=1, keepdims=True), axis=0, keepdims=True)
    kn_scr[...] = jnp.broadcast_to(jnp.maximum(kn2, kn2_tile), kn_scr.shape)
    pos = lax.broadcasted_iota(jnp.int32, (1, s_len), 1)
    live = ((f_base - f_ref[0]) * LOG2E >= thr) & (pos < q0)
    n_live = jnp.sum(live.astype(jnp.int32), axis=1, keepdims=True)
    n_pairs = (n_live[0, 0] + (tq - 1)) // tq

    def pair(t):
        ka = pl.multiple_of(q0 - (t + 1) * tq, tq)
        kb = pl.multiple_of(ka + tk, tk)
        scores(kb, sb_scr)
        softmax_pv(sa_scr, ka)
        scores(pl.multiple_of(jnp.maximum(ka - tq, 0), tq), sa_scr)
        softmax_pv(sb_scr, kb)

    def two_pairs(u, carry):
        pair(2 * u)
        pair(2 * u + 1)
        return carry

    lax.fori_loop(0, n_pairs // 2, two_pairs, 0)

    @pl.when(n_pairs % 2 == 1)
    def _():
        pair(n_pairs - 1)

    acc = acc_scr[...]
    o_ref[...] = (acc[:, :FOX_HEAD_DIM] / acc[:, FOX_HEAD_DIM:]).astype(BF16)


def _attn(proj, f3, sinks, w_br_fox, w_br_swa, w_out):
    s = proj.shape[0]
    tq, tk, w = FOX_TQ, FOX_TK, WINDOW
    nq = s // tq
    nsteps = FOX_HEADS * nq
    assert nsteps == s // w
    kcol = COL_SK // SWA_KV_WIDTH
    vcol = COL_SV // SWA_KV_WIDTH
    blk = lambda h, i: h * nq + i
    prv = lambda h, i: jnp.maximum(h * nq + i - 1, 0)
    rb = FOX_WIDTH // nsteps
    ro = D_MODEL // nsteps
    per64 = SWA_HEAD_DIM // rb
    assert rb % BF16_SUBLANES == 0 and SWA_HEAD_DIM % rb == 0

    def swa_src(h, i):
        n = blk(h, i)
        g, hk = (n // per64) // SWA_KV_HEADS, (n // per64) % SWA_KV_HEADS
        return (hk * SWA_GROUP + g) * per64 + n % per64
    grid_spec = pltpu.PrefetchScalarGridSpec(
        num_scalar_prefetch=1,
        grid=(FOX_HEADS, nq),
        in_specs=[pl.BlockSpec((tq, LANES), lambda h, i, sk: (i, COL_FQ // LANES + h)),
                  pl.BlockSpec((s, LANES), lambda h, i, sk: (0, COL_FK // LANES + h)),
                  pl.BlockSpec((s, LANES), lambda h, i, sk: (0, COL_FV // LANES + h)),
                  pl.BlockSpec((1, 1, s), lambda h, i, sk: (h, 0, 0)),
                  pl.BlockSpec((w, SWA_WIDTH), lambda h, i, sk: (blk(h, i), COL_SQ // SWA_WIDTH)),
                  pl.BlockSpec((w, SWA_KV_WIDTH), lambda h, i, sk: (prv(h, i), kcol)),
                  pl.BlockSpec((w, SWA_KV_WIDTH), lambda h, i, sk: (blk(h, i), kcol)),
                  pl.BlockSpec((w, SWA_KV_WIDTH), lambda h, i, sk: (prv(h, i), vcol)),
                  pl.BlockSpec((w, SWA_KV_WIDTH), lambda h, i, sk: (blk(h, i), vcol)),
                  pl.BlockSpec((rb, D_MODEL), lambda h, i, sk: (blk(h, i), 0)),
                  pl.BlockSpec((rb, D_MODEL), lambda h, i, sk: (swa_src(h, i), 0)),
                  pl.BlockSpec((ro, D_MODEL), lambda h, i, sk: (blk(h, i), 0))],
        out_specs=(pl.BlockSpec((tq, LANES), lambda h, i, sk: (i, h)),
                   pl.BlockSpec((w, SWA_WIDTH), lambda h, i, sk: (blk(h, i), 0)),
                   pl.BlockSpec((rb, D_MODEL), lambda h, i, sk: (blk(h, i), 0)),
                   pl.BlockSpec((rb, D_MODEL), lambda h, i, sk: (blk(h, i), 0)),
                   pl.BlockSpec((ro, D_MODEL), lambda h, i, sk: (blk(h, i), 0))),
        scratch_shapes=[pltpu.VMEM((tq, tk), F32), pltpu.VMEM((tq, tk), F32),
                        pltpu.VMEM((tq, LANES), F32), pltpu.VMEM((tq, 2 * LANES), F32),
                        pltpu.VMEM((8, LANES), F32),
                        pltpu.VMEM((SWA_Q_HEADS * w, 2 * w), F32)],
    )
    return pl.pallas_call(
        _attn_kernel,
        out_shape=(jax.ShapeDtypeStruct((s, FOX_WIDTH), BF16),
                   jax.ShapeDtypeStruct((s, SWA_WIDTH), BF16),
                   jax.ShapeDtypeStruct((FOX_WIDTH, D_MODEL), BF16),
                   jax.ShapeDtypeStruct((SWA_WIDTH, D_MODEL), BF16),
                   jax.ShapeDtypeStruct((D_MODEL, D_MODEL), BF16)),
        grid_spec=grid_spec,
        compiler_params=pltpu.CompilerParams(dimension_semantics=("arbitrary", "arbitrary"),
                                             vmem_limit_bytes=VMEM_LIMIT),
        name="attn",
    )(sinks, proj, proj, proj, f3, proj, proj, proj, proj, proj, w_br_fox, w_br_swa, w_out)


def _swa_tables(n, sink_ref, bias_scr):
    w = WINDOW
    row = lax.broadcasted_iota(jnp.int32, (w, 2 * w), 0)
    col = lax.broadcasted_iota(jnp.int32, (w, 2 * w), 1)
    dist = row - col + w
    valid = (dist >= 0) & (dist < w) & ((col >= w) | (n > 0))
    distf = dist.astype(F32)
    for hq in range(SWA_Q_HEADS):
        slope = 2.0 ** (-8.0 * (hq + 1.0) / SWA_Q_HEADS)
        band = jnp.where(valid, (-slope * LOG2E) * distf, NEG_INF)
        bias_scr[hq * w:(hq + 1) * w, :] = jnp.where(col == 0, sink_ref[hq] * LOG2E, band)


def _swa_block(q_ref, kp_ref, kc_ref, vp_ref, vc_ref, o_ref, bias_scr):
    w = WINDOW
    gw = SWA_GROUP * w
    first = lax.broadcasted_iota(jnp.int32, (w, SWA_KV_WIDTH), 0) == 0
    kp = jnp.where(first, jnp.zeros_like(kp_ref[...]), kp_ref[...])
    vp = jnp.where(first, jnp.zeros_like(vp_ref[...]), vp_ref[...])
    kk = jnp.concatenate([kp, kc_ref[...]], axis=0)
    vv = jnp.concatenate([vp, vc_ref[...]], axis=0)
    lane_head = lax.broadcasted_iota(jnp.int32, (w, 2 * LANES), 1) // SWA_HEAD_DIM
    key_head = lax.broadcasted_iota(jnp.int32, (2 * w, 2 * LANES), 1) // SWA_HEAD_DIM
    parts = []
    for h in range(SWA_KV_HEADS):
        for g in range(SWA_GROUP):
            qg = q_ref[:, g * 256:(g + 1) * 256]
            parts.append(jnp.where(lane_head == h, qg, jnp.zeros_like(qg)))
    qs = jnp.concatenate(parts, axis=0)
    s = lax.dot_general(qs, kk, (((1,), (1,)), ((), ())), preferred_element_type=F32)
    s = s + bias_scr[...]
    m = jnp.max(s, axis=1, keepdims=True)
    p = jnp.exp2(s - m)
    inv = 1.0 / jnp.sum(p, axis=1, keepdims=True)
    pb = p.astype(BF16)
    p_all = jnp.concatenate([pb[h * gw:(h + 1) * gw, :] for h in range(SWA_KV_HEADS)], axis=1)
    v_blk = jnp.concatenate([jnp.where(key_head == h, vv, jnp.zeros_like(vv))
                             for h in range(SWA_KV_HEADS)], axis=0)
    o = jnp.dot(p_all, v_blk, preferred_element_type=F32)
    for g in range(SWA_GROUP):
        inv_g = jnp.broadcast_to(inv[g * w:(g + 1) * w, :], (w, 2 * LANES))
        for h in range(1, SWA_KV_HEADS):
            r0 = h * gw + g * w
            inv_g = jnp.where(lane_head == h, jnp.broadcast_to(inv[r0:r0 + w, :], (w, 2 * LANES)), inv_g)
        o_ref[:, g * 256:(g + 1) * 256] = (o[g * w:(g + 1) * w, :] * inv_g).astype(BF16)


OUT_TM = 256


def _out_kernel(of_ref, gf_ref, os_ref, gs_ref, mf_ref, ms_ref, x_ref, gate_ref,
                wbf_ref, wbs_ref, wo_ref, lng_ref, lnb_ref, o_ref):
    af = (of_ref[...].astype(F32) * jax.nn.silu(gf_ref[...].astype(F32))).astype(BF16)
    yf = jnp.dot(af, wbf_ref[...], preferred_element_type=F32)
    a_s = (os_ref[...].astype(F32) * jax.nn.silu(gs_ref[...].astype(F32))).astype(BF16)
    ys = jnp.dot(a_s, wbs_ref[...], preferred_element_type=F32)
    merged = (jax.nn.sigmoid(mf_ref[...].astype(F32)) * yf
              + jax.nn.sigmoid(ms_ref[...].astype(F32)) * ys)
    sub = jnp.dot(merged.astype(BF16), wo_ref[...], preferred_element_type=F32)
    z = DEEPNORM_ALPHA * x_ref[...] + gate_ref[...] * sub
    mu = jnp.mean(z, axis=-1, keepdims=True)
    zc = z - mu
    var = jnp.mean(zc * zc, axis=-1, keepdims=True)
    o_ref[...] = zc * lax.rsqrt(var + LN_EPS) * lng_ref[...] + lnb_ref[...]


def _out(proj, o_fox, o_swa, x2, ada, wbf, wbs, wo, ln_g, ln_b):
    s = x2.shape[0]
    tm = OUT_TM
    const = lambda i: (0, 0)
    return pl.pallas_call(
        _out_kernel,
        out_shape=jax.ShapeDtypeStruct((s, D_MODEL), F32),
        grid=(s // tm,),
        in_specs=[pl.BlockSpec((tm, FOX_WIDTH), lambda i: (i, 0)),
                  pl.BlockSpec((tm, FOX_WIDTH), lambda i: (i, COL_GF // FOX_WIDTH)),
                  pl.BlockSpec((tm, SWA_WIDTH), lambda i: (i, 0)),
                  pl.BlockSpec((tm, SWA_WIDTH), lambda i: (i, COL_GS // SWA_WIDTH)),
                  pl.BlockSpec((tm, D_MODEL), lambda i: (i, COL_MF // D_MODEL)),
                  pl.BlockSpec((tm, D_MODEL), lambda i: (i, COL_MS // D_MODEL)),
                  pl.BlockSpec((tm, D_MODEL), lambda i: (i, 0)),
                  pl.BlockSpec((1, D_MODEL), lambda i: (0, 2)),
                  pl.BlockSpec((FOX_WIDTH, D_MODEL), const),
                  pl.BlockSpec((SWA_WIDTH, D_MODEL), const),
                  pl.BlockSpec((D_MODEL, D_MODEL), const),
                  pl.BlockSpec((1, D_MODEL), const),
                  pl.BlockSpec((1, D_MODEL), const)],
        out_specs=pl.BlockSpec((tm, D_MODEL), lambda i: (i, 0)),
        compiler_params=pltpu.CompilerParams(dimension_semantics=("arbitrary",),
                                             vmem_limit_bytes=VMEM_LIMIT),
        name="out",
    )(o_fox, proj, o_swa, proj, proj, proj, x2, ada, wbf, wbs, wo, ln_g, ln_b)


def kernel(x, c, w_ada, b_ada, w_in, b_f, attn_sinks, w_br_fox, w_br_swa, w_out, ln_g, ln_b):
    b, s, d = x.shape
    assert (b, s, d) == (1, SEQ, D_MODEL) and w_in.shape[0] == DEPTH
    x2 = x.reshape(s, d)

    wt = jnp.swapaxes(w_in, 1, 2)[0]
    w_main, ada = _wprep(wt, c, w_ada[0], b_ada[0].reshape(1, -1))
    bf_pad = jnp.pad(b_f[0], (0, LANES - FOX_HEADS)).reshape(1, LANES)
    col_scale = jnp.asarray(_proj_col_scale())

    proj, flog = _proj(x2, ada, w_main, wt, col_scale)
    f_cum = _cum(flog, bf_pad)
    o_fox, o_swa, wbf, wbs, wo = _attn(proj, f_cum, attn_sinks[0], w_br_fox[0], w_br_swa[0], w_out[0])
    out = _out(proj, o_fox, o_swa, x2, ada, wbf, wbs, wo,
               ln_g[0].reshape(1, d), ln_b[0].reshape(1, d))
    return out.reshape(b, s, d)
```

```python
import math

import numpy as np
import jax
import jax.numpy as jnp
from jax import lax
from jax.experimental import pallas as pl
from jax.experimental.pallas import tpu as pltpu

F32 = jnp.float32
BF16 = jnp.bfloat16

D_MODEL = 2048
SEQ = 8192
FOX_HEADS = 8
FOX_HEAD_DIM = 128
FOX_WIDTH = FOX_HEADS * FOX_HEAD_DIM
SWA_Q_HEADS = 16
SWA_KV_HEADS = 4
SWA_HEAD_DIM = 64
SWA_GROUP = SWA_Q_HEADS // SWA_KV_HEADS
SWA_WIDTH = SWA_Q_HEADS * SWA_HEAD_DIM
SWA_KV_WIDTH = SWA_KV_HEADS * SWA_HEAD_DIM
WINDOW = 128
LN_EPS = 1e-5
NEG_INF = -1e30
DEPTH = 1
DEEPNORM_ALPHA = (2.0 * DEPTH) ** 0.25
LOG2E = math.log2(math.e)

LANES = 128
BF16_SUBLANES = 16
VMEM_LIMIT = 56 * 1024 * 1024

COL_FQ = 0
COL_FK = 1024
COL_FV = 2048
COL_SQ = 3072
COL_GF = 4096
COL_GS = 5120
COL_MF = 6144
COL_MS = 8192
COL_SK = 10240
COL_SV = 10496
PROJ_WIDTH = 10752

_O_FQ, _O_FK, _O_FV, _O_FLOG = 0, 1024, 2048, 3072
_O_SQ, _O_SK, _O_SV = 3080, 4104, 4360
_O_GF, _O_GS, _O_MF, _O_MS = 4616, 5640, 6664, 8712


def _proj_col_scale():
    s = np.ones((1, PROJ_WIDTH), np.float32)
    s[0, COL_FQ:COL_FQ + 1024] = FOX_HEAD_DIM ** -0.5 * LOG2E
    s[0, COL_SQ:COL_SQ + 1024] = SWA_HEAD_DIM ** -0.5 * LOG2E
    return s


WP_TN = 512
WP_TC = 256
WP_SHIFT = FOX_HEADS
_WP_ALIGNED, _WP_SHIFTED, _WP_PERM0, _WP_PERM1 = 0, 1, 2, 3
ADA_TN = 512
ADA_STEPS = 3 * D_MODEL // ADA_TN


def _wprep_tables():
    nb = PROJ_WIDTH // WP_TN
    blk_a = np.zeros((nb,), np.int32)
    blk_b = np.zeros((nb,), np.int32)
    blk_h = np.zeros((nb,), np.int32)
    mode = np.zeros((nb,), np.int32)

    def fill(col_out, col_src, width, shifted):
        for t in range(width // WP_TN):
            ob = col_out // WP_TN + t
            start = col_src + t * WP_TN - (WP_SHIFT if shifted else 0)
            assert start % WP_TN == 0
            mode[ob] = _WP_SHIFTED if shifted else _WP_ALIGNED
            blk_a[ob] = start // WP_TN
            blk_h[ob] = (start + WP_TN) // WP_SHIFT
            blk_b[ob] = -1

    def fill_perm(col_out, col_src):
        start = col_src - WP_SHIFT
        assert start % WP_TN == 0
        for t, m in enumerate((_WP_PERM0, _WP_PERM1)):
            ob = col_out // WP_TN + t
            mode[ob] = m
            blk_a[ob] = start // WP_TN
            blk_b[ob] = start // WP_TN + 1
            blk_h[ob] = (start + 2 * WP_TN) // WP_SHIFT
    fill(COL_FQ, _O_FQ, 3072, False)
    fill_perm(COL_SQ, _O_SQ)
    fill(COL_GF, _O_GF, 1024, True)
    fill_perm(COL_GS, _O_GS)
    fill(COL_MF, _O_MF, 2048, True)
    fill(COL_MS, _O_MS, 2048, True)
    fill(COL_SK, _O_SK, 512, True)
    for ob in range(nb):
        if blk_b[ob] < 0:
            blk_b[ob] = blk_b[ob - 1] if ob else 0
    return blk_a, blk_b, blk_h, mode


def _wprep_kernel(a_tab, b_tab, h_tab, mode_ref, a_ref, b_ref, h_ref, c_ref, wada_ref, bada_ref,
                  o_ref, ada_ref):
    mode = mode_ref[pl.program_id(0)]

    @pl.when(pl.program_id(0) < ADA_STEPS)
    def _():
        c8 = jnp.broadcast_to(c_ref[...], (8, D_MODEL))
        r = jnp.dot(c8, wada_ref[...], preferred_element_type=F32)
        ada_ref[...] = r[0:1, :] + bada_ref[...]

    def emit(rows_of_chunk):
        for c in range(D_MODEL // WP_TC):
            cols = slice(c * WP_TC, (c + 1) * WP_TC)
            o_ref[cols, :] = rows_of_chunk(cols).T.astype(BF16)

    @pl.when(mode == _WP_ALIGNED)
    def _():
        emit(lambda cols: a_ref[:, cols])

    @pl.when(mode == _WP_SHIFTED)
    def _():
        emit(lambda cols: jnp.concatenate([a_ref[WP_SHIFT:, cols], h_ref[:, cols]], axis=0))

    def perm_rows(cols, t):
        parts = []
        for g in (2 * t, 2 * t + 1):
            for h in range(SWA_KV_HEADS):
                r0 = WP_SHIFT + (h * SWA_GROUP + g) * SWA_HEAD_DIM
                r1 = r0 + SWA_HEAD_DIM
                if r1 <= WP_TN:
                    parts.append(a_ref[r0:r1, cols])
                elif r0 >= WP_TN:
                    if r1 <= 2 * WP_TN:
                        parts.append(b_ref[r0 - WP_TN:r1 - WP_TN, cols])
                    else:
                        parts.append(jnp.concatenate([b_ref[r0 - WP_TN:, cols], h_ref[:, cols]], axis=0))
                else:
                    parts.append(jnp.concatenate([a_ref[r0:, cols], b_ref[:r1 - WP_TN, cols]], axis=0))
        return jnp.concatenate(parts, axis=0)

    @pl.when(mode == _WP_PERM0)
    def _():
        emit(lambda cols: perm_rows(cols, 0))

    @pl.when(mode == _WP_PERM1)
    def _():
        emit(lambda cols: perm_rows(cols, 1))


def _wprep(wt, c, w_ada, b_ada):
    tabs = _wprep_tables()
    n_ada = w_ada.shape[1]
    assert n_ada == ADA_STEPS * ADA_TN and ADA_STEPS <= PROJ_WIDTH // WP_TN
    ada_blk = lambda i, a, b, h, m: (0, jnp.minimum(i, ADA_STEPS - 1))
    grid_spec = pltpu.PrefetchScalarGridSpec(
        num_scalar_prefetch=4,
        grid=(PROJ_WIDTH // WP_TN,),
        in_specs=[pl.BlockSpec((WP_TN, D_MODEL), lambda i, a, b, h, m: (a[i], 0)),
                  pl.BlockSpec((WP_TN, D_MODEL), lambda i, a, b, h, m: (b[i], 0)),
                  pl.BlockSpec((WP_SHIFT, D_MODEL), lambda i, a, b, h, m: (h[i], 0)),
                  pl.BlockSpec((1, D_MODEL), lambda i, a, b, h, m: (0, 0)),
                  pl.BlockSpec((D_MODEL, ADA_TN), ada_blk),
                  pl.BlockSpec((1, ADA_TN), ada_blk)],
        out_specs=(pl.BlockSpec((D_MODEL, WP_TN), lambda i, a, b, h, m: (0, i)),
                   pl.BlockSpec((1, ADA_TN), ada_blk)),
    )
    return pl.pallas_call(
        _wprep_kernel,
        out_shape=(jax.ShapeDtypeStruct((D_MODEL, PROJ_WIDTH), BF16),
                   jax.ShapeDtypeStruct((1, n_ada), F32)),
        grid_spec=grid_spec,
        compiler_params=pltpu.CompilerParams(dimension_semantics=("arbitrary",),
                                             vmem_limit_bytes=VMEM_LIMIT),
        name="wprep",
    )(*[jnp.asarray(t) for t in tabs], wt, wt, wt, c, w_ada, b_ada)


PROJ_TM = 1024
PROJ_TN = 1536
PROJ_NJ = PROJ_WIDTH // PROJ_TN
PROJ_LN_ROWS = 160
assert (PROJ_NJ * PROJ_LN_ROWS >= PROJ_TM and PROJ_LN_ROWS % BF16_SUBLANES == 0
        and (PROJ_TM - PROJ_LN_ROWS) % BF16_SUBLANES == 0)


def _proj_kernel(x_ref, shift_ref, scale_ref, w_ref, wf_ref, cs_ref, o_ref, flog_ref, h0_scr, h1_scr):
    r = pl.program_id(0)
    j = pl.program_id(1)
    nb = pl.num_programs(0) - 1

    rows = pl.ds(pl.multiple_of(jnp.minimum(j * PROJ_LN_ROWS, PROJ_TM - PROJ_LN_ROWS), BF16_SUBLANES),
                 PROJ_LN_ROWS)

    def layer_norm_slice(h_dst):
        x = x_ref[rows, :]
        mu = jnp.mean(x, axis=-1, keepdims=True)
        xc = x - mu
        var = jnp.mean(xc * xc, axis=-1, keepdims=True)
        h = xc * lax.rsqrt(var + LN_EPS) * (1.0 + scale_ref[...]) + shift_ref[...]
        h_dst[rows, :] = h.astype(BF16)

    def project(h_src):
        acc = jnp.dot(h_src[...], w_ref[...], preferred_element_type=F32)
        o_ref[...] = (acc * cs_ref[...]).astype(BF16)
        wf = jnp.concatenate([wf_ref[...], jnp.zeros((LANES - FOX_HEADS, D_MODEL), F32)], axis=0)
        flog_ref[rows, :] = lax.dot_general(h_src[rows, :], wf.astype(BF16), (((1,), (1,)), ((), ())),
                                            preferred_element_type=F32)

    @pl.when(r == 0)
    def _():
        layer_norm_slice(h0_scr)

    @pl.when((r > 0) & (r < nb) & (r % 2 == 1))
    def _():
        layer_norm_slice(h1_scr)
        project(h0_scr)

    @pl.when((r > 0) & (r < nb) & (r % 2 == 0))
    def _():
        layer_norm_slice(h0_scr)
        project(h1_scr)

    @pl.when(r == nb)
    def _():
        project(h1_scr if (SEQ // PROJ_TM) % 2 == 0 else h0_scr)


def _proj(x2, ada, w_main, wt, col_scale):
    s = x2.shape[0]
    nb = s // PROJ_TM
    prev = lambda r: jnp.maximum(r - 1, 0)
    col = lambda r, j: jnp.where(r == 0, 0, j)
    return pl.pallas_call(
        _proj_kernel,
        out_shape=(jax.ShapeDtypeStruct((s, PROJ_WIDTH), BF16),
                   jax.ShapeDtypeStruct((s, LANES), F32)),
        grid=(nb + 1, PROJ_NJ),
        in_specs=[pl.BlockSpec((PROJ_TM, D_MODEL), lambda r, j: (jnp.minimum(r, nb - 1), 0)),
                  pl.BlockSpec((1, D_MODEL), lambda r, j: (0, 0)),
                  pl.BlockSpec((1, D_MODEL), lambda r, j: (0, 1)),
                  pl.BlockSpec((D_MODEL, PROJ_TN), lambda r, j: (0, col(r, j))),
                  pl.BlockSpec((FOX_HEADS, D_MODEL), lambda r, j: (_O_FLOG // FOX_HEADS, 0)),
                  pl.BlockSpec((1, PROJ_TN), lambda r, j: (0, col(r, j)))],
        out_specs=(pl.BlockSpec((PROJ_TM, PROJ_TN), lambda r, j: (prev(r), col(r, j))),
                   pl.BlockSpec((PROJ_TM, LANES), lambda r, j: (prev(r), 0))),
        scratch_shapes=[pltpu.VMEM((PROJ_TM, D_MODEL), BF16), pltpu.VMEM((PROJ_TM, D_MODEL), BF16)],
        compiler_params=pltpu.CompilerParams(dimension_semantics=("arbitrary", "arbitrary"),
                                             vmem_limit_bytes=VMEM_LIMIT),
        name="proj",
    )(x2, ada, ada, w_main, wt, col_scale)


def _cum_kernel(flog_ref, bf_ref, f_ref):
    s = flog_ref.shape[0]
    lf = jax.nn.log_sigmoid(flog_ref[...] + bf_ref[...])
    acc = lf.T[0:FOX_HEADS, :]
    lane = lax.broadcasted_iota(jnp.int32, acc.shape, 1)
    sh = 1
    while sh < s:
        rolled = pltpu.roll(acc, sh, axis=1)
        acc = acc + jnp.where(lane >= sh, rolled, 0.0)
        sh *= 2
    for h in range(FOX_HEADS):
        f_ref[h] = acc[h:h + 1, :]


def _cum(flog, bf_pad):
    s = flog.shape[0]
    return pl.pallas_call(
        _cum_kernel,
        out_shape=jax.ShapeDtypeStruct((FOX_HEADS, 1, s), F32),
        in_specs=[pl.BlockSpec((s, LANES), lambda: (0, 0)),
                  pl.BlockSpec((1, LANES), lambda: (0, 0))],
        out_specs=pl.BlockSpec((FOX_HEADS, 1, s), lambda: (0, 0, 0)),
        compiler_params=pltpu.CompilerParams(vmem_limit_bytes=VMEM_LIMIT),
        name="cum",
    )(flog, bf_pad)


FOX_TK = 512
FOX_TQ = 2 * FOX_TK
FOX_SKIP_LOG2 = 152.0


def _attn_kernel(sink_ref, q_ref, k_ref, v_ref, f_ref, sq_ref, skp_ref, skc_ref, svp_ref, svc_ref,
                 wbf32_ref, wbs32_ref, wo32_ref,
                 o_ref, os_ref, wbf_ref, wbs_ref, wo_ref,
                 sa_scr, sb_scr, m_scr, acc_scr, kn_scr, bias_scr):
    i = pl.program_id(1)
    n_blk = pl.program_id(0) * pl.num_programs(1) + i
    tq, tk = FOX_TQ, FOX_TK
    s_len = k_ref.shape[0]

    @pl.when(n_blk <= 1)
    def _():
        _swa_tables(n_blk, sink_ref, bias_scr)

    @pl.when(i == 0)
    def _():
        kn_scr[...] = jnp.zeros(kn_scr.shape, F32)

    q0 = pl.multiple_of(i * tq, tq)
    f_q = f_ref[0, :, pl.ds(q0, tq)]
    f_base = jnp.max(f_q, axis=1, keepdims=True)
    ones = jnp.ones((tk, LANES), BF16)
    top, bot = slice(0, tk), slice(tk, tq)

    m_scr[...] = jnp.full(m_scr.shape, NEG_INF, F32)
    acc_scr[...] = jnp.zeros(acc_scr.shape, F32)

    def scores(k0, dst, rows=slice(0, FOX_TQ)):
        k_t = k_ref[pl.ds(k0, tk), :]
        s = lax.dot_general(q_ref[rows, :], k_t, (((1,), (1,)), ((), ())), preferred_element_type=F32)
        bias = (f_base - f_ref[0, :, pl.ds(k0, tk)]) * LOG2E
        dst[rows, :] = s + bias

    def softmax_pv(src, k0, rows=slice(0, FOX_TQ), causal=False):
        s = src[rows, :]
        if causal:
            n = rows.stop - rows.start
            keep = (lax.broadcasted_iota(jnp.int32, (n, tk), 1)
                    <= lax.broadcasted_iota(jnp.int32, (n, tk), 0))
            s = jnp.where(keep, s, NEG_INF)
        m_prev = m_scr[rows, :]
        m_new = jnp.maximum(m_prev, jnp.max(s, axis=1, keepdims=True))
        alpha = jnp.exp2(m_prev - m_new)
        p = jnp.exp2(s - jnp.tile(m_new, (1, tk // LANES))).astype(BF16)
        v_aug = jnp.concatenate([v_ref[pl.ds(k0, tk), :], ones], axis=1)
        pv = jnp.dot(p, v_aug, preferred_element_type=F32)
        acc_scr[rows, :] = acc_scr[rows, :] * jnp.tile(alpha, (1, 2)) + pv
        m_scr[rows, :] = m_new

    kd = pl.multiple_of(q0 + tk, tk)
    scores(q0, sa_scr)
    scores(kd, sb_scr, bot)
    softmax_pv(sa_scr, q0, top, causal=True)
    softmax_pv(sa_scr, q0, bot)
    softmax_pv(sb_scr, kd, bot, causal=True)
    scores(pl.multiple_of(jnp.maximum(q0 - tq, 0), tq), sa_scr)

    _swa_block(sq_ref, skp_ref, skc_ref, svp_ref, svc_ref, os_ref, bias_scr)

    wbf_ref[...] = wbf32_ref[...].astype(BF16)
    wbs_ref[...] = wbs32_ref[...].astype(BF16)
    wo_ref[...] = wo32_ref[...].astype(BF16)

    qf = q_ref[...].astype(F32)
    qn2 = jnp.max(jnp.sum(qf * qf, axis=1, keepdims=True), axis=0, keepdims=True)
    m_low = jnp.min(jnp.min(m_scr[...], axis=0, keepdims=True), axis=1, keepdims=True)
    kn2 = kn_scr[0:1, 0:1]
    thr = m_low - FOX_SKIP_LOG2 - jnp.sqrt(qn2 * kn2)
    kf = k_ref[pl.ds(q0, tq), :].astype(F32)
    kn2_tile = jnp.max(jnp.sum(kf * kf, axis=1, keepdims=True), axis=0, keepdims=True)
    kn_scr[...] = jnp.broadcast_to(jnp.maximum(kn2, kn2_tile), kn_scr.shape)
    pos = lax.broadcasted_iota(jnp.int32, (1, s_len), 1)
    live = ((f_base - f_ref[0]) * LOG2E >= thr) & (pos < q0)
    n_live = jnp.sum(live.astype(jnp.int32), axis=1, keepdims=True)
    n_pairs = (n_live[0, 0] + (tq - 1)) // tq

    def pair(t):
        ka = pl.multiple_of(q0 - (t + 1) * tq, tq)
        kb = pl.multiple_of(ka + tk, tk)
        scores(kb, sb_scr)
        softmax_pv(sa_scr, ka)
        scores(pl.multiple_of(jnp.maximum(ka - tq, 0), tq), sa_scr)
        softmax_pv(sb_scr, kb)

    def run_pairs(count):
        def branch():
            for t in range(count):
                pair(t)
        return branch

    lax.switch(n_pairs, [run_pairs(count) for count in range(s_len // tq)])

    acc = acc_scr[...]
    o_ref[...] = (acc[:, :FOX_HEAD_DIM] / acc[:, FOX_HEAD_DIM:]).astype(BF16)


def _attn(proj, f3, sinks, w_br_fox, w_br_swa, w_out):
    s = proj.shape[0]
    tq, tk, w = FOX_TQ, FOX_TK, WINDOW
    nq = s // tq
    nsteps = FOX_HEADS * nq
    assert nsteps == s // w
    kcol = COL_SK // SWA_KV_WIDTH
    vcol = COL_SV // SWA_KV_WIDTH
    blk = lambda h, i: h * nq + i
    prv = lambda h, i: jnp.maximum(h * nq + i - 1, 0)
    rb = FOX_WIDTH // nsteps
    ro = D_MODEL // nsteps
    per64 = SWA_HEAD_DIM // rb
    assert rb % BF16_SUBLANES == 0 and SWA_HEAD_DIM % rb == 0

    def swa_src(h, i):
        n = blk(h, i)
        g, hk = (n // per64) // SWA_KV_HEADS, (n // per64) % SWA_KV_HEADS
        return (hk * SWA_GROUP + g) * per64 + n % per64
    grid_spec = pltpu.PrefetchScalarGridSpec(
        num_scalar_prefetch=1,
        grid=(FOX_HEADS, nq),
        in_specs=[pl.BlockSpec((tq, LANES), lambda h, i, sk: (i, COL_FQ // LANES + h)),
                  pl.BlockSpec((s, LANES), lambda h, i, sk: (0, COL_FK // LANES + h)),
                  pl.BlockSpec((s, LANES), lambda h, i, sk: (0, COL_FV // LANES + h)),
                  pl.BlockSpec((1, 1, s), lambda h, i, sk: (h, 0, 0)),
                  pl.BlockSpec((w, SWA_WIDTH), lambda h, i, sk: (blk(h, i), COL_SQ // SWA_WIDTH)),
                  pl.BlockSpec((w, SWA_KV_WIDTH), lambda h, i, sk: (prv(h, i), kcol)),
                  pl.BlockSpec((w, SWA_KV_WIDTH), lambda h, i, sk: (blk(h, i), kcol)),
                  pl.BlockSpec((w, SWA_KV_WIDTH), lambda h, i, sk: (prv(h, i), vcol)),
                  pl.BlockSpec((w, SWA_KV_WIDTH), lambda h, i, sk: (blk(h, i), vcol)),
                  pl.BlockSpec((rb, D_MODEL), lambda h, i, sk: (blk(h, i), 0)),
                  pl.BlockSpec((rb, D_MODEL), lambda h, i, sk: (swa_src(h, i), 0)),
                  pl.BlockSpec((ro, D_MODEL), lambda h, i, sk: (blk(h, i), 0))],
        out_specs=(pl.BlockSpec((tq, LANES), lambda h, i, sk: (i, h)),
                   pl.BlockSpec((w, SWA_WIDTH), lambda h, i, sk: (blk(h, i), 0)),
                   pl.BlockSpec((rb, D_MODEL), lambda h, i, sk: (blk(h, i), 0)),
                   pl.BlockSpec((rb, D_MODEL), lambda h, i, sk: (blk(h, i), 0)),
                   pl.BlockSpec((ro, D_MODEL), lambda h, i, sk: (blk(h, i), 0))),
        scratch_shapes=[pltpu.VMEM((tq, tk), F32), pltpu.VMEM((tq, tk), F32),
                        pltpu.VMEM((tq, LANES), F32), pltpu.VMEM((tq, 2 * LANES), F32),
                        pltpu.VMEM((8, LANES), F32),
                        pltpu.VMEM((SWA_Q_HEADS * w, 2 * w), F32)],
    )
    return pl.pallas_call(
        _attn_kernel,
        out_shape=(jax.ShapeDtypeStruct((s, FOX_WIDTH), BF16),
                   jax.ShapeDtypeStruct((s, SWA_WIDTH), BF16),
                   jax.ShapeDtypeStruct((FOX_WIDTH, D_MODEL), BF16),
                   jax.ShapeDtypeStruct((SWA_WIDTH, D_MODEL), BF16),
                   jax.ShapeDtypeStruct((D_MODEL, D_MODEL), BF16)),
        grid_spec=grid_spec,
        compiler_params=pltpu.CompilerParams(dimension_semantics=("arbitrary", "arbitrary"),
                                             vmem_limit_bytes=VMEM_LIMIT),
        name="attn",
    )(sinks, proj, proj, proj, f3, proj, proj, proj, proj, proj, w_br_fox, w_br_swa, w_out)


def _swa_tables(n, sink_ref, bias_scr):
    w = WINDOW
    row = lax.broadcasted_iota(jnp.int32, (w, 2 * w), 0)
    col = lax.broadcasted_iota(jnp.int32, (w, 2 * w), 1)
    dist = row - col + w
    valid = (dist >= 0) & (dist < w) & ((col >= w) | (n > 0))
    distf = dist.astype(F32)
    for hq in range(SWA_Q_HEADS):
        slope = 2.0 ** (-8.0 * (hq + 1.0) / SWA_Q_HEADS)
        band = jnp.where(valid, (-slope * LOG2E) * distf, NEG_INF)
        bias_scr[hq * w:(hq + 1) * w, :] = jnp.where(col == 0, sink_ref[hq] * LOG2E, band)


def _swa_block(q_ref, kp_ref, kc_ref, vp_ref, vc_ref, o_ref, bias_scr):
    w = WINDOW
    gw = SWA_GROUP * w
    first = lax.broadcasted_iota(jnp.int32, (w, SWA_KV_WIDTH), 0) == 0
    kp = jnp.where(first, jnp.zeros_like(kp_ref[...]), kp_ref[...])
    vp = jnp.where(first, jnp.zeros_like(vp_ref[...]), vp_ref[...])
    kk = jnp.concatenate([kp, kc_ref[...]], axis=0)
    vv = jnp.concatenate([vp, vc_ref[...]], axis=0)
    lane_head = lax.broadcasted_iota(jnp.int32, (w, 2 * LANES), 1) // SWA_HEAD_DIM
    key_head = lax.broadcasted_iota(jnp.int32, (2 * w, 2 * LANES), 1) // SWA_HEAD_DIM
    parts = []
    for h in range(SWA_KV_HEADS):
        for g in range(SWA_GROUP):
            qg = q_ref[:, g * 256:(g + 1) * 256]
            parts.append(jnp.where(lane_head == h, qg, jnp.zeros_like(qg)))
    qs = jnp.concatenate(parts, axis=0)
    s = lax.dot_general(qs, kk, (((1,), (1,)), ((), ())), preferred_element_type=F32)
    s = s + bias_scr[...]
    m = jnp.max(s, axis=1, keepdims=True)
    p = jnp.exp2(s - m)
    inv = 1.0 / jnp.sum(p, axis=1, keepdims=True)
    pb = p.astype(BF16)
    p_all = jnp.concatenate([pb[h * gw:(h + 1) * gw, :] for h in range(SWA_KV_HEADS)], axis=1)
    v_blk = jnp.concatenate([jnp.where(key_head == h, vv, jnp.zeros_like(vv))
                             for h in range(SWA_KV_HEADS)], axis=0)
    o = jnp.dot(p_all, v_blk, preferred_element_type=F32)
    for g in range(SWA_GROUP):
        inv_g = jnp.broadcast_to(inv[g * w:(g + 1) * w, :], (w, 2 * LANES))
        for h in range(1, SWA_KV_HEADS):
            r0 = h * gw + g * w
            inv_g = jnp.where(lane_head == h, jnp.broadcast_to(inv[r0:r0 + w, :], (w, 2 * LANES)), inv_g)
        o_ref[:, g * 256:(g + 1) * 256] = (o[g * w:(g + 1) * w, :] * inv_g).astype(BF16)


OUT_TM = 256


def _out_kernel(of_ref, gf_ref, os_ref, gs_ref, mf_ref, ms_ref, x_ref, gate_ref,
                wbf_ref, wbs_ref, wo_ref, lng_ref, lnb_ref, o_ref):
    af = (of_ref[...].astype(F32) * jax.nn.silu(gf_ref[...].astype(F32))).astype(BF16)
    yf = jnp.dot(af, wbf_ref[...], preferred_element_type=F32)
    a_s = (os_ref[...].astype(F32) * jax.nn.silu(gs_ref[...].astype(F32))).astype(BF16)
    ys = jnp.dot(a_s, wbs_ref[...], preferred_element_type=F32)
    merged = (jax.nn.sigmoid(mf_ref[...].astype(F32)) * yf
              + jax.nn.sigmoid(ms_ref[...].astype(F32)) * ys)
    sub = jnp.dot(merged.astype(BF16), wo_ref[...], preferred_element_type=F32)
    z = DEEPNORM_ALPHA * x_ref[...] + gate_ref[...] * sub
    mu = jnp.mean(z, axis=-1, keepdims=True)
    zc = z - mu
    var = jnp.mean(zc * zc, axis=-1, keepdims=True)
    o_ref[...] = zc * lax.rsqrt(var + LN_EPS) * lng_ref[...] + lnb_ref[...]


def _out(proj, o_fox, o_swa, x2, ada, wbf, wbs, wo, ln_g, ln_b):
    s = x2.shape[0]
    tm = OUT_TM
    const = lambda i: (0, 0)
    return pl.pallas_call(
        _out_kernel,
        out_shape=jax.ShapeDtypeStruct((s, D_MODEL), F32),
        grid=(s // tm,),
        in_specs=[pl.BlockSpec((tm, FOX_WIDTH), lambda i: (i, 0)),
                  pl.BlockSpec((tm, FOX_WIDTH), lambda i: (i, COL_GF // FOX_WIDTH)),
                  pl.BlockSpec((tm, SWA_WIDTH), lambda i: (i, 0)),
                  pl.BlockSpec((tm, SWA_WIDTH), lambda i: (i, COL_GS // SWA_WIDTH)),
                  pl.BlockSpec((tm, D_MODEL), lambda i: (i, COL_MF // D_MODEL)),
                  pl.BlockSpec((tm, D_MODEL), lambda i: (i, COL_MS // D_MODEL)),
                  pl.BlockSpec((tm, D_MODEL), lambda i: (i, 0)),
                  pl.BlockSpec((1, D_MODEL), lambda i: (0, 2)),
                  pl.BlockSpec((FOX_WIDTH, D_MODEL), const),
                  pl.BlockSpec((SWA_WIDTH, D_MODEL), const),
                  pl.BlockSpec((D_MODEL, D_MODEL), const),
                  pl.BlockSpec((1, D_MODEL), const),
                  pl.BlockSpec((1, D_MODEL), const)],
        out_specs=pl.BlockSpec((tm, D_MODEL), lambda i: (i, 0)),
        compiler_params=pltpu.CompilerParams(dimension_semantics=("arbitrary",),
                                             vmem_limit_bytes=VMEM_LIMIT),
        name="out",
    )(o_fox, proj, o_swa, proj, proj, proj, x2, ada, wbf, wbs, wo, ln_g, ln_b)


def kernel(x, c, w_ada, b_ada, w_in, b_f, attn_sinks, w_br_fox, w_br_swa, w_out, ln_g, ln_b):
    b, s, d = x.shape
    assert (b, s, d) == (1, SEQ, D_MODEL) and w_in.shape[0] == DEPTH
    x2 = x.reshape(s, d)

    wt = jnp.swapaxes(w_in, 1, 2)[0]
    w_main, ada = _wprep(wt, c, w_ada[0], b_ada[0].reshape(1, -1))
    bf_pad = jnp.pad(b_f[0], (0, LANES - FOX_HEADS)).reshape(1, LANES)
    col_scale = jnp.asarray(_proj_col_scale())

    proj, flog = _proj(x2, ada, w_main, wt, col_scale)
    f_cum = _cum(flog, bf_pad)
    o_fox, o_swa, wbf, wbs, wo = _attn(proj, f_cum, attn_sinks[0], w_br_fox[0], w_br_swa[0], w_out[0])
    out = _out(proj, o_fox, o_swa, x2, ada, wbf, wbs, wo,
               ln_g[0].reshape(1, d), ln_b[0].reshape(1, d))
    return out.reshape(b, s, d)
```

```python
import math

import numpy as np
import jax
import jax.numpy as jnp
from jax import lax
from jax.experimental import pallas as pl
from jax.experimental.pallas import tpu as pltpu

F32 = jnp.float32
BF16 = jnp.bfloat16

D_MODEL = 2048
SEQ = 8192
FOX_HEADS = 8
FOX_HEAD_DIM = 128
FOX_WIDTH = FOX_HEADS * FOX_HEAD_DIM
SWA_Q_HEADS = 16
SWA_KV_HEADS = 4
SWA_HEAD_DIM = 64
SWA_GROUP = SWA_Q_HEADS // SWA_KV_HEADS
SWA_WIDTH = SWA_Q_HEADS * SWA_HEAD_DIM
SWA_KV_WIDTH = SWA_KV_HEADS * SWA_HEAD_DIM
WINDOW = 128
LN_EPS = 1e-5
NEG_INF = -1e30
DEPTH = 1
DEEPNORM_ALPHA = (2.0 * DEPTH) ** 0.25
LOG2E = math.log2(math.e)

LANES = 128
BF16_SUBLANES = 16
VMEM_LIMIT = 56 * 1024 * 1024

COL_FQ = 0
COL_FK = 1024
COL_FV = 2048
COL_SQ = 3072
COL_GF = 4096
COL_GS = 5120
COL_MF = 6144
COL_MS = 8192
COL_SK = 10240
COL_SV = 10496
PROJ_WIDTH = 10752

_O_FQ, _O_FK, _O_FV, _O_FLOG = 0, 1024, 2048, 3072
_O_SQ, _O_SK, _O_SV = 3080, 4104, 4360
_O_GF, _O_GS, _O_MF, _O_MS = 4616, 5640, 6664, 8712


def _proj_col_scale():
    s = np.ones((1, PROJ_WIDTH), np.float32)
    s[0, COL_FQ:COL_FQ + 1024] = FOX_HEAD_DIM ** -0.5 * LOG2E
    s[0, COL_SQ:COL_SQ + 1024] = SWA_HEAD_DIM ** -0.5 * LOG2E
    return s


WP_TN = 512
WP_TC = 256
WP_SHIFT = FOX_HEADS
_WP_ALIGNED, _WP_SHIFTED, _WP_PERM0, _WP_PERM1 = 0, 1, 2, 3
ADA_TN = 512
ADA_STEPS = 3 * D_MODEL // ADA_TN


def _wprep_tables():
    nb = PROJ_WIDTH // WP_TN
    blk_a = np.zeros((nb,), np.int32)
    blk_b = np.zeros((nb,), np.int32)
    blk_h = np.zeros((nb,), np.int32)
    mode = np.zeros((nb,), np.int32)

    def fill(col_out, col_src, width, shifted):
        for t in range(width // WP_TN):
            ob = col_out // WP_TN + t
            start = col_src + t * WP_TN - (WP_SHIFT if shifted else 0)
            assert start % WP_TN == 0
            mode[ob] = _WP_SHIFTED if shifted else _WP_ALIGNED
            blk_a[ob] = start // WP_TN
            blk_h[ob] = (start + WP_TN) // WP_SHIFT
            blk_b[ob] = -1

    def fill_perm(col_out, col_src):
        start = col_src - WP_SHIFT
        assert start % WP_TN == 0
        for t, m in enumerate((_WP_PERM0, _WP_PERM1)):
            ob = col_out // WP_TN + t
            mode[ob] = m
            blk_a[ob] = start // WP_TN
            blk_b[ob] = start // WP_TN + 1
            blk_h[ob] = (start + 2 * WP_TN) // WP_SHIFT
    fill(COL_FQ, _O_FQ, 3072, False)
    fill_perm(COL_SQ, _O_SQ)
    fill(COL_GF, _O_GF, 1024, True)
    fill_perm(COL_GS, _O_GS)
    fill(COL_MF, _O_MF, 2048, True)
    fill(COL_MS, _O_MS, 2048, True)
    fill(COL_SK, _O_SK, 512, True)
    for ob in range(nb):
        if blk_b[ob] < 0:
            blk_b[ob] = blk_b[ob - 1] if ob else 0
    return blk_a, blk_b, blk_h, mode


def _wprep_kernel(a_tab, b_tab, h_tab, mode_ref, a_ref, b_ref, h_ref, c_ref, wada_ref, bada_ref,
                  o_ref, ada_ref):
    mode = mode_ref[pl.program_id(0)]

    @pl.when(pl.program_id(0) < ADA_STEPS)
    def _():
        c8 = jnp.broadcast_to(c_ref[...], (8, D_MODEL))
        r = jnp.dot(c8, wada_ref[...], preferred_element_type=F32)
        ada_ref[...] = r[0:1, :] + bada_ref[...]

    def emit(rows_of_chunk):
        for c in range(D_MODEL // WP_TC):
            cols = slice(c * WP_TC, (c + 1) * WP_TC)
            o_ref[cols, :] = rows_of_chunk(cols).T.astype(BF16)

    @pl.when(mode == _WP_ALIGNED)
    def _():
        emit(lambda cols: a_ref[:, cols])

    @pl.when(mode == _WP_SHIFTED)
    def _():
        emit(lambda cols: jnp.concatenate([a_ref[WP_SHIFT:, cols], h_ref[:, cols]], axis=0))

    def perm_rows(cols, t):
        parts = []
        for g in (2 * t, 2 * t + 1):
            for h in range(SWA_KV_HEADS):
                r0 = WP_SHIFT + (h * SWA_GROUP + g) * SWA_HEAD_DIM
                r1 = r0 + SWA_HEAD_DIM
                if r1 <= WP_TN:
                    parts.append(a_ref[r0:r1, cols])
                elif r0 >= WP_TN:
                    if r1 <= 2 * WP_TN:
                        parts.append(b_ref[r0 - WP_TN:r1 - WP_TN, cols])
                    else:
                        parts.append(jnp.concatenate([b_ref[r0 - WP_TN:, cols], h_ref[:, cols]], axis=0))
                else:
                    parts.append(jnp.concatenate([a_ref[r0:, cols], b_ref[:r1 - WP_TN, cols]], axis=0))
        return jnp.concatenate(parts, axis=0)

    @pl.when(mode == _WP_PERM0)
    def _():
        emit(lambda cols: perm_rows(cols, 0))

    @pl.when(mode == _WP_PERM1)
    def _():
        emit(lambda cols: perm_rows(cols, 1))


def _wprep(wt, c, w_ada, b_ada):
    tabs = _wprep_tables()
    n_ada = w_ada.shape[1]
    assert n_ada == ADA_STEPS * ADA_TN and ADA_STEPS <= PROJ_WIDTH // WP_TN
    ada_blk = lambda i, a, b, h, m: (0, jnp.minimum(i, ADA_STEPS - 1))
    grid_spec = pltpu.PrefetchScalarGridSpec(
        num_scalar_prefetch=4,
        grid=(PROJ_WIDTH // WP_TN,),
        in_specs=[pl.BlockSpec((WP_TN, D_MODEL), lambda i, a, b, h, m: (a[i], 0)),
                  pl.BlockSpec((WP_TN, D_MODEL), lambda i, a, b, h, m: (b[i], 0)),
                  pl.BlockSpec((WP_SHIFT, D_MODEL), lambda i, a, b, h, m: (h[i], 0)),
                  pl.BlockSpec((1, D_MODEL), lambda i, a, b, h, m: (0, 0)),
                  pl.BlockSpec((D_MODEL, ADA_TN), ada_blk),
                  pl.BlockSpec((1, ADA_TN), ada_blk)],
        out_specs=(pl.BlockSpec((D_MODEL, WP_TN), lambda i, a, b, h, m: (0, i)),
                   pl.BlockSpec((1, ADA_TN), ada_blk)),
    )
    return pl.pallas_call(
        _wprep_kernel,
        out_shape=(jax.ShapeDtypeStruct((D_MODEL, PROJ_WIDTH), BF16),
                   jax.ShapeDtypeStruct((1, n_ada), F32)),
        grid_spec=grid_spec,
        compiler_params=pltpu.CompilerParams(dimension_semantics=("arbitrary",),
                                             vmem_limit_bytes=VMEM_LIMIT),
        name="wprep",
    )(*[jnp.asarray(t) for t in tabs], wt, wt, wt, c, w_ada, b_ada)


PROJ_TM = 1024
PROJ_TN = 1536
PROJ_NJ = PROJ_WIDTH // PROJ_TN
PROJ_LN_ROWS = 160
assert (PROJ_NJ * PROJ_LN_ROWS >= PROJ_TM and PROJ_LN_ROWS % BF16_SUBLANES == 0
        and (PROJ_TM - PROJ_LN_ROWS) % BF16_SUBLANES == 0)


def _proj_kernel(x_ref, shift_ref, scale_ref, w_ref, wf_ref, cs_ref, o_ref, flog_ref, h0_scr, h1_scr):
    r = pl.program_id(0)
    j = pl.program_id(1)

    rows = pl.ds(pl.multiple_of(jnp.minimum(j * PROJ_LN_ROWS, PROJ_TM - PROJ_LN_ROWS), BF16_SUBLANES),
                 PROJ_LN_ROWS)

    def layer_norm_slice(h_dst):
        x = x_ref[rows, :]
        mu = jnp.mean(x, axis=-1, keepdims=True)
        xc = x - mu
        var = jnp.mean(xc * xc, axis=-1, keepdims=True)
        h = xc * lax.rsqrt(var + LN_EPS) * (1.0 + scale_ref[...]) + shift_ref[...]
        h_dst[rows, :] = h.astype(BF16)

    def project(h_src):
        acc = jnp.dot(h_src[...], w_ref[...], preferred_element_type=F32)
        o_ref[...] = (acc * cs_ref[...]).astype(BF16)
        wf = jnp.concatenate([wf_ref[...], jnp.zeros((LANES - FOX_HEADS, D_MODEL), F32)], axis=0)
        flog_ref[rows, :] = lax.dot_general(h_src[rows, :], wf.astype(BF16), (((1,), (1,)), ((), ())),
                                            preferred_element_type=F32)

    @pl.when(r == 0)
    def _():
        layer_norm_slice(h0_scr)

    @pl.when((r > 0) & (r % 2 == 1))
    def _():
        layer_norm_slice(h1_scr)
        project(h0_scr)

    @pl.when((r > 0) & (r % 2 == 0))
    def _():
        layer_norm_slice(h0_scr)
        project(h1_scr)


def _proj(x2, ada, w_main, wt, col_scale):
    s = x2.shape[0]
    nb = s // PROJ_TM
    prev = lambda r: jnp.maximum(r - 1, 0)
    col = lambda r, j: jnp.where(r == 0, 0, j)
    return pl.pallas_call(
        _proj_kernel,
        out_shape=(jax.ShapeDtypeStruct((s, PROJ_WIDTH), BF16),
                   jax.ShapeDtypeStruct((s, LANES), F32)),
        grid=(nb + 1, PROJ_NJ),
        in_specs=[pl.BlockSpec((PROJ_TM, D_MODEL), lambda r, j: (jnp.minimum(r, nb - 1), 0)),
                  pl.BlockSpec((1, D_MODEL), lambda r, j: (0, 0)),
                  pl.BlockSpec((1, D_MODEL), lambda r, j: (0, 1)),
                  pl.BlockSpec((D_MODEL, PROJ_TN), lambda r, j: (0, col(r, j))),
                  pl.BlockSpec((FOX_HEADS, D_MODEL), lambda r, j: (_O_FLOG // FOX_HEADS, 0)),
                  pl.BlockSpec((1, PROJ_TN), lambda r, j: (0, col(r, j)))],
        out_specs=(pl.BlockSpec((PROJ_TM, PROJ_TN), lambda r, j: (prev(r), col(r, j))),
                   pl.BlockSpec((PROJ_TM, LANES), lambda r, j: (prev(r), 0))),
        scratch_shapes=[pltpu.VMEM((PROJ_TM, D_MODEL), BF16), pltpu.VMEM((PROJ_TM, D_MODEL), BF16)],
        compiler_params=pltpu.CompilerParams(dimension_semantics=("arbitrary", "arbitrary"),
                                             vmem_limit_bytes=VMEM_LIMIT),
        name="proj",
    )(x2, ada, ada, w_main, wt, col_scale)


def _cum_kernel(flog_ref, bf_ref, f_ref):
    s = flog_ref.shape[0]
    lf = jax.nn.log_sigmoid(flog_ref[...] + bf_ref[...])
    acc = lf.T[0:FOX_HEADS, :]
    lane = lax.broadcasted_iota(jnp.int32, acc.shape, 1)
    sh = 1
    while sh < s:
        rolled = pltpu.roll(acc, sh, axis=1)
        acc = acc + jnp.where(lane >= sh, rolled, 0.0)
        sh *= 2
    for h in range(FOX_HEADS):
        f_ref[h] = acc[h:h + 1, :]


def _cum(flog, bf_pad):
    s = flog.shape[0]
    return pl.pallas_call(
        _cum_kernel,
        out_shape=jax.ShapeDtypeStruct((FOX_HEADS, 1, s), F32),
        in_specs=[pl.BlockSpec((s, LANES), lambda: (0, 0)),
                  pl.BlockSpec((1, LANES), lambda: (0, 0))],
        out_specs=pl.BlockSpec((FOX_HEADS, 1, s), lambda: (0, 0, 0)),
        compiler_params=pltpu.CompilerParams(vmem_limit_bytes=VMEM_LIMIT),
        name="cum",
    )(flog, bf_pad)


FOX_TK = 512
FOX_TQ = 2 * FOX_TK
FOX_SKIP_LOG2 = 152.0


def _attn_kernel(sink_ref, q_ref, k_ref, v_ref, f_ref, sq_ref, skp_ref, skc_ref, svp_ref, svc_ref,
                 wbf32_ref, wbs32_ref, wo32_ref,
                 o_ref, os_ref, wbf_ref, wbs_ref, wo_ref,
                 sa_scr, sb_scr, m_scr, acc_scr, kn_scr, bias_scr):
    i = pl.program_id(1)
    n_blk = pl.program_id(0) * pl.num_programs(1) + i
    tq, tk = FOX_TQ, FOX_TK
    s_len = k_ref.shape[0]

    @pl.when(n_blk <= 1)
    def _():
        _swa_tables(n_blk, sink_ref, bias_scr)

    @pl.when(i == 0)
    def _():
        kn_scr[...] = jnp.zeros(kn_scr.shape, F32)

    q0 = pl.multiple_of(i * tq, tq)
    f_q = f_ref[0, :, pl.ds(q0, tq)]
    f_base = jnp.max(f_q, axis=1, keepdims=True)
    ones = jnp.ones((tk, LANES), BF16)
    top, bot = slice(0, tk), slice(tk, tq)

    m_scr[...] = jnp.full(m_scr.shape, NEG_INF, F32)
    acc_scr[...] = jnp.zeros(acc_scr.shape, F32)

    def scores(k0, dst, rows=slice(0, FOX_TQ)):
        k_t = k_ref[pl.ds(k0, tk), :]
        s = lax.dot_general(q_ref[rows, :], k_t, (((1,), (1,)), ((), ())), preferred_element_type=F32)
        bias = (f_base - f_ref[0, :, pl.ds(k0, tk)]) * LOG2E
        dst[rows, :] = s + bias

    def softmax_pv(src, k0, rows=slice(0, FOX_TQ), causal=False):
        s = src[rows, :]
        if causal:
            n = rows.stop - rows.start
            keep = (lax.broadcasted_iota(jnp.int32, (n, tk), 1)
                    <= lax.broadcasted_iota(jnp.int32, (n, tk), 0))
            s = jnp.where(keep, s, NEG_INF)
        m_prev = m_scr[rows, :]
        m_new = jnp.maximum(m_prev, jnp.max(s, axis=1, keepdims=True))
        alpha = jnp.exp2(m_prev - m_new)
        p = jnp.exp2(s - jnp.tile(m_new, (1, tk // LANES))).astype(BF16)
        v_aug = jnp.concatenate([v_ref[pl.ds(k0, tk), :], ones], axis=1)
        pv = jnp.dot(p, v_aug, preferred_element_type=F32)
        acc_scr[rows, :] = acc_scr[rows, :] * jnp.tile(alpha, (1, 2)) + pv
        m_scr[rows, :] = m_new

    kd = pl.multiple_of(q0 + tk, tk)
    scores(q0, sa_scr)
    scores(kd, sb_scr, bot)
    softmax_pv(sa_scr, q0, top, causal=True)
    softmax_pv(sa_scr, q0, bot)
    softmax_pv(sb_scr, kd, bot, causal=True)
    scores(pl.multiple_of(jnp.maximum(q0 - tq, 0), tq), sa_scr)

    _swa_block(sq_ref, skp_ref, skc_ref, svp_ref, svc_ref, os_ref, bias_scr)

    wbf_ref[...] = wbf32_ref[...].astype(BF16)
    wbs_ref[...] = wbs32_ref[...].astype(BF16)
    wo_ref[...] = wo32_ref[...].astype(BF16)

    qf = q_ref[...].astype(F32)
    qn2 = jnp.max(jnp.sum(qf * qf, axis=1, keepdims=True), axis=0, keepdims=True)
    m_low = jnp.min(jnp.min(m_scr[...], axis=0, keepdims=True), axis=1, keepdims=True)
    kn2 = kn_scr[0:1, 0:1]
    thr = m_low - FOX_SKIP_LOG2 - jnp.sqrt(qn2 * kn2)
    kf = k_ref[pl.ds(q0, tq), :].astype(F32)
    kn2_tile = jnp.max(jnp.sum(kf * kf, axis=1, keepdims=True), axis=0, keepdims=True)
    kn_scr[...] = jnp.broadcast_to(jnp.maximum(kn2, kn2_tile), kn_scr.shape)
    pos = lax.broadcasted_iota(jnp.int32, (1, s_len), 1)
    live = ((f_base - f_ref[0]) * LOG2E >= thr) & (pos < q0)
    n_live = jnp.sum(live.astype(jnp.int32), axis=1, keepdims=True)
    n_pairs = (n_live[0, 0] + (tq - 1)) // tq

    def pair(t):
        ka = pl.multiple_of(q0 - (t + 1) * tq, tq)
        kb = pl.multiple_of(ka + tk, tk)
        scores(kb, sb_scr)
        softmax_pv(sa_scr, ka)
        scores(pl.multiple_of(jnp.maximum(ka - tq, 0), tq), sa_scr)
        softmax_pv(sb_scr, kb)

    def two_pairs(u, carry):
        pair(2 * u)
        pair(2 * u + 1)
        return carry

    lax.fori_loop(0, n_pairs // 2, two_pairs, 0)

    @pl.when(n_pairs % 2 == 1)
    def _():
        pair(n_pairs - 1)

    acc = acc_scr[...]
    o_ref[...] = (acc[:, :FOX_HEAD_DIM] / acc[:, FOX_HEAD_DIM:]).astype(BF16)


def _attn(proj, f3, sinks, w_br_fox, w_br_swa, w_out):
    s = proj.shape[0]
    tq, tk, w = FOX_TQ, FOX_TK, WINDOW
    nq = s // tq
    nsteps = FOX_HEADS * nq
    assert nsteps == s // w
    kcol = COL_SK // SWA_KV_WIDTH
    vcol = COL_SV // SWA_KV_WIDTH
    blk = lambda h, i: h * nq + i
    prv = lambda h, i: jnp.maximum(h * nq + i - 1, 0)
    rb = FOX_WIDTH // nsteps
    ro = D_MODEL // nsteps
    per64 = SWA_HEAD_DIM // rb
    assert rb % BF16_SUBLANES == 0 and SWA_HEAD_DIM % rb == 0

    def swa_src(h, i):
        n = blk(h, i)
        g, hk = (n // per64) // SWA_KV_HEADS, (n // per64) % SWA_KV_HEADS
        return (hk * SWA_GROUP + g) * per64 + n % per64
    grid_spec = pltpu.PrefetchScalarGridSpec(
        num_scalar_prefetch=1,
        grid=(FOX_HEADS, nq),
        in_specs=[pl.BlockSpec((tq, LANES), lambda h, i, sk: (i, COL_FQ // LANES + h)),
                  pl.BlockSpec((s, LANES), lambda h, i, sk: (0, COL_FK // LANES + h)),
                  pl.BlockSpec((s, LANES), lambda h, i, sk: (0, COL_FV // LANES + h)),
                  pl.BlockSpec((1, 1, s), lambda h, i, sk: (h, 0, 0)),
                  pl.BlockSpec((w, SWA_WIDTH), lambda h, i, sk: (blk(h, i), COL_SQ // SWA_WIDTH)),
                  pl.BlockSpec((w, SWA_KV_WIDTH), lambda h, i, sk: (prv(h, i), kcol)),
                  pl.BlockSpec((w, SWA_KV_WIDTH), lambda h, i, sk: (blk(h, i), kcol)),
                  pl.BlockSpec((w, SWA_KV_WIDTH), lambda h, i, sk: (prv(h, i), vcol)),
                  pl.BlockSpec((w, SWA_KV_WIDTH), lambda h, i, sk: (blk(h, i), vcol)),
                  pl.BlockSpec((rb, D_MODEL), lambda h, i, sk: (blk(h, i), 0)),
                  pl.BlockSpec((rb, D_MODEL), lambda h, i, sk: (swa_src(h, i), 0)),
                  pl.BlockSpec((ro, D_MODEL), lambda h, i, sk: (blk(h, i), 0))],
        out_specs=(pl.BlockSpec((tq, LANES), lambda h, i, sk: (i, h)),
                   pl.BlockSpec((w, SWA_WIDTH), lambda h, i, sk: (blk(h, i), 0)),
                   pl.BlockSpec((rb, D_MODEL), lambda h, i, sk: (blk(h, i), 0)),
                   pl.BlockSpec((rb, D_MODEL), lambda h, i, sk: (blk(h, i), 0)),
                   pl.BlockSpec((ro, D_MODEL), lambda h, i, sk: (blk(h, i), 0))),
        scratch_shapes=[pltpu.VMEM((tq, tk), F32), pltpu.VMEM((tq, tk), F32),
                        pltpu.VMEM((tq, LANES), F32), pltpu.VMEM((tq, 2 * LANES), F32),
                        pltpu.VMEM((8, LANES), F32),
                        pltpu.VMEM((SWA_Q_HEADS * w, 2 * w), F32)],
    )
    return pl.pallas_call(
        _attn_kernel,
        out_shape=(jax.ShapeDtypeStruct((s, FOX_WIDTH), BF16),
                   jax.ShapeDtypeStruct((s, SWA_WIDTH), BF16),
                   jax.ShapeDtypeStruct((FOX_WIDTH, D_MODEL), BF16),
                   jax.ShapeDtypeStruct((SWA_WIDTH, D_MODEL), BF16),
                   jax.ShapeDtypeStruct((D_MODEL, D_MODEL), BF16)),
        grid_spec=grid_spec,
        compiler_params=pltpu.CompilerParams(dimension_semantics=("arbitrary", "arbitrary"),
                                             vmem_limit_bytes=VMEM_LIMIT),
        name="attn",
    )(sinks, proj, proj, proj, f3, proj, proj, proj, proj, proj, w_br_fox, w_br_swa, w_out)


def _swa_tables(n, sink_ref, bias_scr):
    w = WINDOW
    row = lax.broadcasted_iota(jnp.int32, (w, 2 * w), 0)
    col = lax.broadcasted_iota(jnp.int32, (w, 2 * w), 1)
    dist = row - col + w
    valid = (dist >= 0) & (dist < w) & ((col >= w) | (n > 0))
    distf = dist.astype(F32)
    for hq in range(SWA_Q_HEADS):
        slope = 2.0 ** (-8.0 * (hq + 1.0) / SWA_Q_HEADS)
        band = jnp.where(valid, (-slope * LOG2E) * distf, NEG_INF)
        bias_scr[hq * w:(hq + 1) * w, :] = jnp.where(col == 0, sink_ref[hq] * LOG2E, band)


def _swa_block(q_ref, kp_ref, kc_ref, vp_ref, vc_ref, o_ref, bias_scr):
    w = WINDOW
    gw = SWA_GROUP * w
    first = lax.broadcasted_iota(jnp.int32, (w, SWA_KV_WIDTH), 0) == 0
    kp = jnp.where(first, jnp.zeros_like(kp_ref[...]), kp_ref[...])
    vp = jnp.where(first, jnp.zeros_like(vp_ref[...]), vp_ref[...])
    kk = jnp.concatenate([kp, kc_ref[...]], axis=0)
    vv = jnp.concatenate([vp, vc_ref[...]], axis=0)
    lane_head = lax.broadcasted_iota(jnp.int32, (w, 2 * LANES), 1) // SWA_HEAD_DIM
    key_head = lax.broadcasted_iota(jnp.int32, (2 * w, 2 * LANES), 1) // SWA_HEAD_DIM
    parts = []
    for h in range(SWA_KV_HEADS):
        for g in range(SWA_GROUP):
            qg = q_ref[:, g * 256:(g + 1) * 256]
            parts.append(jnp.where(lane_head == h, qg, jnp.zeros_like(qg)))
    qs = jnp.concatenate(parts, axis=0)
    s = lax.dot_general(qs, kk, (((1,), (1,)), ((), ())), preferred_element_type=F32)
    s = s + bias_scr[...]
    m = jnp.max(s, axis=1, keepdims=True)
    p = jnp.exp2(s - m)
    inv = 1.0 / jnp.sum(p, axis=1, keepdims=True)
    pb = p.astype(BF16)
    p_all = jnp.concatenate([pb[h * gw:(h + 1) * gw, :] for h in range(SWA_KV_HEADS)], axis=1)
    v_blk = jnp.concatenate([jnp.where(key_head == h, vv, jnp.zeros_like(vv))
                             for h in range(SWA_KV_HEADS)], axis=0)
    o = jnp.dot(p_all, v_blk, preferred_element_type=F32)
    for g in range(SWA_GROUP):
        inv_g = jnp.broadcast_to(inv[g * w:(g + 1) * w, :], (w, 2 * LANES))
        for h in range(1, SWA_KV_HEADS):
            r0 = h * gw + g * w
            inv_g = jnp.where(lane_head == h, jnp.broadcast_to(inv[r0:r0 + w, :], (w, 2 * LANES)), inv_g)
        o_ref[:, g * 256:(g + 1) * 256] = (o[g * w:(g + 1) * w, :] * inv_g).astype(BF16)


OUT_TM = 256


def _out_kernel(of_ref, gf_ref, os_ref, gs_ref, mf_ref, ms_ref, x_ref, gate_ref,
                wbf_ref, wbs_ref, wo_ref, lng_ref, lnb_ref, o_ref):
    af = (of_ref[...].astype(F32) * jax.nn.silu(gf_ref[...].astype(F32))).astype(BF16)
    yf = jnp.dot(af, wbf_ref[...], preferred_element_type=F32)
    a_s = (os_ref[...].astype(F32) * jax.nn.silu(gs_ref[...].astype(F32))).astype(BF16)
    ys = jnp.dot(a_s, wbs_ref[...], preferred_element_type=F32)
    merged = (jax.nn.sigmoid(mf_ref[...].astype(F32)) * yf
              + jax.nn.sigmoid(ms_ref[...].astype(F32)) * ys)
    sub = jnp.dot(merged.astype(BF16), wo_ref[...], preferred_element_type=F32)
    z = DEEPNORM_ALPHA * x_ref[...] + gate_ref[...] * sub
    mu = jnp.mean(z, axis=-1, keepdims=True)
    zc = z - mu
    var = jnp.mean(zc * zc, axis=-1, keepdims=True)
    o_ref[...] = zc * lax.rsqrt(var + LN_EPS) * lng_ref[...] + lnb_ref[...]


def _out(proj, o_fox, o_swa, x2, ada, wbf, wbs, wo, ln_g, ln_b):
    s = x2.shape[0]
    tm = OUT_TM
    const = lambda i: (0, 0)
    return pl.pallas_call(
        _out_kernel,
        out_shape=jax.ShapeDtypeStruct((s, D_MODEL), F32),
        grid=(s // tm,),
        in_specs=[pl.BlockSpec((tm, FOX_WIDTH), lambda i: (i, 0)),
                  pl.BlockSpec((tm, FOX_WIDTH), lambda i: (i, COL_GF // FOX_WIDTH)),
                  pl.BlockSpec((tm, SWA_WIDTH), lambda i: (i, 0)),
                  pl.BlockSpec((tm, SWA_WIDTH), lambda i: (i, COL_GS // SWA_WIDTH)),
                  pl.BlockSpec((tm, D_MODEL), lambda i: (i, COL_MF // D_MODEL)),
                  pl.BlockSpec((tm, D_MODEL), lambda i: (i, COL_MS // D_MODEL)),
                  pl.BlockSpec((tm, D_MODEL), lambda i: (i, 0)),
                  pl.BlockSpec((1, D_MODEL), lambda i: (0, 2)),
                  pl.BlockSpec((FOX_WIDTH, D_MODEL), const),
                  pl.BlockSpec((SWA_WIDTH, D_MODEL), const),
                  pl.BlockSpec((D_MODEL, D_MODEL), const),
                  pl.BlockSpec((1, D_MODEL), const),
                  pl.BlockSpec((1, D_MODEL), const)],
        out_specs=pl.BlockSpec((tm, D_MODEL), lambda i: (i, 0)),
        compiler_params=pltpu.CompilerParams(dimension_semantics=("arbitrary",),
                                             vmem_limit_bytes=VMEM_LIMIT),
        name="out",
    )(o_fox, proj, o_swa, proj, proj, proj, x2, ada, wbf, wbs, wo, ln_g, ln_b)


def kernel(x, c, w_ada, b_ada, w_in, b_f, attn_sinks, w_br_fox, w_br_swa, w_out, ln_g, ln_b):
    b, s, d = x.shape
    assert (b, s, d) == (1, SEQ, D_MODEL) and w_in.shape[0] == DEPTH
    x2 = x.reshape(s, d)

    wt = jnp.swapaxes(w_in, 1, 2)[0]
    w_main, ada = _wprep(wt, c, w_ada[0], b_ada[0].reshape(1, -1))
    bf_pad = jnp.pad(b_f[0], (0, LANES - FOX_HEADS)).reshape(1, LANES)
    col_scale = jnp.asarray(_proj_col_scale())

    proj, flog = _proj(x2, ada, w_main, wt, col_scale)
    f_cum = _cum(flog, bf_pad)
    o_fox, o_swa, wbf, wbs, wo = _attn(proj, f_cum, attn_sinks[0], w_br_fox[0], w_br_swa[0], w_out[0])
    out = _out(proj, o_fox, o_swa, x2, ada, wbf, wbs, wo,
               ln_g[0].reshape(1, d), ln_b[0].reshape(1, d))
    return out.reshape(b, s, d)
```

```python
import math

import numpy as np
import jax
import jax.numpy as jnp
from jax import lax
from jax.experimental import pallas as pl
from jax.experimental.pallas import tpu as pltpu

F32 = jnp.float32
BF16 = jnp.bfloat16

D_MODEL = 2048
SEQ = 8192
FOX_HEADS = 8
FOX_HEAD_DIM = 128
FOX_WIDTH = FOX_HEADS * FOX_HEAD_DIM
SWA_Q_HEADS = 16
SWA_KV_HEADS = 4
SWA_HEAD_DIM = 64
SWA_GROUP = SWA_Q_HEADS // SWA_KV_HEADS
SWA_WIDTH = SWA_Q_HEADS * SWA_HEAD_DIM
SWA_KV_WIDTH = SWA_KV_HEADS * SWA_HEAD_DIM
WINDOW = 128
LN_EPS = 1e-5
NEG_INF = -1e30
DEPTH = 1
DEEPNORM_ALPHA = (2.0 * DEPTH) ** 0.25
LOG2E = math.log2(math.e)

LANES = 128
BF16_SUBLANES = 16
VMEM_LIMIT = 56 * 1024 * 1024

COL_FQ = 0
COL_FK = 1024
COL_FV = 2048
COL_SQ = 3072
COL_GF = 4096
COL_GS = 5120
COL_MF = 6144
COL_MS = 8192
COL_SK = 10240
COL_SV = 10496
PROJ_WIDTH = 10752

_O_FQ, _O_FK, _O_FV, _O_FLOG = 0, 1024, 2048, 3072
_O_SQ, _O_SK, _O_SV = 3080, 4104, 4360
_O_GF, _O_GS, _O_MF, _O_MS = 4616, 5640, 6664, 8712


def _proj_col_scale():
    s = np.ones((1, PROJ_WIDTH), np.float32)
    s[0, COL_FQ:COL_FQ + 1024] = FOX_HEAD_DIM ** -0.5 * LOG2E
    s[0, COL_SQ:COL_SQ + 1024] = SWA_HEAD_DIM ** -0.5 * LOG2E
    return s


WP_TN = 512
WP_TC = 256
WP_SHIFT = FOX_HEADS
_WP_ALIGNED, _WP_SHIFTED, _WP_PERM0, _WP_PERM1, _WP_NONE = 0, 1, 2, 3, 4
ADA_TN = 512
ADA_STEPS = 3 * D_MODEL // ADA_TN
WP_LN_FIRST = 2 * D_MODEL // ADA_TN
PROJ_TM = 1024


def _wprep_tables():
    nb = PROJ_WIDTH // WP_TN
    blk_a = np.zeros((nb,), np.int32)
    blk_b = np.zeros((nb,), np.int32)
    blk_h = np.zeros((nb,), np.int32)
    mode = np.zeros((nb,), np.int32)

    def fill(col_out, col_src, width, shifted):
        for t in range(width // WP_TN):
            ob = col_out // WP_TN + t
            start = col_src + t * WP_TN - (WP_SHIFT if shifted else 0)
            assert start % WP_TN == 0
            mode[ob] = _WP_SHIFTED if shifted else _WP_ALIGNED
            blk_a[ob] = start // WP_TN
            blk_h[ob] = (start + WP_TN) // WP_SHIFT
            blk_b[ob] = -1

    def fill_perm(col_out, col_src):
        start = col_src - WP_SHIFT
        assert start % WP_TN == 0
        for t, m in enumerate((_WP_PERM0, _WP_PERM1)):
            ob = col_out // WP_TN + t
            mode[ob] = m
            blk_a[ob] = start // WP_TN
            blk_b[ob] = start // WP_TN + 1
            blk_h[ob] = (start + 2 * WP_TN) // WP_SHIFT
    fill(COL_FQ, _O_FQ, 3072, False)
    fill_perm(COL_SQ, _O_SQ)
    fill(COL_GF, _O_GF, 1024, True)
    fill_perm(COL_GS, _O_GS)
    fill(COL_MF, _O_MF, 2048, True)
    fill(COL_MS, _O_MS, 2048, True)
    fill(COL_SK, _O_SK, 512, True)
    for ob in range(nb):
        if blk_b[ob] < 0:
            blk_b[ob] = blk_b[ob - 1] if ob else 0
    return blk_a, blk_b, blk_h, mode


def _wprep_kernel(a_tab, b_tab, h_tab, mode_ref, a_ref, b_ref, h_ref, c_ref, wada_ref, bada_ref,
                  x_ref, wf_ref, cs_ref, o_ref, ada_ref, p_ref, flog_ref, ada_scr, h_scr):
    s = pl.program_id(0)
    mode = mode_ref[s]

    @pl.when(s < ADA_STEPS)
    def _():
        c8 = jnp.broadcast_to(c_ref[...], (8, D_MODEL))
        r = jnp.dot(c8, wada_ref[...], preferred_element_type=F32)
        val = r[0:1, :] + bada_ref[...]
        ada_ref[...] = val
        ada_scr[pl.ds(s, 1), :] = val

    @pl.when((s >= WP_LN_FIRST) & (s < ADA_STEPS))
    def _():
        ln_rows = PROJ_TM // (ADA_STEPS - WP_LN_FIRST)
        rows = pl.ds(pl.multiple_of((s - WP_LN_FIRST) * ln_rows, ln_rows), ln_rows)
        per = D_MODEL // ADA_TN
        shift = jnp.concatenate([ada_scr[k:k + 1, :] for k in range(per)], axis=1)
        scale = jnp.concatenate([ada_scr[k:k + 1, :] for k in range(per, 2 * per)], axis=1)
        x = x_ref[rows, :]
        mu = jnp.mean(x, axis=-1, keepdims=True)
        xc = x - mu
        var = jnp.mean(xc * xc, axis=-1, keepdims=True)
        h = xc * lax.rsqrt(var + LN_EPS) * (1.0 + scale) + shift
        h_scr[rows, :] = h.astype(BF16)

    def emit(rows_of_chunk):
        for c in range(D_MODEL // WP_TC):
            cols = slice(c * WP_TC, (c + 1) * WP_TC)
            o_ref[cols, :] = rows_of_chunk(cols).T.astype(BF16)
        acc = jnp.dot(h_scr[...], o_ref[...], preferred_element_type=F32)
        p_ref[...] = (acc * cs_ref[...]).astype(BF16)

    @pl.when(s == pl.num_programs(0) - 1)
    def _():
        wf = jnp.concatenate([wf_ref[...], jnp.zeros((LANES - FOX_HEADS, D_MODEL), F32)], axis=0)
        flog_ref[...] = lax.dot_general(h_scr[...], wf.astype(BF16), (((1,), (1,)), ((), ())),
                                        preferred_element_type=F32)

    @pl.when(mode == _WP_ALIGNED)
    def _():
        emit(lambda cols: a_ref[:, cols])

    @pl.when(mode == _WP_SHIFTED)
    def _():
        emit(lambda cols: jnp.concatenate([a_ref[WP_SHIFT:, cols], h_ref[:, cols]], axis=0))

    def perm_rows(cols, t):
        parts = []
        for g in (2 * t, 2 * t + 1):
            for h in range(SWA_KV_HEADS):
                r0 = WP_SHIFT + (h * SWA_GROUP + g) * SWA_HEAD_DIM
                r1 = r0 + SWA_HEAD_DIM
                if r1 <= WP_TN:
                    parts.append(a_ref[r0:r1, cols])
                elif r0 >= WP_TN:
                    if r1 <= 2 * WP_TN:
                        parts.append(b_ref[r0 - WP_TN:r1 - WP_TN, cols])
                    else:
                        parts.append(jnp.concatenate([b_ref[r0 - WP_TN:, cols], h_ref[:, cols]], axis=0))
                else:
                    parts.append(jnp.concatenate([a_ref[r0:, cols], b_ref[:r1 - WP_TN, cols]], axis=0))
        return jnp.concatenate(parts, axis=0)

    @pl.when(mode == _WP_PERM0)
    def _():
        emit(lambda cols: perm_rows(cols, 0))

    @pl.when(mode == _WP_PERM1)
    def _():
        emit(lambda cols: perm_rows(cols, 1))


def _wprep(wt, x2, c, w_ada, b_ada, col_scale):
    blk_a, blk_b, blk_h, mode = _wprep_tables()
    n_ada = w_ada.shape[1]
    assert n_ada == ADA_STEPS * ADA_TN and WP_LN_FIRST < ADA_STEPS
    nb = PROJ_WIDTH // WP_TN

    def per_step(t, lead):
        return jnp.asarray(np.concatenate([np.full((ADA_STEPS,), lead, np.int32), t]))
    tabs = (per_step(blk_a, blk_a[0]), per_step(blk_b, blk_b[0]), per_step(blk_h, blk_h[0]),
            per_step(mode, _WP_NONE))
    ada_blk = lambda i, a, b, h, m: (0, jnp.minimum(i, ADA_STEPS - 1))
    w_blk = lambda i, a, b, h, m: (0, jnp.maximum(i - ADA_STEPS, 0))
    grid_spec = pltpu.PrefetchScalarGridSpec(
        num_scalar_prefetch=4,
        grid=(ADA_STEPS + nb,),
        in_specs=[pl.BlockSpec((WP_TN, D_MODEL), lambda i, a, b, h, m: (a[i], 0)),
                  pl.BlockSpec((WP_TN, D_MODEL), lambda i, a, b, h, m: (b[i], 0)),
                  pl.BlockSpec((WP_SHIFT, D_MODEL), lambda i, a, b, h, m: (h[i], 0)),
                  pl.BlockSpec((1, D_MODEL), lambda i, a, b, h, m: (0, 0)),
                  pl.BlockSpec((D_MODEL, ADA_TN), ada_blk),
                  pl.BlockSpec((1, ADA_TN), ada_blk),
                  pl.BlockSpec((PROJ_TM, D_MODEL), lambda i, a, b, h, m: (0, 0), pipeline_mode=pl.Buffered(1)),
                  pl.BlockSpec((FOX_HEADS, D_MODEL), lambda i, a, b, h, m: (_O_FLOG // FOX_HEADS, 0)),
                  pl.BlockSpec((1, WP_TN), w_blk)],
        out_specs=(pl.BlockSpec((D_MODEL, WP_TN), w_blk),
                   pl.BlockSpec((1, ADA_TN), ada_blk),
                   pl.BlockSpec((PROJ_TM, WP_TN), w_blk),
                   pl.BlockSpec((PROJ_TM, LANES), lambda i, a, b, h, m: (0, 0))),
        scratch_shapes=[pltpu.VMEM((16, ADA_TN), F32), pltpu.VMEM((PROJ_TM, D_MODEL), BF16)],
    )
    return pl.pallas_call(
        _wprep_kernel,
        out_shape=(jax.ShapeDtypeStruct((D_MODEL, PROJ_WIDTH), BF16),
                   jax.ShapeDtypeStruct((1, n_ada), F32),
                   jax.ShapeDtypeStruct((x2.shape[0], PROJ_WIDTH), BF16),
                   jax.ShapeDtypeStruct((x2.shape[0], LANES), F32)),
        grid_spec=grid_spec,
        compiler_params=pltpu.CompilerParams(dimension_semantics=("arbitrary",),
                                             vmem_limit_bytes=VMEM_LIMIT),
        name="wprep",
    )(*tabs, wt, wt, wt, c, w_ada, b_ada, x2, wt, col_scale)


PROJ_TN = 1536
PROJ_NJ = PROJ_WIDTH // PROJ_TN
PROJ_LN_ROWS = 160
assert (PROJ_NJ * PROJ_LN_ROWS >= PROJ_TM and PROJ_LN_ROWS % BF16_SUBLANES == 0
        and (PROJ_TM - PROJ_LN_ROWS) % BF16_SUBLANES == 0)


def _proj_kernel(x_ref, shift_ref, scale_ref, w_ref, wf_ref, cs_ref, p0_ref, f0_ref, o_ref, flog_ref,
                 h0_scr, h1_scr):
    r = pl.program_id(0)
    j = pl.program_id(1)

    rows = pl.ds(pl.multiple_of(jnp.minimum(j * PROJ_LN_ROWS, PROJ_TM - PROJ_LN_ROWS), BF16_SUBLANES),
                 PROJ_LN_ROWS)

    def layer_norm_slice(h_dst):
        x = x_ref[rows, :]
        mu = jnp.mean(x, axis=-1, keepdims=True)
        xc = x - mu
        var = jnp.mean(xc * xc, axis=-1, keepdims=True)
        h = xc * lax.rsqrt(var + LN_EPS) * (1.0 + scale_ref[...]) + shift_ref[...]
        h_dst[rows, :] = h.astype(BF16)

    def project(h_src):
        acc = jnp.dot(h_src[...], w_ref[...], preferred_element_type=F32)
        o_ref[...] = (acc * cs_ref[...]).astype(BF16)
        wf = jnp.concatenate([wf_ref[...], jnp.zeros((LANES - FOX_HEADS, D_MODEL), F32)], axis=0)
        flog_ref[rows, :] = lax.dot_general(h_src[rows, :], wf.astype(BF16), (((1,), (1,)), ((), ())),
                                            preferred_element_type=F32)

    @pl.when(r == 0)
    def _():
        layer_norm_slice(h1_scr)

    @pl.when((r > 0) & (r % 2 == 1))
    def _():
        layer_norm_slice(h0_scr)
        project(h1_scr)

    @pl.when((r > 0) & (r % 2 == 0))
    def _():
        layer_norm_slice(h1_scr)
        project(h0_scr)


def _proj(x2, ada, w_main, wt, col_scale, proj0, flog0):
    s = x2.shape[0]
    nb = s // PROJ_TM
    cur = lambda r: jnp.maximum(r, 1)
    col = lambda r, j: jnp.where(r == 0, 0, j)
    return pl.pallas_call(
        _proj_kernel,
        out_shape=(jax.ShapeDtypeStruct((s, PROJ_WIDTH), BF16),
                   jax.ShapeDtypeStruct((s, LANES), F32)),
        grid=(nb, PROJ_NJ),
        in_specs=[pl.BlockSpec((PROJ_TM, D_MODEL), lambda r, j: (jnp.minimum(r + 1, nb - 1), 0)),
                  pl.BlockSpec((1, D_MODEL), lambda r, j: (0, 0)),
                  pl.BlockSpec((1, D_MODEL), lambda r, j: (0, 1)),
                  pl.BlockSpec((D_MODEL, PROJ_TN), lambda r, j: (0, col(r, j))),
                  pl.BlockSpec((FOX_HEADS, D_MODEL), lambda r, j: (_O_FLOG // FOX_HEADS, 0)),
                  pl.BlockSpec((1, PROJ_TN), lambda r, j: (0, col(r, j))),
                  pl.BlockSpec(memory_space=pl.ANY),
                  pl.BlockSpec(memory_space=pl.ANY)],
        out_specs=(pl.BlockSpec((PROJ_TM, PROJ_TN), lambda r, j: (cur(r), col(r, j))),
                   pl.BlockSpec((PROJ_TM, LANES), lambda r, j: (cur(r), 0))),
        scratch_shapes=[pltpu.VMEM((PROJ_TM, D_MODEL), BF16), pltpu.VMEM((PROJ_TM, D_MODEL), BF16)],
        input_output_aliases={6: 0, 7: 1},
        compiler_params=pltpu.CompilerParams(dimension_semantics=("arbitrary", "arbitrary"),
                                             vmem_limit_bytes=VMEM_LIMIT),
        name="proj",
    )(x2, ada, ada, w_main, wt, col_scale, proj0, flog0)


def _cum_kernel(flog_ref, bf_ref, f_ref):
    s = flog_ref.shape[0]
    lf = jax.nn.log_sigmoid(flog_ref[...] + bf_ref[...])
    acc = lf.T[0:FOX_HEADS, :]
    lane = lax.broadcasted_iota(jnp.int32, acc.shape, 1)
    sh = 1
    while sh < s:
        rolled = pltpu.roll(acc, sh, axis=1)
        acc = acc + jnp.where(lane >= sh, rolled, 0.0)
        sh *= 2
    for h in range(FOX_HEADS):
        f_ref[h] = acc[h:h + 1, :]


def _cum(flog, bf_pad):
    s = flog.shape[0]
    return pl.pallas_call(
        _cum_kernel,
        out_shape=jax.ShapeDtypeStruct((FOX_HEADS, 1, s), F32),
        in_specs=[pl.BlockSpec((s, LANES), lambda: (0, 0)),
                  pl.BlockSpec((1, LANES), lambda: (0, 0))],
        out_specs=pl.BlockSpec((FOX_HEADS, 1, s), lambda: (0, 0, 0)),
        compiler_params=pltpu.CompilerParams(vmem_limit_bytes=VMEM_LIMIT),
        name="cum",
    )(flog, bf_pad)


FOX_TK = 512
FOX_TQ = 2 * FOX_TK
FOX_SKIP_LOG2 = 152.0


def _attn_kernel(sink_ref, q_ref, k_ref, v_ref, f_ref, sq_ref, skp_ref, skc_ref, svp_ref, svc_ref,
                 wbf32_ref, wbs32_ref, wo32_ref,
                 o_ref, os_ref, wbf_ref, wbs_ref, wo_ref,
                 sa_scr, sb_scr, m_scr, acc_scr, kn_scr, bias_scr):
    i = pl.program_id(1)
    n_blk = pl.program_id(0) * pl.num_programs(1) + i
    tq, tk = FOX_TQ, FOX_TK
    s_len = k_ref.shape[0]

    @pl.when(n_blk <= 1)
    def _():
        _swa_tables(n_blk, sink_ref, bias_scr)

    @pl.when(i == 0)
    def _():
        kn_scr[...] = jnp.zeros(kn_scr.shape, F32)

    q0 = pl.multiple_of(i * tq, tq)
    f_q = f_ref[0, :, pl.ds(q0, tq)]
    f_base = jnp.max(f_q, axis=1, keepdims=True)
    ones = jnp.ones((tk, LANES), BF16)
    top, bot = slice(0, tk), slice(tk, tq)

    m_scr[...] = jnp.full(m_scr.shape, NEG_INF, F32)
    acc_scr[...] = jnp.zeros(acc_scr.shape, F32)

    def scores(k0, dst, rows=slice(0, FOX_TQ)):
        k_t = k_ref[pl.ds(k0, tk), :]
        s = lax.dot_general(q_ref[rows, :], k_t, (((1,), (1,)), ((), ())), preferred_element_type=F32)
        bias = (f_base - f_ref[0, :, pl.ds(k0, tk)]) * LOG2E
        dst[rows, :] = s + bias

    def softmax_pv(src, k0, rows=slice(0, FOX_TQ), causal=False):
        s = src[rows, :]
        if causal:
            n = rows.stop - rows.start
            keep = (lax.broadcasted_iota(jnp.int32, (n, tk), 1)
                    <= lax.broadcasted_iota(jnp.int32, (n, tk), 0))
            s = jnp.where(keep, s, NEG_INF)
        m_prev = m_scr[rows, :]
        m_new = jnp.maximum(m_prev, jnp.max(s, axis=1, keepdims=True))
        alpha = jnp.exp2(m_prev - m_new)
        p = jnp.exp2(s - jnp.tile(m_new, (1, tk // LANES))).astype(BF16)
        v_aug = jnp.concatenate([v_ref[pl.ds(k0, tk), :], ones], axis=1)
        pv = jnp.dot(p, v_aug, preferred_element_type=F32)
        acc_scr[rows, :] = acc_scr[rows, :] * jnp.tile(alpha, (1, 2)) + pv
        m_scr[rows, :] = m_new

    kd = pl.multiple_of(q0 + tk, tk)
    scores(q0, sa_scr)
    scores(kd, sb_scr, bot)
    softmax_pv(sa_scr, q0, top, causal=True)
    softmax_pv(sa_scr, q0, bot)
    softmax_pv(sb_scr, kd, bot, causal=True)
    scores(pl.multiple_of(jnp.maximum(q0 - tq, 0), tq), sa_scr)

    _swa_block(sq_ref, skp_ref, skc_ref, svp_ref, svc_ref, os_ref, bias_scr)

    wbf_ref[...] = wbf32_ref[...].astype(BF16)
    wbs_ref[...] = wbs32_ref[...].astype(BF16)
    wo_ref[...] = wo32_ref[...].astype(BF16)

    qf = q_ref[...].astype(F32)
    qn2 = jnp.max(jnp.sum(qf * qf, axis=1, keepdims=True), axis=0, keepdims=True)
    m_low = jnp.min(jnp.min(m_scr[...], axis=0, keepdims=True), axis=1, keepdims=True)
    kn2 = kn_scr[0:1, 0:1]
    thr = m_low - FOX_SKIP_LOG2 - jnp.sqrt(qn2 * kn2)
    kf = k_ref[pl.ds(q0, tq), :].astype(F32)
    kn2_tile = jnp.max(jnp.sum(kf * kf, axis=1, keepdims=True), axis=0, keepdims=True)
    kn_scr[...] = jnp.broadcast_to(jnp.maximum(kn2, kn2_tile), kn_scr.shape)
    pos = lax.broadcasted_iota(jnp.int32, (1, s_len), 1)
    live = ((f_base - f_ref[0]) * LOG2E >= thr) & (pos < q0)
    n_live = jnp.sum(live.astype(jnp.int32), axis=1, keepdims=True)
    n_pairs = (n_live[0, 0] + (tq - 1)) // tq

    def pair(t):
        ka = pl.multiple_of(q0 - (t + 1) * tq, tq)
        kb = pl.multiple_of(ka + tk, tk)
        scores(kb, sb_scr)
        softmax_pv(sa_scr, ka)
        scores(pl.multiple_of(jnp.maximum(ka - tq, 0), tq), sa_scr)
        softmax_pv(sb_scr, kb)

    def two_pairs(u, carry):
        pair(2 * u)
        pair(2 * u + 1)
        return carry

    lax.fori_loop(0, n_pairs // 2, two_pairs, 0)

    @pl.when(n_pairs % 2 == 1)
    def _():
        pair(n_pairs - 1)

    acc = acc_scr[...]
    o_ref[...] = (acc[:, :FOX_HEAD_DIM] / acc[:, FOX_HEAD_DIM:]).astype(BF16)


def _attn(proj, f3, sinks, w_br_fox, w_br_swa, w_out):
    s = proj.shape[0]
    tq, tk, w = FOX_TQ, FOX_TK, WINDOW
    nq = s // tq
    nsteps = FOX_HEADS * nq
    assert nsteps == s // w
    kcol = COL_SK // SWA_KV_WIDTH
    vcol = COL_SV // SWA_KV_WIDTH
    blk = lambda h, i: h * nq + i
    prv = lambda h, i: jnp.maximum(h * nq + i - 1, 0)
    rb = FOX_WIDTH // nsteps
    ro = D_MODEL // nsteps
    per64 = SWA_HEAD_DIM // rb
    assert rb % BF16_SUBLANES == 0 and SWA_HEAD_DIM % rb == 0

    def swa_src(h, i):
        n = blk(h, i)
        g, hk = (n // per64) // SWA_KV_HEADS, (n // per64) % SWA_KV_HEADS
        return (hk * SWA_GROUP + g) * per64 + n % per64
    grid_spec = pltpu.PrefetchScalarGridSpec(
        num_scalar_prefetch=1,
        grid=(FOX_HEADS, nq),
        in_specs=[pl.BlockSpec((tq, LANES), lambda h, i, sk: (i, COL_FQ // LANES + h)),
                  pl.BlockSpec((s, LANES), lambda h, i, sk: (0, COL_FK // LANES + h)),
                  pl.BlockSpec((s, LANES), lambda h, i, sk: (0, COL_FV // LANES + h)),
                  pl.BlockSpec((1, 1, s), lambda h, i, sk: (h, 0, 0)),
                  pl.BlockSpec((w, SWA_WIDTH), lambda h, i, sk: (blk(h, i), COL_SQ // SWA_WIDTH)),
                  pl.BlockSpec((w, SWA_KV_WIDTH), lambda h, i, sk: (prv(h, i), kcol)),
                  pl.BlockSpec((w, SWA_KV_WIDTH), lambda h, i, sk: (blk(h, i), kcol)),
                  pl.BlockSpec((w, SWA_KV_WIDTH), lambda h, i, sk: (prv(h, i), vcol)),
                  pl.BlockSpec((w, SWA_KV_WIDTH), lambda h, i, sk: (blk(h, i), vcol)),
                  pl.BlockSpec((rb, D_MODEL), lambda h, i, sk: (blk(h, i), 0)),
                  pl.BlockSpec((rb, D_MODEL), lambda h, i, sk: (swa_src(h, i), 0)),
                  pl.BlockSpec((ro, D_MODEL), lambda h, i, sk: (blk(h, i), 0))],
        out_specs=(pl.BlockSpec((tq, LANES), lambda h, i, sk: (i, h)),
                   pl.BlockSpec((w, SWA_WIDTH), lambda h, i, sk: (blk(h, i), 0)),
                   pl.BlockSpec((rb, D_MODEL), lambda h, i, sk: (blk(h, i), 0)),
                   pl.BlockSpec((rb, D_MODEL), lambda h, i, sk: (blk(h, i), 0)),
                   pl.BlockSpec((ro, D_MODEL), lambda h, i, sk: (blk(h, i), 0))),
        scratch_shapes=[pltpu.VMEM((tq, tk), F32), pltpu.VMEM((tq, tk), F32),
                        pltpu.VMEM((tq, LANES), F32), pltpu.VMEM((tq, 2 * LANES), F32),
                        pltpu.VMEM((8, LANES), F32),
                        pltpu.VMEM((SWA_Q_HEADS * w, 2 * w), F32)],
    )
    return pl.pallas_call(
        _attn_kernel,
        out_shape=(jax.ShapeDtypeStruct((s, FOX_WIDTH), BF16),
                   jax.ShapeDtypeStruct((s, SWA_WIDTH), BF16),
                   jax.ShapeDtypeStruct((FOX_WIDTH, D_MODEL), BF16),
                   jax.ShapeDtypeStruct((SWA_WIDTH, D_MODEL), BF16),
                   jax.ShapeDtypeStruct((D_MODEL, D_MODEL), BF16)),
        grid_spec=grid_spec,
        compiler_params=pltpu.CompilerParams(dimension_semantics=("arbitrary", "arbitrary"),
                                             vmem_limit_bytes=VMEM_LIMIT),
        name="attn",
    )(sinks, proj, proj, proj, f3, proj, proj, proj, proj, proj, w_br_fox, w_br_swa, w_out)


def _swa_tables(n, sink_ref, bias_scr):
    w = WINDOW
    row = lax.broadcasted_iota(jnp.int32, (w, 2 * w), 0)
    col = lax.broadcasted_iota(jnp.int32, (w, 2 * w), 1)
    dist = row - col + w
    valid = (dist >= 0) & (dist < w) & ((col >= w) | (n > 0))
    distf = dist.astype(F32)
    for hq in range(SWA_Q_HEADS):
        slope = 2.0 ** (-8.0 * (hq + 1.0) / SWA_Q_HEADS)
        band = jnp.where(valid, (-slope * LOG2E) * distf, NEG_INF)
        bias_scr[hq * w:(hq + 1) * w, :] = jnp.where(col == 0, sink_ref[hq] * LOG2E, band)


def _swa_block(q_ref, kp_ref, kc_ref, vp_ref, vc_ref, o_ref, bias_scr):
    w = WINDOW
    gw = SWA_GROUP * w
    first = lax.broadcasted_iota(jnp.int32, (w, SWA_KV_WIDTH), 0) == 0
    kp = jnp.where(first, jnp.zeros_like(kp_ref[...]), kp_ref[...])
    vp = jnp.where(first, jnp.zeros_like(vp_ref[...]), vp_ref[...])
    kk = jnp.concatenate([kp, kc_ref[...]], axis=0)
    vv = jnp.concatenate([vp, vc_ref[...]], axis=0)
    lane_head = lax.broadcasted_iota(jnp.int32, (w, 2 * LANES), 1) // SWA_HEAD_DIM
    key_head = lax.broadcasted_iota(jnp.int32, (2 * w, 2 * LANES), 1) // SWA_HEAD_DIM
    parts = []
    for h in range(SWA_KV_HEADS):
        for g in range(SWA_GROUP):
            qg = q_ref[:, g * 256:(g + 1) * 256]
            parts.append(jnp.where(lane_head == h, qg, jnp.zeros_like(qg)))
    qs = jnp.concatenate(parts, axis=0)
    s = lax.dot_general(qs, kk, (((1,), (1,)), ((), ())), preferred_element_type=F32)
    s = s + bias_scr[...]
    m = jnp.max(s, axis=1, keepdims=True)
    p = jnp.exp2(s - m)
    inv = 1.0 / jnp.sum(p, axis=1, keepdims=True)
    pb = p.astype(BF16)
    p_all = jnp.concatenate([pb[h * gw:(h + 1) * gw, :] for h in range(SWA_KV_HEADS)], axis=1)
    v_blk = jnp.concatenate([jnp.where(key_head == h, vv, jnp.zeros_like(vv))
                             for h in range(SWA_KV_HEADS)], axis=0)
    o = jnp.dot(p_all, v_blk, preferred_element_type=F32)
    for g in range(SWA_GROUP):
        inv_g = jnp.broadcast_to(inv[g * w:(g + 1) * w, :], (w, 2 * LANES))
        for h in range(1, SWA_KV_HEADS):
            r0 = h * gw + g * w
            inv_g = jnp.where(lane_head == h, jnp.broadcast_to(inv[r0:r0 + w, :], (w, 2 * LANES)), inv_g)
        o_ref[:, g * 256:(g + 1) * 256] = (o[g * w:(g + 1) * w, :] * inv_g).astype(BF16)


OUT_TM = 256


def _out_kernel(of_ref, gf_ref, os_ref, gs_ref, mf_ref, ms_ref, x_ref, gate_ref,
                wbf_ref, wbs_ref, wo_ref, lng_ref, lnb_ref, o_ref):
    af = (of_ref[...].astype(F32) * jax.nn.silu(gf_ref[...].astype(F32))).astype(BF16)
    yf = jnp.dot(af, wbf_ref[...], preferred_element_type=F32)
    a_s = (os_ref[...].astype(F32) * jax.nn.silu(gs_ref[...].astype(F32))).astype(BF16)
    ys = jnp.dot(a_s, wbs_ref[...], preferred_element_type=F32)
    merged = (jax.nn.sigmoid(mf_ref[...].astype(F32)) * yf
              + jax.nn.sigmoid(ms_ref[...].astype(F32)) * ys)
    sub = jnp.dot(merged.astype(BF16), wo_ref[...], preferred_element_type=F32)
    z = DEEPNORM_ALPHA * x_ref[...] + gate_ref[...] * sub
    mu = jnp.mean(z, axis=-1, keepdims=True)
    zc = z - mu
    var = jnp.mean(zc * zc, axis=-1, keepdims=True)
    o_ref[...] = zc * lax.rsqrt(var + LN_EPS) * lng_ref[...] + lnb_ref[...]


def _out(proj, o_fox, o_swa, x2, ada, wbf, wbs, wo, ln_g, ln_b):
    s = x2.shape[0]
    tm = OUT_TM
    const = lambda i: (0, 0)
    return pl.pallas_call(
        _out_kernel,
        out_shape=jax.ShapeDtypeStruct((s, D_MODEL), F32),
        grid=(s // tm,),
        in_specs=[pl.BlockSpec((tm, FOX_WIDTH), lambda i: (i, 0)),
                  pl.BlockSpec((tm, FOX_WIDTH), lambda i: (i, COL_GF // FOX_WIDTH)),
                  pl.BlockSpec((tm, SWA_WIDTH), lambda i: (i, 0)),
                  pl.BlockSpec((tm, SWA_WIDTH), lambda i: (i, COL_GS // SWA_WIDTH)),
                  pl.BlockSpec((tm, D_MODEL), lambda i: (i, COL_MF // D_MODEL)),
                  pl.BlockSpec((tm, D_MODEL), lambda i: (i, COL_MS // D_MODEL)),
                  pl.BlockSpec((tm, D_MODEL), lambda i: (i, 0)),
                  pl.BlockSpec((1, D_MODEL), lambda i: (0, 2)),
                  pl.BlockSpec((FOX_WIDTH, D_MODEL), const),
                  pl.BlockSpec((SWA_WIDTH, D_MODEL), const),
                  pl.BlockSpec((D_MODEL, D_MODEL), const),
                  pl.BlockSpec((1, D_MODEL), const),
                  pl.BlockSpec((1, D_MODEL), const)],
        out_specs=pl.BlockSpec((tm, D_MODEL), lambda i: (i, 0)),
        compiler_params=pltpu.CompilerParams(dimension_semantics=("arbitrary",),
                                             vmem_limit_bytes=VMEM_LIMIT),
        name="out",
    )(o_fox, proj, o_swa, proj, proj, proj, x2, ada, wbf, wbs, wo, ln_g, ln_b)


def kernel(x, c, w_ada, b_ada, w_in, b_f, attn_sinks, w_br_fox, w_br_swa, w_out, ln_g, ln_b):
    b, s, d = x.shape
    assert (b, s, d) == (1, SEQ, D_MODEL) and w_in.shape[0] == DEPTH
    x2 = x.reshape(s, d)

    wt = jnp.swapaxes(w_in, 1, 2)[0]
    col_scale = jnp.asarray(_proj_col_scale())
    w_main, ada, proj0, flog0 = _wprep(wt, x2, c, w_ada[0], b_ada[0].reshape(1, -1), col_scale)
    bf_pad = jnp.pad(b_f[0], (0, LANES - FOX_HEADS)).reshape(1, LANES)

    proj, flog = _proj(x2, ada, w_main, wt, col_scale, proj0, flog0)
    f_cum = _cum(flog, bf_pad)
    o_fox, o_swa, wbf, wbs, wo = _attn(proj, f_cum, attn_sinks[0], w_br_fox[0], w_br_swa[0], w_out[0])
    out = _out(proj, o_fox, o_swa, x2, ada, wbf, wbs, wo,
               ln_g[0].reshape(1, d), ln_b[0].reshape(1, d))
    return out.reshape(b, s, d)
```

```python
import math

import numpy as np
import jax
import jax.numpy as jnp
from jax import lax
from jax.experimental import pallas as pl
from jax.experimental.pallas import tpu as pltpu

F32 = jnp.float32
BF16 = jnp.bfloat16

D_MODEL = 2048
SEQ = 8192
FOX_HEADS = 8
FOX_HEAD_DIM = 128
FOX_WIDTH = FOX_HEADS * FOX_HEAD_DIM
SWA_Q_HEADS = 16
SWA_KV_HEADS = 4
SWA_HEAD_DIM = 64
SWA_GROUP = SWA_Q_HEADS // SWA_KV_HEADS
SWA_WIDTH = SWA_Q_HEADS * SWA_HEAD_DIM
SWA_KV_WIDTH = SWA_KV_HEADS * SWA_HEAD_DIM
WINDOW = 128
LN_EPS = 1e-5
NEG_INF = -1e30
DEPTH = 1
DEEPNORM_ALPHA = (2.0 * DEPTH) ** 0.25
LOG2E = math.log2(math.e)

LANES = 128
BF16_SUBLANES = 16
VMEM_LIMIT = 56 * 1024 * 1024

COL_FQ = 0
COL_FK = 1024
COL_FV = 2048
COL_SQ = 3072
COL_GF = 4096
COL_GS = 5120
COL_MF = 6144
COL_MS = 8192
COL_SK = 10240
COL_SV = 10496
PROJ_WIDTH = 10752

_O_FQ, _O_FK, _O_FV, _O_FLOG = 0, 1024, 2048, 3072
_O_SQ, _O_SK, _O_SV = 3080, 4104, 4360
_O_GF, _O_GS, _O_MF, _O_MS = 4616, 5640, 6664, 8712


def _proj_col_scale():
    s = np.ones((1, PROJ_WIDTH), np.float32)
    s[0, COL_FQ:COL_FQ + 1024] = FOX_HEAD_DIM ** -0.5 * LOG2E
    s[0, COL_SQ:COL_SQ + 1024] = SWA_HEAD_DIM ** -0.5 * LOG2E
    return s


WP_TN = 512
WP_TC = 256
WP_SHIFT = FOX_HEADS
_WP_ALIGNED, _WP_SHIFTED, _WP_PERM0, _WP_PERM1, _WP_NONE = 0, 1, 2, 3, 4
ADA_TN = 1024
ADA_STEPS = 3 * D_MODEL // ADA_TN
WP_LN_FIRST = 2 * D_MODEL // ADA_TN
PROJ_TM = 1024


def _wprep_tables():
    nb = PROJ_WIDTH // WP_TN
    blk_a = np.zeros((nb,), np.int32)
    blk_b = np.zeros((nb,), np.int32)
    blk_h = np.zeros((nb,), np.int32)
    mode = np.zeros((nb,), np.int32)

    def fill(col_out, col_src, width, shifted):
        for t in range(width // WP_TN):
            ob = col_out // WP_TN + t
            start = col_src + t * WP_TN - (WP_SHIFT if shifted else 0)
            assert start % WP_TN == 0
            mode[ob] = _WP_SHIFTED if shifted else _WP_ALIGNED
            blk_a[ob] = start // WP_TN
            blk_h[ob] = (start + WP_TN) // WP_SHIFT
            blk_b[ob] = -1

    def fill_perm(col_out, col_src):
        start = col_src - WP_SHIFT
        assert start % WP_TN == 0
        for t, m in enumerate((_WP_PERM0, _WP_PERM1)):
            ob = col_out // WP_TN + t
            mode[ob] = m
            blk_a[ob] = start // WP_TN
            blk_b[ob] = start // WP_TN + 1
            blk_h[ob] = (start + 2 * WP_TN) // WP_SHIFT
    fill(COL_FQ, _O_FQ, 3072, False)
    fill_perm(COL_SQ, _O_SQ)
    fill(COL_GF, _O_GF, 1024, True)
    fill_perm(COL_GS, _O_GS)
    fill(COL_MF, _O_MF, 2048, True)
    fill(COL_MS, _O_MS, 2048, True)
    fill(COL_SK, _O_SK, 512, True)
    for ob in range(nb):
        if blk_b[ob] < 0:
            blk_b[ob] = blk_b[ob - 1] if ob else 0
    return blk_a, blk_b, blk_h, mode


def _wprep_kernel(a_tab, b_tab, h_tab, mode_ref, a_ref, b_ref, h_ref, c_ref, wada_ref, bada_ref,
                  x_ref, wf_ref, cs_ref, o_ref, ada_ref, p_ref, flog_ref, ada_scr, h_scr):
    s = pl.program_id(0)
    mode = mode_ref[s]

    @pl.when(s < ADA_STEPS)
    def _():
        c8 = jnp.broadcast_to(c_ref[...], (8, D_MODEL))
        r = jnp.dot(c8, wada_ref[...], preferred_element_type=F32)
        val = r[0:1, :] + bada_ref[...]
        ada_ref[...] = val
        ada_scr[pl.ds(s, 1), :] = val

    @pl.when((s >= WP_LN_FIRST) & (s < ADA_STEPS))
    def _():
        ln_rows = PROJ_TM // (ADA_STEPS - WP_LN_FIRST)
        rows = pl.ds(pl.multiple_of((s - WP_LN_FIRST) * ln_rows, ln_rows), ln_rows)
        per = D_MODEL // ADA_TN
        shift = jnp.concatenate([ada_scr[k:k + 1, :] for k in range(per)], axis=1)
        scale = jnp.concatenate([ada_scr[k:k + 1, :] for k in range(per, 2 * per)], axis=1)
        x = x_ref[rows, :]
        mu = jnp.mean(x, axis=-1, keepdims=True)
        xc = x - mu
        var = jnp.mean(xc * xc, axis=-1, keepdims=True)
        h = xc * lax.rsqrt(var + LN_EPS) * (1.0 + scale) + shift
        h_scr[rows, :] = h.astype(BF16)

    def emit(rows_of_chunk):
        for c in range(D_MODEL // WP_TC):
            cols = slice(c * WP_TC, (c + 1) * WP_TC)
            o_ref[cols, :] = rows_of_chunk(cols).T.astype(BF16)
        acc = jnp.dot(h_scr[...], o_ref[...], preferred_element_type=F32)
        p_ref[...] = (acc * cs_ref[...]).astype(BF16)

    @pl.when(s == pl.num_programs(0) - 1)
    def _():
        wf = jnp.concatenate([wf_ref[...], jnp.zeros((LANES - FOX_HEADS, D_MODEL), F32)], axis=0)
        flog_ref[...] = lax.dot_general(h_scr[...], wf.astype(BF16), (((1,), (1,)), ((), ())),
                                        preferred_element_type=F32)

    @pl.when(mode == _WP_ALIGNED)
    def _():
        emit(lambda cols: a_ref[:, cols])

    @pl.when(mode == _WP_SHIFTED)
    def _():
        emit(lambda cols: jnp.concatenate([a_ref[WP_SHIFT:, cols], h_ref[:, cols]], axis=0))

    def perm_rows(cols, t):
        parts = []
        for g in (2 * t, 2 * t + 1):
            for h in range(SWA_KV_HEADS):
                r0 = WP_SHIFT + (h * SWA_GROUP + g) * SWA_HEAD_DIM
                r1 = r0 + SWA_HEAD_DIM
                if r1 <= WP_TN:
                    parts.append(a_ref[r0:r1, cols])
                elif r0 >= WP_TN:
                    if r1 <= 2 * WP_TN:
                        parts.append(b_ref[r0 - WP_TN:r1 - WP_TN, cols])
                    else:
                        parts.append(jnp.concatenate([b_ref[r0 - WP_TN:, cols], h_ref[:, cols]], axis=0))
                else:
                    parts.append(jnp.concatenate([a_ref[r0:, cols], b_ref[:r1 - WP_TN, cols]], axis=0))
        return jnp.concatenate(parts, axis=0)

    @pl.when(mode == _WP_PERM0)
    def _():
        emit(lambda cols: perm_rows(cols, 0))

    @pl.when(mode == _WP_PERM1)
    def _():
        emit(lambda cols: perm_rows(cols, 1))


def _wprep(wt, x2, c, w_ada, b_ada, col_scale):
    blk_a, blk_b, blk_h, mode = _wprep_tables()
    n_ada = w_ada.shape[1]
    assert n_ada == ADA_STEPS * ADA_TN and WP_LN_FIRST < ADA_STEPS
    nb = PROJ_WIDTH // WP_TN

    def per_step(t, lead):
        return jnp.asarray(np.concatenate([np.full((ADA_STEPS,), lead, np.int32), t]))
    tabs = (per_step(blk_a, blk_a[0]), per_step(blk_b, blk_b[0]), per_step(blk_h, blk_h[0]),
            per_step(mode, _WP_NONE))
    ada_blk = lambda i, a, b, h, m: (0, jnp.minimum(i, ADA_STEPS - 1))
    w_blk = lambda i, a, b, h, m: (0, jnp.maximum(i - ADA_STEPS, 0))
    grid_spec = pltpu.PrefetchScalarGridSpec(
        num_scalar_prefetch=4,
        grid=(ADA_STEPS + nb,),
        in_specs=[pl.BlockSpec((WP_TN, D_MODEL), lambda i, a, b, h, m: (a[i], 0)),
                  pl.BlockSpec((WP_TN, D_MODEL), lambda i, a, b, h, m: (b[i], 0)),
                  pl.BlockSpec((WP_SHIFT, D_MODEL), lambda i, a, b, h, m: (h[i], 0)),
                  pl.BlockSpec((1, D_MODEL), lambda i, a, b, h, m: (0, 0)),
                  pl.BlockSpec((D_MODEL, ADA_TN), ada_blk),
                  pl.BlockSpec((1, ADA_TN), ada_blk),
                  pl.BlockSpec((PROJ_TM, D_MODEL), lambda i, a, b, h, m: (0, 0), pipeline_mode=pl.Buffered(1)),
                  pl.BlockSpec((FOX_HEADS, D_MODEL), lambda i, a, b, h, m: (_O_FLOG // FOX_HEADS, 0)),
                  pl.BlockSpec((1, WP_TN), w_blk)],
        out_specs=(pl.BlockSpec((D_MODEL, WP_TN), w_blk),
                   pl.BlockSpec((1, ADA_TN), ada_blk),
                   pl.BlockSpec((PROJ_TM, WP_TN), w_blk),
                   pl.BlockSpec((PROJ_TM, LANES), lambda i, a, b, h, m: (0, 0))),
        scratch_shapes=[pltpu.VMEM((16, ADA_TN), F32), pltpu.VMEM((PROJ_TM, D_MODEL), BF16)],
    )
    return pl.pallas_call(
        _wprep_kernel,
        out_shape=(jax.ShapeDtypeStruct((D_MODEL, PROJ_WIDTH), BF16),
                   jax.ShapeDtypeStruct((1, n_ada), F32),
                   jax.ShapeDtypeStruct((x2.shape[0], PROJ_WIDTH), BF16),
                   jax.ShapeDtypeStruct((x2.shape[0], LANES), F32)),
        grid_spec=grid_spec,
        compiler_params=pltpu.CompilerParams(dimension_semantics=("arbitrary",),
                                             vmem_limit_bytes=VMEM_LIMIT),
        name="wprep",
    )(*tabs, wt, wt, wt, c, w_ada, b_ada, x2, wt, col_scale)


PROJ_TN = 1536
PROJ_NJ = PROJ_WIDTH // PROJ_TN
PROJ_LN_ROWS = 160
assert (PROJ_NJ * PROJ_LN_ROWS >= PROJ_TM and PROJ_LN_ROWS % BF16_SUBLANES == 0
        and (PROJ_TM - PROJ_LN_ROWS) % BF16_SUBLANES == 0)


def _proj_kernel(x_ref, shift_ref, scale_ref, w_ref, wf_ref, cs_ref, p0_ref, f0_ref, o_ref, flog_ref,
                 h0_scr, h1_scr):
    r = pl.program_id(0)
    j = pl.program_id(1)

    rows = pl.ds(pl.multiple_of(jnp.minimum(j * PROJ_LN_ROWS, PROJ_TM - PROJ_LN_ROWS), BF16_SUBLANES),
                 PROJ_LN_ROWS)

    def layer_norm_slice(h_dst):
        x = x_ref[rows, :]
        mu = jnp.mean(x, axis=-1, keepdims=True)
        xc = x - mu
        var = jnp.mean(xc * xc, axis=-1, keepdims=True)
        h = xc * lax.rsqrt(var + LN_EPS) * (1.0 + scale_ref[...]) + shift_ref[...]
        h_dst[rows, :] = h.astype(BF16)

    def project(h_src):
        acc = jnp.dot(h_src[...], w_ref[...], preferred_element_type=F32)
        o_ref[...] = (acc * cs_ref[...]).astype(BF16)
        wf = jnp.concatenate([wf_ref[...], jnp.zeros((LANES - FOX_HEADS, D_MODEL), F32)], axis=0)
        flog_ref[rows, :] = lax.dot_general(h_src[rows, :], wf.astype(BF16), (((1,), (1,)), ((), ())),
                                            preferred_element_type=F32)

    @pl.when(r == 0)
    def _():
        layer_norm_slice(h1_scr)

    @pl.when((r > 0) & (r % 2 == 1))
    def _():
        layer_norm_slice(h0_scr)
        project(h1_scr)

    @pl.when((r > 0) & (r % 2 == 0))
    def _():
        layer_norm_slice(h1_scr)
        project(h0_scr)


def _proj(x2, ada, w_main, wt, col_scale, proj0, flog0):
    s = x2.shape[0]
    nb = s // PROJ_TM
    cur = lambda r: jnp.maximum(r, 1)
    col = lambda r, j: jnp.where(r == 0, 0, j)
    return pl.pallas_call(
        _proj_kernel,
        out_shape=(jax.ShapeDtypeStruct((s, PROJ_WIDTH), BF16),
                   jax.ShapeDtypeStruct((s, LANES), F32)),
        grid=(nb, PROJ_NJ),
        in_specs=[pl.BlockSpec((PROJ_TM, D_MODEL), lambda r, j: (jnp.minimum(r + 1, nb - 1), 0)),
                  pl.BlockSpec((1, D_MODEL), lambda r, j: (0, 0)),
                  pl.BlockSpec((1, D_MODEL), lambda r, j: (0, 1)),
                  pl.BlockSpec((D_MODEL, PROJ_TN), lambda r, j: (0, col(r, j))),
                  pl.BlockSpec((FOX_HEADS, D_MODEL), lambda r, j: (_O_FLOG // FOX_HEADS, 0)),
                  pl.BlockSpec((1, PROJ_TN), lambda r, j: (0, col(r, j))),
                  pl.BlockSpec(memory_space=pl.ANY),
                  pl.BlockSpec(memory_space=pl.ANY)],
        out_specs=(pl.BlockSpec((PROJ_TM, PROJ_TN), lambda r, j: (cur(r), col(r, j))),
                   pl.BlockSpec((PROJ_TM, LANES), lambda r, j: (cur(r), 0))),
        scratch_shapes=[pltpu.VMEM((PROJ_TM, D_MODEL), BF16), pltpu.VMEM((PROJ_TM, D_MODEL), BF16)],
        input_output_aliases={6: 0, 7: 1},
        compiler_params=pltpu.CompilerParams(dimension_semantics=("arbitrary", "arbitrary"),
                                             vmem_limit_bytes=VMEM_LIMIT),
        name="proj",
    )(x2, ada, ada, w_main, wt, col_scale, proj0, flog0)


def _cum_kernel(flog_ref, bf_ref, f_ref):
    s = flog_ref.shape[0]
    lf = jax.nn.log_sigmoid(flog_ref[...] + bf_ref[...])
    acc = lf.T[0:FOX_HEADS, :]
    lane = lax.broadcasted_iota(jnp.int32, acc.shape, 1)
    sh = 1
    while sh < s:
        rolled = pltpu.roll(acc, sh, axis=1)
        acc = acc + jnp.where(lane >= sh, rolled, 0.0)
        sh *= 2
    for h in range(FOX_HEADS):
        f_ref[h] = acc[h:h + 1, :]


def _cum(flog, bf_pad):
    s = flog.shape[0]
    return pl.pallas_call(
        _cum_kernel,
        out_shape=jax.ShapeDtypeStruct((FOX_HEADS, 1, s), F32),
        in_specs=[pl.BlockSpec((s, LANES), lambda: (0, 0)),
                  pl.BlockSpec((1, LANES), lambda: (0, 0))],
        out_specs=pl.BlockSpec((FOX_HEADS, 1, s), lambda: (0, 0, 0)),
        compiler_params=pltpu.CompilerParams(vmem_limit_bytes=VMEM_LIMIT),
        name="cum",
    )(flog, bf_pad)


FOX_TK = 512
FOX_TQ = 2 * FOX_TK
FOX_SKIP_LOG2 = 152.0


def _attn_kernel(sink_ref, q_ref, k_ref, v_ref, f_ref, sq_ref, skp_ref, skc_ref, svp_ref, svc_ref,
                 wbf32_ref, wbs32_ref, wo32_ref,
                 o_ref, os_ref, wbf_ref, wbs_ref, wo_ref,
                 sa_scr, sb_scr, m_scr, acc_scr, kn_scr, bias_scr):
    i = pl.program_id(1)
    n_blk = pl.program_id(0) * pl.num_programs(1) + i
    tq, tk = FOX_TQ, FOX_TK
    s_len = k_ref.shape[0]

    @pl.when(n_blk <= 1)
    def _():
        _swa_tables(n_blk, sink_ref, bias_scr)

    @pl.when(i == 0)
    def _():
        kn_scr[...] = jnp.zeros(kn_scr.shape, F32)

    q0 = pl.multiple_of(i * tq, tq)
    f_q = f_ref[0, :, pl.ds(q0, tq)]
    f_base = jnp.max(f_q, axis=1, keepdims=True)
    ones = jnp.ones((tk, LANES), BF16)
    top, bot = slice(0, tk), slice(tk, tq)

    m_scr[...] = jnp.full(m_scr.shape, NEG_INF, F32)
    acc_scr[...] = jnp.zeros(acc_scr.shape, F32)

    def scores(k0, dst, rows=slice(0, FOX_TQ)):
        k_t = k_ref[pl.ds(k0, tk), :]
        s = lax.dot_general(q_ref[rows, :], k_t, (((1,), (1,)), ((), ())), preferred_element_type=F32)
        bias = (f_base - f_ref[0, :, pl.ds(k0, tk)]) * LOG2E
        dst[rows, :] = s + bias

    def softmax_pv(src, k0, rows=slice(0, FOX_TQ), causal=False):
        s = src[rows, :]
        if causal:
            n = rows.stop - rows.start
            keep = (lax.broadcasted_iota(jnp.int32, (n, tk), 1)
                    <= lax.broadcasted_iota(jnp.int32, (n, tk), 0))
            s = jnp.where(keep, s, NEG_INF)
        m_prev = m_scr[rows, :]
        m_new = jnp.maximum(m_prev, jnp.max(s, axis=1, keepdims=True))
        alpha = jnp.exp2(m_prev - m_new)
        p = jnp.exp2(s - jnp.tile(m_new, (1, tk // LANES))).astype(BF16)
        v_aug = jnp.concatenate([v_ref[pl.ds(k0, tk), :], ones], axis=1)
        pv = jnp.dot(p, v_aug, preferred_element_type=F32)
        acc_scr[rows, :] = acc_scr[rows, :] * jnp.tile(alpha, (1, 2)) + pv
        m_scr[rows, :] = m_new

    kd = pl.multiple_of(q0 + tk, tk)
    scores(q0, sa_scr)
    scores(kd, sb_scr, bot)
    softmax_pv(sa_scr, q0, top, causal=True)
    softmax_pv(sa_scr, q0, bot)
    softmax_pv(sb_scr, kd, bot, causal=True)
    scores(pl.multiple_of(jnp.maximum(q0 - tq, 0), tq), sa_scr)

    _swa_block(sq_ref, skp_ref, skc_ref, svp_ref, svc_ref, os_ref, bias_scr)

    wbf_ref[...] = wbf32_ref[...].astype(BF16)
    wbs_ref[...] = wbs32_ref[...].astype(BF16)
    wo_ref[...] = wo32_ref[...].astype(BF16)

    qf = q_ref[...].astype(F32)
    qn2 = jnp.max(jnp.sum(qf * qf, axis=1, keepdims=True), axis=0, keepdims=True)
    m_low = jnp.min(jnp.min(m_scr[...], axis=0, keepdims=True), axis=1, keepdims=True)
    kn2 = kn_scr[0:1, 0:1]
    thr = m_low - FOX_SKIP_LOG2 - jnp.sqrt(qn2 * kn2)
    kf = k_ref[pl.ds(q0, tq), :].astype(F32)
    kn2_tile = jnp.max(jnp.sum(kf * kf, axis=1, keepdims=True), axis=0, keepdims=True)
    kn_scr[...] = jnp.broadcast_to(jnp.maximum(kn2, kn2_tile), kn_scr.shape)
    pos = lax.broadcasted_iota(jnp.int32, (1, s_len), 1)
    live = ((f_base - f_ref[0]) * LOG2E >= thr) & (pos < q0)
    n_live = jnp.sum(live.astype(jnp.int32), axis=1, keepdims=True)
    n_pairs = (n_live[0, 0] + (tq - 1)) // tq

    def pair(t):
        ka = pl.multiple_of(q0 - (t + 1) * tq, tq)
        kb = pl.multiple_of(ka + tk, tk)
        scores(kb, sb_scr)
        softmax_pv(sa_scr, ka)
        scores(pl.multiple_of(jnp.maximum(ka - tq, 0), tq), sa_scr)
        softmax_pv(sb_scr, kb)

    def two_pairs(u, carry):
        pair(2 * u)
        pair(2 * u + 1)
        return carry

    lax.fori_loop(0, n_pairs // 2, two_pairs, 0)

    @pl.when(n_pairs % 2 == 1)
    def _():
        pair(n_pairs - 1)

    acc = acc_scr[...]
    o_ref[...] = (acc[:, :FOX_HEAD_DIM] / acc[:, FOX_HEAD_DIM:]).astype(BF16)


def _attn(proj, f3, sinks, w_br_fox, w_br_swa, w_out):
    s = proj.shape[0]
    tq, tk, w = FOX_TQ, FOX_TK, WINDOW
    nq = s // tq
    nsteps = FOX_HEADS * nq
    assert nsteps == s // w
    kcol = COL_SK // SWA_KV_WIDTH
    vcol = COL_SV // SWA_KV_WIDTH
    blk = lambda h, i: h * nq + i
    prv = lambda h, i: jnp.maximum(h * nq + i - 1, 0)
    rb = FOX_WIDTH // nsteps
    ro = D_MODEL // nsteps
    per64 = SWA_HEAD_DIM // rb
    assert rb % BF16_SUBLANES == 0 and SWA_HEAD_DIM % rb == 0

    def swa_src(h, i):
        n = blk(h, i)
        g, hk = (n // per64) // SWA_KV_HEADS, (n // per64) % SWA_KV_HEADS
        return (hk * SWA_GROUP + g) * per64 + n % per64
    grid_spec = pltpu.PrefetchScalarGridSpec(
        num_scalar_prefetch=1,
        grid=(FOX_HEADS, nq),
        in_specs=[pl.BlockSpec((tq, LANES), lambda h, i, sk: (i, COL_FQ // LANES + h)),
                  pl.BlockSpec((s, LANES), lambda h, i, sk: (0, COL_FK // LANES + h)),
                  pl.BlockSpec((s, LANES), lambda h, i, sk: (0, COL_FV // LANES + h)),
                  pl.BlockSpec((1, 1, s), lambda h, i, sk: (h, 0, 0)),
                  pl.BlockSpec((w, SWA_WIDTH), lambda h, i, sk: (blk(h, i), COL_SQ // SWA_WIDTH)),
                  pl.BlockSpec((w, SWA_KV_WIDTH), lambda h, i, sk: (prv(h, i), kcol)),
                  pl.BlockSpec((w, SWA_KV_WIDTH), lambda h, i, sk: (blk(h, i), kcol)),
                  pl.BlockSpec((w, SWA_KV_WIDTH), lambda h, i, sk: (prv(h, i), vcol)),
                  pl.BlockSpec((w, SWA_KV_WIDTH), lambda h, i, sk: (blk(h, i), vcol)),
                  pl.BlockSpec((rb, D_MODEL), lambda h, i, sk: (blk(h, i), 0)),
                  pl.BlockSpec((rb, D_MODEL), lambda h, i, sk: (swa_src(h, i), 0)),
                  pl.BlockSpec((ro, D_MODEL), lambda h, i, sk: (blk(h, i), 0))],
        out_specs=(pl.BlockSpec((tq, LANES), lambda h, i, sk: (i, h)),
                   pl.BlockSpec((w, SWA_WIDTH), lambda h, i, sk: (blk(h, i), 0)),
                   pl.BlockSpec((rb, D_MODEL), lambda h, i, sk: (blk(h, i), 0)),
                   pl.BlockSpec((rb, D_MODEL), lambda h, i, sk: (blk(h, i), 0)),
                   pl.BlockSpec((ro, D_MODEL), lambda h, i, sk: (blk(h, i), 0))),
        scratch_shapes=[pltpu.VMEM((tq, tk), F32), pltpu.VMEM((tq, tk), F32),
                        pltpu.VMEM((tq, LANES), F32), pltpu.VMEM((tq, 2 * LANES), F32),
                        pltpu.VMEM((8, LANES), F32),
                        pltpu.VMEM((SWA_Q_HEADS * w, 2 * w), F32)],
    )
    return pl.pallas_call(
        _attn_kernel,
        out_shape=(jax.ShapeDtypeStruct((s, FOX_WIDTH), BF16),
                   jax.ShapeDtypeStruct((s, SWA_WIDTH), BF16),
                   jax.ShapeDtypeStruct((FOX_WIDTH, D_MODEL), BF16),
                   jax.ShapeDtypeStruct((SWA_WIDTH, D_MODEL), BF16),
                   jax.ShapeDtypeStruct((D_MODEL, D_MODEL), BF16)),
        grid_spec=grid_spec,
        compiler_params=pltpu.CompilerParams(dimension_semantics=("arbitrary", "arbitrary"),
                                             vmem_limit_bytes=VMEM_LIMIT),
        name="attn",
    )(sinks, proj, proj, proj, f3, proj, proj, proj, proj, proj, w_br_fox, w_br_swa, w_out)


def _swa_tables(n, sink_ref, bias_scr):
    w = WINDOW
    row = lax.broadcasted_iota(jnp.int32, (w, 2 * w), 0)
    col = lax.broadcasted_iota(jnp.int32, (w, 2 * w), 1)
    dist = row - col + w
    valid = (dist >= 0) & (dist < w) & ((col >= w) | (n > 0))
    distf = dist.astype(F32)
    for hq in range(SWA_Q_HEADS):
        slope = 2.0 ** (-8.0 * (hq + 1.0) / SWA_Q_HEADS)
        band = jnp.where(valid, (-slope * LOG2E) * distf, NEG_INF)
        bias_scr[hq * w:(hq + 1) * w, :] = jnp.where(col == 0, sink_ref[hq] * LOG2E, band)


def _swa_block(q_ref, kp_ref, kc_ref, vp_ref, vc_ref, o_ref, bias_scr):
    w = WINDOW
    gw = SWA_GROUP * w
    first = lax.broadcasted_iota(jnp.int32, (w, SWA_KV_WIDTH), 0) == 0
    kp = jnp.where(first, jnp.zeros_like(kp_ref[...]), kp_ref[...])
    vp = jnp.where(first, jnp.zeros_like(vp_ref[...]), vp_ref[...])
    kk = jnp.concatenate([kp, kc_ref[...]], axis=0)
    vv = jnp.concatenate([vp, vc_ref[...]], axis=0)
    lane_head = lax.broadcasted_iota(jnp.int32, (w, 2 * LANES), 1) // SWA_HEAD_DIM
    key_head = lax.broadcasted_iota(jnp.int32, (2 * w, 2 * LANES), 1) // SWA_HEAD_DIM
    parts = []
    for h in range(SWA_KV_HEADS):
        for g in range(SWA_GROUP):
            qg = q_ref[:, g * 256:(g + 1) * 256]
            parts.append(jnp.where(lane_head == h, qg, jnp.zeros_like(qg)))
    qs = jnp.concatenate(parts, axis=0)
    s = lax.dot_general(qs, kk, (((1,), (1,)), ((), ())), preferred_element_type=F32)
    s = s + bias_scr[...]
    m = jnp.max(s, axis=1, keepdims=True)
    p = jnp.exp2(s - m)
    inv = 1.0 / jnp.sum(p, axis=1, keepdims=True)
    pb = p.astype(BF16)
    p_all = jnp.concatenate([pb[h * gw:(h + 1) * gw, :] for h in range(SWA_KV_HEADS)], axis=1)
    v_blk = jnp.concatenate([jnp.where(key_head == h, vv, jnp.zeros_like(vv))
                             for h in range(SWA_KV_HEADS)], axis=0)
    o = jnp.dot(p_all, v_blk, preferred_element_type=F32)
    for g in range(SWA_GROUP):
        inv_g = jnp.broadcast_to(inv[g * w:(g + 1) * w, :], (w, 2 * LANES))
        for h in range(1, SWA_KV_HEADS):
            r0 = h * gw + g * w
            inv_g = jnp.where(lane_head == h, jnp.broadcast_to(inv[r0:r0 + w, :], (w, 2 * LANES)), inv_g)
        o_ref[:, g * 256:(g + 1) * 256] = (o[g * w:(g + 1) * w, :] * inv_g).astype(BF16)


OUT_TM = 256


def _out_kernel(of_ref, gf_ref, os_ref, gs_ref, mf_ref, ms_ref, x_ref, gate_ref,
                wbf_ref, wbs_ref, wo_ref, lng_ref, lnb_ref, o_ref):
    af = (of_ref[...].astype(F32) * jax.nn.silu(gf_ref[...].astype(F32))).astype(BF16)
    yf = jnp.dot(af, wbf_ref[...], preferred_element_type=F32)
    a_s = (os_ref[...].astype(F32) * jax.nn.silu(gs_ref[...].astype(F32))).astype(BF16)
    ys = jnp.dot(a_s, wbs_ref[...], preferred_element_type=F32)
    merged = (jax.nn.sigmoid(mf_ref[...].astype(F32)) * yf
              + jax.nn.sigmoid(ms_ref[...].astype(F32)) * ys)
    sub = jnp.dot(merged.astype(BF16), wo_ref[...], preferred_element_type=F32)
    z = DEEPNORM_ALPHA * x_ref[...] + gate_ref[...] * sub
    mu = jnp.mean(z, axis=-1, keepdims=True)
    zc = z - mu
    var = jnp.mean(zc * zc, axis=-1, keepdims=True)
    o_ref[...] = zc * lax.rsqrt(var + LN_EPS) * lng_ref[...] + lnb_ref[...]


def _out(proj, o_fox, o_swa, x2, ada, wbf, wbs, wo, ln_g, ln_b):
    s = x2.shape[0]
    tm = OUT_TM
    const = lambda i: (0, 0)
    return pl.pallas_call(
        _out_kernel,
        out_shape=jax.ShapeDtypeStruct((s, D_MODEL), F32),
        grid=(s // tm,),
        in_specs=[pl.BlockSpec((tm, FOX_WIDTH), lambda i: (i, 0)),
                  pl.BlockSpec((tm, FOX_WIDTH), lambda i: (i, COL_GF // FOX_WIDTH)),
                  pl.BlockSpec((tm, SWA_WIDTH), lambda i: (i, 0)),
                  pl.BlockSpec((tm, SWA_WIDTH), lambda i: (i, COL_GS // SWA_WIDTH)),
                  pl.BlockSpec((tm, D_MODEL), lambda i: (i, COL_MF // D_MODEL)),
                  pl.BlockSpec((tm, D_MODEL), lambda i: (i, COL_MS // D_MODEL)),
                  pl.BlockSpec((tm, D_MODEL), lambda i: (i, 0)),
                  pl.BlockSpec((1, D_MODEL), lambda i: (0, 2)),
                  pl.BlockSpec((FOX_WIDTH, D_MODEL), const),
                  pl.BlockSpec((SWA_WIDTH, D_MODEL), const),
                  pl.BlockSpec((D_MODEL, D_MODEL), const),
                  pl.BlockSpec((1, D_MODEL), const),
                  pl.BlockSpec((1, D_MODEL), const)],
        out_specs=pl.BlockSpec((tm, D_MODEL), lambda i: (i, 0)),
        compiler_params=pltpu.CompilerParams(dimension_semantics=("arbitrary",),
                                             vmem_limit_bytes=VMEM_LIMIT),
        name="out",
    )(o_fox, proj, o_swa, proj, proj, proj, x2, ada, wbf, wbs, wo, ln_g, ln_b)


def kernel(x, c, w_ada, b_ada, w_in, b_f, attn_sinks, w_br_fox, w_br_swa, w_out, ln_g, ln_b):
    b, s, d = x.shape
    assert (b, s, d) == (1, SEQ, D_MODEL) and w_in.shape[0] == DEPTH
    x2 = x.reshape(s, d)

    wt = jnp.swapaxes(w_in, 1, 2)[0]
    col_scale = jnp.asarray(_proj_col_scale())
    w_main, ada, proj0, flog0 = _wprep(wt, x2, c, w_ada[0], b_ada[0].reshape(1, -1), col_scale)
    bf_pad = jnp.pad(b_f[0], (0, LANES - FOX_HEADS)).reshape(1, LANES)

    proj, flog = _proj(x2, ada, w_main, wt, col_scale, proj0, flog0)
    f_cum = _cum(flog, bf_pad)
    o_fox, o_swa, wbf, wbs, wo = _attn(proj, f_cum, attn_sinks[0], w_br_fox[0], w_br_swa[0], w_out[0])
    out = _out(proj, o_fox, o_swa, x2, ada, wbf, wbs, wo,
               ln_g[0].reshape(1, d), ln_b[0].reshape(1, d))
    return out.reshape(b, s, d)
```
